```python
import math
import jax, jax.numpy as jnp
from jax import lax
import numpy as np

D_MODEL = 1024
BATCH = 8
SEQ = 2048
DEPTH = 1

N_MEM = 256
A_HEADS = 4
A_HEAD_DIM = 64
A_V_DIM = 2 * A_HEAD_DIM
B_GROUPS = ((128, 1), (512, 4), (2048, 16))
B_HEADS_PER_GROUP = 4
B_HEAD_DIM = 128
C_HEADS = 4
C_HEAD_DIM = 128
N_BRANCHES = 3
BRANCH_WIDTH = 512
D_FF = -(-8 * D_MODEL // (3 * 256)) * 256
Q_BLOCK = 128
EPS = 1e-6
ALIBI_MAX_BIAS = 8.0

A_QK_COLS = A_HEADS * 2 * A_HEAD_DIM
A_V_COLS = A_HEADS * A_V_DIM
B_COLS = len(B_GROUPS) * B_HEADS_PER_GROUP * B_HEAD_DIM
C_Q_COLS = C_HEADS * C_HEAD_DIM
IN_SIZES = (A_QK_COLS, A_QK_COLS, A_V_COLS, B_COLS, B_COLS, B_COLS, C_Q_COLS)
D_IN = sum(IN_SIZES)

kernel_name = 'gated_hybrid_diff_dilated_memory_block'


def rms_norm(x, g):
    xf = x.astype(jnp.float32)
    y = xf * lax.rsqrt(jnp.mean(xf * xf, axis=-1, keepdims=True) + EPS)
    return (y * g.astype(jnp.float32)).astype(x.dtype)


def alibi_slopes(n):
    return jnp.exp2(-ALIBI_MAX_BIAS * jnp.arange(1, n + 1, dtype=jnp.float32) / n)


def diff_attention(q, k, v, slopes, lam):
    b, t, h, _, dh = q.shape
    nq = t // Q_BLOCK
    scale = dh ** -0.5
    pos_k = jnp.arange(t, dtype=jnp.int32)
    q_blocks = q.reshape(b, nq, Q_BLOCK, h, 2, dh).transpose(1, 0, 2, 3, 4, 5)
    starts = jnp.arange(nq, dtype=jnp.int32) * Q_BLOCK

    def one_block(args):
        qb, start = args
        s = jnp.einsum('bqhcd,bkhcd->bhcqk', qb, k).astype(jnp.float32) * scale
        pos_q = start + jnp.arange(Q_BLOCK, dtype=jnp.int32)
        dist = jnp.abs(pos_q[:, None] - pos_k[None, :]).astype(jnp.float32)
        s = s - slopes[None, :, None, None, None] * dist[None, None, None]
        p = jax.nn.softmax(s, axis=-1)
        p_diff = p[:, :, 0] - lam * p[:, :, 1]
        return jnp.einsum('bhqk,bkhe->bqhe', p_diff.astype(v.dtype), v)

    o = lax.map(one_block, (q_blocks, starts))
    return o.transpose(1, 0, 2, 3, 4).reshape(b, t, h, v.shape[-1])


def dilated_window_attention(q, k, v, slopes, window, dilation):
    b, t, h, hd = q.shape
    n_side = window // (2 * dilation)
    sub_len = t // dilation
    blk = n_side
    nb = -(-sub_len // blk)
    pad = nb * blk - sub_len

    def to_sub(a):
        return a.reshape(b, sub_len, dilation, h, hd).transpose(0, 2, 3, 1, 4)

    qs = jnp.pad(to_sub(q), ((0, 0), (0, 0), (0, 0), (0, pad), (0, 0)))
    qs = qs.reshape(b, dilation, h, nb, blk, hd)

    def windows(a):
        ap = jnp.pad(to_sub(a), ((0, 0), (0, 0), (0, 0), (blk, blk + pad), (0, 0)))
        ap = ap.reshape(b, dilation, h, nb + 2, blk, hd)
        return jnp.concatenate([ap[:, :, :, :-2], ap[:, :, :, 1:-1], ap[:, :, :, 2:]], axis=4)

    kw, vw = windows(k), windows(v)
    s = jnp.einsum('brhnqd,brhnkd->brhnqk', qs, kw).astype(jnp.float32) * hd ** -0.5
    q_idx = jnp.arange(nb)[:, None, None] * blk + jnp.arange(blk)[None, :, None]
    k_idx = (jnp.arange(nb)[:, None, None] - 1) * blk + jnp.arange(3 * blk)[None, None, :]
    rel = k_idx - q_idx
    valid = (jnp.abs(rel) <= n_side) & (k_idx >= 0) & (k_idx < sub_len)
    dist = (dilation * jnp.abs(rel)).astype(jnp.float32)
    s = s - slopes[None, None, :, None, None, None] * dist
    s = jnp.where(valid, s, -jnp.inf)
    lse = jax.nn.logsumexp(s, axis=-1)
    p = jnp.exp(s - lse[..., None])
    o = jnp.einsum('brhnqk,brhnkd->brhnqd', p.astype(v.dtype), vw)
    o = o.reshape(b, dilation, h, nb * blk, hd)[:, :, :, :sub_len]
    lse = lse.reshape(b, dilation, h, nb * blk)[:, :, :, :sub_len]
    o = o.transpose(0, 3, 1, 2, 4).reshape(b, t, h, hd)
    lse = lse.transpose(0, 3, 1, 2).reshape(b, t, h)
    return o, lse


def memory_attention(q, k, v):
    s = jnp.einsum('bthd,bnhd->bhtn', q, k).astype(jnp.float32) * q.shape[-1] ** -0.5
    p = jax.nn.softmax(s, axis=-1)
    return jnp.einsum('bhtn,bnhd->bthd', p.astype(v.dtype), v)


def setup_inputs(seed: int = 0) -> dict:
    key = jax.random.key(seed)
    ks = jax.random.split(key, 32)
    L = DEPTH

    def nrm(k, shape, fan_in):
        return jax.random.normal(k, shape, jnp.float32) * fan_in ** -0.5

    def gain(k, shape):
        return 1.0 + 0.02 * jax.random.normal(k, shape, jnp.float32)

    def small(k, shape, scale):
        return scale * jax.random.normal(k, shape, jnp.float32)

    return {
        'x': jax.random.normal(ks[0], (BATCH, SEQ, D_MODEL), jnp.float32),
        'mem': jax.random.normal(ks[1], (BATCH, N_MEM, D_MODEL), jnp.float32),
        'norm_mix': gain(ks[2], (L, D_MODEL)),
        'w_in': nrm(ks[3], (L, D_MODEL, D_IN), D_MODEL),
        'w_gate': nrm(ks[4], (L, D_MODEL, N_BRANCHES * D_MODEL), D_MODEL),
        'b_gate': small(ks[5], (L, N_BRANCHES * D_MODEL), 0.02),
        'a_q_norm': gain(ks[6], (L, A_HEAD_DIM)),
        'a_k_norm': gain(ks[7], (L, A_HEAD_DIM)),
        'a_lambda_q1': small(ks[8], (L, A_HEAD_DIM), 0.1),
        'a_lambda_k1': small(ks[9], (L, A_HEAD_DIM), 0.1),
        'a_lambda_q2': small(ks[10], (L, A_HEAD_DIM), 0.1),
        'a_lambda_k2': small(ks[11], (L, A_HEAD_DIM), 0.1),
        'a_subln': gain(ks[12], (L, A_V_DIM)),
        'b_q_norm': gain(ks[13], (L, B_HEAD_DIM)),
        'b_k_norm': gain(ks[14], (L, B_HEAD_DIM)),
        'mem_norm': gain(ks[15], (L, D_MODEL)),
        'w_mem_kv': nrm(ks[16], (L, D_MODEL, 2 * C_HEADS * C_HEAD_DIM), D_MODEL),
        'c_q_norm': gain(ks[17], (L, C_HEAD_DIM)),
        'c_k_norm': gain(ks[18], (L, C_HEAD_DIM)),
        'w_branch': nrm(ks[19], (L, N_BRANCHES, BRANCH_WIDTH, D_MODEL), BRANCH_WIDTH),
        'w_out': nrm(ks[20], (L, D_MODEL, D_MODEL), D_MODEL),
        'norm_ffn': gain(ks[21], (L, D_MODEL)),
        'w_ffn_gate': nrm(ks[22], (L, D_MODEL, D_FF), D_MODEL),
        'w_ffn_up': nrm(ks[23], (L, D_MODEL, D_FF), D_MODEL),
        'w_ffn_down': nrm(ks[24], (L, D_FF, D_MODEL), D_FF),
    }


def reference(x, mem, norm_mix, w_in, w_gate, b_gate, a_q_norm, a_k_norm, a_lambda_q1, a_lambda_k1,
              a_lambda_q2, a_lambda_k2, a_subln, b_q_norm, b_k_norm, mem_norm, w_mem_kv, c_q_norm,
              c_k_norm, w_branch, w_out, norm_ffn, w_ffn_gate, w_ffn_up, w_ffn_down):
    b, t, d = x.shape
    n_mem = mem.shape[1]
    n_groups = len(B_GROUPS)
    offsets = [int(o) for o in np.cumsum(IN_SIZES)[:-1]]
    slopes_a = alibi_slopes(A_HEADS)
    slopes_b = alibi_slopes(n_groups * B_HEADS_PER_GROUP).reshape(n_groups, B_HEADS_PER_GROUP)
    for l in range(DEPTH):
        lambda_init = 0.8 - 0.6 * math.exp(-0.3 * l)
        h = rms_norm(x, norm_mix[l])
        aq, ak, av, bq, bk, bv, cq = jnp.split(h @ w_in[l], offsets, axis=-1)

        aq = rms_norm(aq.reshape(b, t, A_HEADS, 2, A_HEAD_DIM), a_q_norm[l])
        ak = rms_norm(ak.reshape(b, t, A_HEADS, 2, A_HEAD_DIM), a_k_norm[l])
        av = av.reshape(b, t, A_HEADS, A_V_DIM)
        lam = (jnp.exp(jnp.sum(a_lambda_q1[l] * a_lambda_k1[l]).astype(jnp.float32))
               - jnp.exp(jnp.sum(a_lambda_q2[l] * a_lambda_k2[l]).astype(jnp.float32)) + lambda_init)
        out_a = diff_attention(aq, ak, av, slopes_a, lam)
        out_a = (rms_norm(out_a, a_subln[l]) * (1.0 - lambda_init)).reshape(b, t, BRANCH_WIDTH)

        bq = rms_norm(bq.reshape(b, t, n_groups, B_HEADS_PER_GROUP, B_HEAD_DIM), b_q_norm[l])
        bk = rms_norm(bk.reshape(b, t, n_groups, B_HEADS_PER_GROUP, B_HEAD_DIM), b_k_norm[l])
        bv = bv.reshape(b, t, n_groups, B_HEADS_PER_GROUP, B_HEAD_DIM)
        outs, lses = [], []
        for g, (window, dilation) in enumerate(B_GROUPS):
            o, lse = dilated_window_attention(bq[:, :, g], bk[:, :, g], bv[:, :, g], slopes_b[g], window, dilation)
            outs.append(o)
            lses.append(lse)
        wts = jax.nn.softmax(jnp.stack(lses, axis=2), axis=2)
        out_b = jnp.sum(wts[..., None].astype(bv.dtype) * jnp.stack(outs, axis=2), axis=2)
        out_b = out_b.reshape(b, t, BRANCH_WIDTH)

        m = rms_norm(mem, mem_norm[l])
        kv = (m @ w_mem_kv[l]).reshape(b, n_mem, 2, C_HEADS, C_HEAD_DIM)
        ck = rms_norm(kv[:, :, 0], c_k_norm[l])
        cv = kv[:, :, 1]
        cq = rms_norm(cq.reshape(b, t, C_HEADS, C_HEAD_DIM), c_q_norm[l])
        out_c = memory_attention(cq, ck, cv).reshape(b, t, BRANCH_WIDTH)

        branches = jnp.einsum('btgc,gcd->btgd', jnp.stack([out_a, out_b, out_c], axis=2), w_branch[l])
        gates = jax.nn.sigmoid(h @ w_gate[l] + b_gate[l]).reshape(b, t, N_BRANCHES, d)
        x = x + jnp.sum(gates * branches, axis=2) @ w_out[l]

        h2 = rms_norm(x, norm_ffn[l])
        x = x + (jax.nn.silu(h2 @ w_ffn_gate[l]) * (h2 @ w_ffn_up[l])) @ w_ffn_down[l]
    return x
```

```python
import functools
import math

import jax
import jax.numpy as jnp
from jax import lax
from jax.experimental import pallas as pl
from jax.experimental.pallas import tpu as pltpu

F32 = jnp.float32
BF16 = jnp.bfloat16

D_MODEL = 1024
A_HEADS = 4
A_HEAD_DIM = 64
A_V_DIM = 2 * A_HEAD_DIM
B_GROUPS = ((128, 1), (512, 4), (2048, 16))
B_HEADS = 4
B_HEAD_DIM = 128
C_HEADS = 4
C_HEAD_DIM = 128
BRANCH_WIDTH = 512
N_BRANCHES = 3
EPS = 1e-6
ALIBI_MAX_BIAS = 8.0
LAMBDA_INIT = 0.8 - 0.6 * math.exp(-0.3 * 0)

LANES = 128
VMEM_LIMIT_BYTES = 56 * 1024 * 1024

NAT_COLS = 7 * BRANCH_WIDTH
GRP_COLS = 3 * BRANCH_WIDTH
NEG_BIG = -1e30


def _rms(x, gain):
    return x * lax.rsqrt(jnp.mean(x * x, axis=-1, keepdims=True) + EPS) * gain


def _dot(a, b):
    return jnp.dot(a, b, preferred_element_type=F32)


def _dot_nt(a, b):
    return lax.dot_general(a, b, (((1,), (1,)), ((), ())), preferred_element_type=F32)


def _const_spec(shape):
    nd = len(shape)
    return pl.BlockSpec(shape, lambda *_: (0,) * nd)


def _proj_kernel(x_ref, g_ref, wn_ref, w1_ref, w2_ref, on_ref, o1_ref, o2_ref, slab_ref, hp_ref, *, tm):
    h = _rms(x_ref[0], g_ref[...])
    hb = h.astype(BF16)
    for c in range(0, NAT_COLS, BRANCH_WIDTH):
        on_ref[0, :, c:c + BRANCH_WIDTH] = _dot(hb, wn_ref[:, c:c + BRANCH_WIDTH]).astype(BF16)
    n_slabs = D_MODEL // LANES
    for s in range(n_slabs):
        slab_ref[s] = h[:, s * LANES:(s + 1) * LANES]
    for dil, w_ref, o_ref in ((4, w1_ref, o1_ref), (16, w2_ref, o2_ref)):
        n = tm // dil
        for r in range(dil):
            for s in range(n_slabs):
                hp_ref[r * n:(r + 1) * n, s * LANES:(s + 1) * LANES] = (
                    slab_ref[s, pl.ds(r, n, stride=dil), :].astype(BF16))
        hp = hp_ref[...]
        for c in range(0, GRP_COLS, BRANCH_WIDTH):
            res = _dot(hp, w_ref[:, c:c + BRANCH_WIDTH]).astype(BF16)
            for r in range(dil):
                o_ref[0, r, :, c:c + BRANCH_WIDTH] = res[r * n:(r + 1) * n]


def _project(x, norm_g, w_nat, w_g1, w_g2, tm=512):
    b, t, d = x.shape
    grid = (b, t // tm)
    return pl.pallas_call(
        functools.partial(_proj_kernel, tm=tm),
        grid=grid,
        in_specs=[
            pl.BlockSpec((1, tm, d), lambda i, j: (i, j, 0)),
            _const_spec((1, d)),
            _const_spec((d, NAT_COLS)),
            _const_spec((d, GRP_COLS)),
            _const_spec((d, GRP_COLS)),
        ],
        out_specs=[
            pl.BlockSpec((1, tm, NAT_COLS), lambda i, j: (i, j, 0)),
            pl.BlockSpec((1, 4, tm // 4, GRP_COLS), lambda i, j: (i, 0, j, 0)),
            pl.BlockSpec((1, 16, tm // 16, GRP_COLS), lambda i, j: (i, 0, j, 0)),
        ],
        out_shape=[
            jax.ShapeDtypeStruct((b, t, NAT_COLS), BF16),
            jax.ShapeDtypeStruct((b, 4, t // 4, GRP_COLS), BF16),
            jax.ShapeDtypeStruct((b, 16, t // 16, GRP_COLS), BF16),
        ],
        scratch_shapes=[
            pltpu.VMEM((d // LANES, tm, LANES), F32),
            pltpu.VMEM((tm, d), BF16),
        ],
        compiler_params=pltpu.CompilerParams(
            dimension_semantics=("arbitrary", "arbitrary"), vmem_limit_bytes=VMEM_LIMIT_BYTES),
        name="proj",
    )(x, norm_g, w_nat, w_g1, w_g2)


def _half_rms(x, gain2, lo_mask):
    sq = x * x
    s_all = jnp.sum(sq, axis=-1, keepdims=True)
    s_lo = jnp.sum(jnp.where(lo_mask, sq, 0.0), axis=-1, keepdims=True)
    ms = jnp.where(lo_mask, s_lo, s_all - s_lo) * (1.0 / A_HEAD_DIM)
    return x * lax.rsqrt(ms + EPS) * gain2


def _attn_a_kernel(slopes_ref, q_ref, k_ref, v_ref, gq_ref, gk_ref, lq1_ref, lk1_ref, lq2_ref, lk2_ref,
                   sub_ref, o_ref, kn_ref, tab_ref, *, t, qb):
    h = pl.program_id(1)
    slope = slopes_ref[h]
    lo_mask = lax.broadcasted_iota(jnp.int32, (1, LANES), 1) < A_HEAD_DIM
    scale = A_HEAD_DIM ** -0.5

    lam = (jnp.exp(jnp.sum(lq1_ref[...] * lk1_ref[...], keepdims=True))
           - jnp.exp(jnp.sum(lq2_ref[...] * lk2_ref[...], keepdims=True)) + LAMBDA_INIT)

    kn_ref[...] = _half_rms(k_ref[...].astype(F32), gk_ref[...], lo_mask).astype(BF16)

    il = lax.broadcasted_iota(jnp.int32, (qb, 2 * t - qb), 0)
    cc = lax.broadcasted_iota(jnp.int32, (qb, 2 * t - qb), 1)
    tab_ref[...] = -slope * jnp.abs(il - (cc - (t - qb))).astype(F32)

    def body(j, carry):
        i0 = pl.multiple_of(j * qb, qb)
        q = _half_rms(q_ref[pl.ds(i0, qb), :].astype(F32), gq_ref[...], lo_mask) * scale
        q2 = jnp.concatenate([jnp.where(lo_mask, q, 0.0), jnp.where(lo_mask, 0.0, q)], axis=0).astype(BF16)
        s = _dot_nt(q2, kn_ref[...])
        off = pl.multiple_of(t - qb - i0, qb)
        bias = tab_ref[:, pl.ds(off, t)]
        s = s + jnp.concatenate([bias, bias], axis=0)
        m = jnp.max(s, axis=-1, keepdims=True)
        e = jnp.exp(s - m)
        inv = 1.0 / jnp.sum(e, axis=-1, keepdims=True)
        p = e[:qb] * inv[:qb] - e[qb:] * (lam * inv[qb:])
        o = _dot(p.astype(BF16), v_ref[...])
        o = _rms(o, sub_ref[...]) * (1.0 - LAMBDA_INIT)
        o_ref[pl.ds(i0, qb), :] = o.astype(BF16)
        return carry

    lax.fori_loop(0, t // qb, body, 0)


def _attn_a(qkv_nat, slopes, gq2, gk2, lq1, lk1, lq2, lk2, subln, b, t, qb=128):
    n = b * t
    vec = lambda w: _const_spec((1, w))
    return pl.pallas_call(
        functools.partial(_attn_a_kernel, t=t, qb=qb),
        grid=(b, A_HEADS),
        in_specs=[
            pl.BlockSpec(memory_space=pltpu.SMEM),
            pl.BlockSpec((t, LANES), lambda i, h: (i, h)),
            pl.BlockSpec((t, LANES), lambda i, h: (i, A_HEADS + h)),
            pl.BlockSpec((t, LANES), lambda i, h: (i, 2 * A_HEADS + h)),
            vec(LANES), vec(LANES), vec(A_HEAD_DIM), vec(A_HEAD_DIM), vec(A_HEAD_DIM), vec(A_HEAD_DIM),
            vec(A_V_DIM),
        ],
        out_specs=pl.BlockSpec((t, LANES), lambda i, h: (i, h)),
        out_shape=jax.ShapeDtypeStruct((n, BRANCH_WIDTH), BF16),
        scratch_shapes=[
            pltpu.VMEM((t, LANES), BF16),
            pltpu.VMEM((qb, 2 * t - qb), F32),
        ],
        compiler_params=pltpu.CompilerParams(
            dimension_semantics=("arbitrary", "arbitrary"), vmem_limit_bytes=VMEM_LIMIT_BYTES),
        name="attn_a",
    )(slopes, qkv_nat, qkv_nat, qkv_nat, gq2, gk2, lq1, lk1, lq2, lk2, subln)


def _attn_b_kernel(slopes_ref, q0_ref, k0_ref, v0_ref, q1_ref, k1_ref, v1_ref, q2_ref, k2_ref, v2_ref,
                   gq_ref, gk_ref, o_ref, qn_ref, kn_ref, og_ref, lse_ref, *, t, qb):
    h = pl.program_id(1)
    scale = B_HEAD_DIM ** -0.5
    refs = ((q0_ref, k0_ref, v0_ref), (q1_ref, k1_ref, v1_ref), (q2_ref, k2_ref, v2_ref))
    for g, (window, dil) in enumerate(B_GROUPS):
        q_ref, k_ref, v_ref = refs[g]
        n_side = window // (2 * dil)
        sub = t // dil
        wk = min(2 * qb, sub)
        slope = slopes_ref[g, h] * float(dil)
        qn_ref[...] = (_rms(q_ref[...].astype(F32), gq_ref[...]) * scale).astype(BF16)
        kn_ref[...] = _rms(k_ref[...].astype(F32), gk_ref[...]).astype(BF16)

        biases = {}

        def bias_for(offset):
            if offset not in biases:
                ql = lax.broadcasted_iota(jnp.int32, (qb, wk), 0)
                kl = lax.broadcasted_iota(jnp.int32, (qb, wk), 1)
                dist = jnp.abs(kl - ql - offset)
                biases[offset] = jnp.where(dist <= n_side, -slope * dist.astype(F32), NEG_BIG)
            return biases[offset]

        for r in range(dil):
            for j in range(sub // qb):
                i0 = j * qb
                ws = min(max(i0 - n_side, 0), sub - wk)
                base = r * sub
                s = _dot_nt(qn_ref[base + i0:base + i0 + qb, :], kn_ref[base + ws:base + ws + wk, :])
                s = s + bias_for(i0 - ws)
                m = jnp.max(s, axis=-1, keepdims=True)
                e = jnp.exp(s - m)
                l = jnp.sum(e, axis=-1, keepdims=True)
                o = _dot(e.astype(BF16), v_ref[base + ws:base + ws + wk, :]) * (1.0 / l)
                lse = jnp.broadcast_to(m + jnp.log(l), (qb, LANES))
                if dil == 1:
                    og_ref[g, i0:i0 + qb, :] = o
                    lse_ref[g, i0:i0 + qb, :] = lse
                else:
                    og_ref[g, pl.ds(r + dil * i0, qb, stride=dil), :] = o
                    lse_ref[g, pl.ds(r + dil * i0, qb, stride=dil), :] = lse

    l0, l1, l2 = lse_ref[0], lse_ref[1], lse_ref[2]
    m = jnp.maximum(jnp.maximum(l0, l1), l2)
    e0, e1, e2 = jnp.exp(l0 - m), jnp.exp(l1 - m), jnp.exp(l2 - m)
    inv = 1.0 / (e0 + e1 + e2)
    o_ref[...] = ((e0 * og_ref[0] + e1 * og_ref[1] + e2 * og_ref[2]) * inv).astype(BF16)


def _attn_b(qkv_nat, qkv_g1, qkv_g2, slopes, gq, gk, b, t, qb=128):
    n = b * t
    blk = lambda off: pl.BlockSpec((t, LANES), lambda i, h: (i, off + h))
    return pl.pallas_call(
        functools.partial(_attn_b_kernel, t=t, qb=qb),
        grid=(b, B_HEADS),
        in_specs=[
            pl.BlockSpec(memory_space=pltpu.SMEM),
            blk(12), blk(16), blk(20),
            blk(0), blk(4), blk(8),
            blk(0), blk(4), blk(8),
            _const_spec((1, LANES)), _const_spec((1, LANES)),
        ],
        out_specs=pl.BlockSpec((t, LANES), lambda i, h: (i, h)),
        out_shape=jax.ShapeDtypeStruct((n, BRANCH_WIDTH), BF16),
        scratch_shapes=[
            pltpu.VMEM((t, LANES), BF16),
            pltpu.VMEM((t, LANES), BF16),
            pltpu.VMEM((3, t, LANES), F32),
            pltpu.VMEM((3, t, LANES), F32),
        ],
        compiler_params=pltpu.CompilerParams(
            dimension_semantics=("arbitrary", "arbitrary"), vmem_limit_bytes=VMEM_LIMIT_BYTES),
        name="attn_b",
    )(slopes, qkv_nat, qkv_nat, qkv_nat, qkv_g1, qkv_g1, qkv_g1, qkv_g2, qkv_g2, qkv_g2, gq, gk)


def _attn_c_kernel(q_ref, mem_ref, gm_ref, wk_ref, wv_ref, gq_ref, gk_ref, o_ref, *, t, tq):
    scale = C_HEAD_DIM ** -0.5
    mn = _rms(mem_ref[0], gm_ref[...]).astype(BF16)
    ck = _rms(_dot(mn, wk_ref[...]), gk_ref[...]).astype(BF16)
    cv = _dot(mn, wv_ref[...]).astype(BF16)

    def body(j, carry):
        i0 = pl.multiple_of(j * tq, tq)
        q = (_rms(q_ref[pl.ds(i0, tq), :].astype(F32), gq_ref[...]) * scale).astype(BF16)
        s = _dot_nt(q, ck)
        m = jnp.max(s, axis=-1, keepdims=True)
        e = jnp.exp(s - m)
        inv = 1.0 / jnp.sum(e, axis=-1, keepdims=True)
        o_ref[pl.ds(i0, tq), :] = (_dot(e.astype(BF16), cv) * inv).astype(BF16)
        return carry

    lax.fori_loop(0, t // tq, body, 0)


def _attn_c(qkv_nat, mem, gm, w_kv, gq, gk, b, t, tq=256):
    n = b * t
    n_mem, d = mem.shape[1], mem.shape[2]
    return pl.pallas_call(
        functools.partial(_attn_c_kernel, t=t, tq=tq),
        grid=(b, C_HEADS),
        in_specs=[
            pl.BlockSpec((t, LANES), lambda i, h: (i, 24 + h)),
            pl.BlockSpec((1, n_mem, d), lambda i, h: (i, 0, 0)),
            _const_spec((1, d)),
            pl.BlockSpec((d, LANES), lambda i, h: (0, h)),
            pl.BlockSpec((d, LANES), lambda i, h: (0, C_HEADS + h)),
            _const_spec((1, LANES)), _const_spec((1, LANES)),
        ],
        out_specs=pl.BlockSpec((t, LANES), lambda i, h: (i, h)),
        out_shape=jax.ShapeDtypeStruct((n, BRANCH_WIDTH), BF16),
        compiler_params=pltpu.CompilerParams(
            dimension_semantics=("arbitrary", "arbitrary"), vmem_limit_bytes=VMEM_LIMIT_BYTES),
        name="attn_c",
    )(qkv_nat, mem, gm, w_kv, w_kv, gq, gk)


def _merge_kernel(x_ref, a_ref, b_ref, c_ref, g_ref, wg_ref, bg_ref, wb_ref, wo_ref, o_ref):
    x = x_ref[...]
    hb = _rms(x, g_ref[...]).astype(BF16)
    d = x.shape[-1]
    acc = None
    for g, br_ref in enumerate((a_ref, b_ref, c_ref)):
        gate = jax.nn.sigmoid(_dot(hb, wg_ref[:, g * d:(g + 1) * d]) + bg_ref[:, g * d:(g + 1) * d])
        term = gate * _dot(br_ref[...], wb_ref[g])
        acc = term if acc is None else acc + term
    o_ref[...] = x + _dot(acc.astype(BF16), wo_ref[...])


def _merge(x2, out_a, out_b, out_c, norm_g, w_gate, b_gate, w_branch, w_out, tm=256):
    n, d = x2.shape
    row = lambda w: pl.BlockSpec((tm, w), lambda i: (i, 0))
    return pl.pallas_call(
        _merge_kernel,
        grid=(n // tm,),
        in_specs=[
            row(d), row(BRANCH_WIDTH), row(BRANCH_WIDTH), row(BRANCH_WIDTH),
            _const_spec((1, d)),
            _const_spec((d, N_BRANCHES * d)),
            _const_spec((1, N_BRANCHES * d)),
            _const_spec((N_BRANCHES, BRANCH_WIDTH, d)),
            _const_spec((d, d)),
        ],
        out_specs=row(d),
        out_shape=jax.ShapeDtypeStruct((n, d), F32),
        compiler_params=pltpu.CompilerParams(
            dimension_semantics=("arbitrary",), vmem_limit_bytes=VMEM_LIMIT_BYTES),
        name="merge",
    )(x2, out_a, out_b, out_c, norm_g, w_gate, b_gate, w_branch, w_out)


def _ffn_kernel(x_ref, g_ref, wg_ref, wu_ref, wd_ref, o_ref, *, fc):
    x = x_ref[...]
    hb = _rms(x, g_ref[...]).astype(BF16)
    d_ff = wg_ref.shape[1]
    acc = x
    for c in range(0, d_ff, fc):
        gt = _dot(hb, wg_ref[:, c:c + fc])
        up = _dot(hb, wu_ref[:, c:c + fc])
        acc = acc + _dot((jax.nn.silu(gt) * up).astype(BF16), wd_ref[c:c + fc, :])
    o_ref[...] = acc


def _ffn(x2, norm_g, w_gate, w_up, w_down, tm=256):
    n, d = x2.shape
    d_ff = w_gate.shape[1]
    fc = d_ff // 2
    row = pl.BlockSpec((tm, d), lambda i: (i, 0))
    return pl.pallas_call(
        functools.partial(_ffn_kernel, fc=fc),
        grid=(n // tm,),
        in_specs=[row, _const_spec((1, d)), _const_spec((d, d_ff)), _const_spec((d, d_ff)),
                  _const_spec((d_ff, d))],
        out_specs=row,
        out_shape=jax.ShapeDtypeStruct((n, d), F32),
        compiler_params=pltpu.CompilerParams(
            dimension_semantics=("arbitrary",), vmem_limit_bytes=VMEM_LIMIT_BYTES),
        name="ffn",
    )(x2, norm_g, w_gate, w_up, w_down)


def kernel(x, mem, norm_mix, w_in, w_gate, b_gate, a_q_norm, a_k_norm, a_lambda_q1, a_lambda_k1, a_lambda_q2,
           a_lambda_k2, a_subln, b_q_norm, b_k_norm, mem_norm, w_mem_kv, c_q_norm, c_k_norm, w_branch, w_out,
           norm_ffn, w_ffn_gate, w_ffn_up, w_ffn_down):
    b, t, d = x.shape
    n = b * t
    n_groups = len(B_GROUPS)
    slopes_a = jnp.exp2(-ALIBI_MAX_BIAS * jnp.arange(1, A_HEADS + 1, dtype=F32) / A_HEADS)
    nb = n_groups * B_HEADS
    slopes_b = jnp.exp2(-ALIBI_MAX_BIAS * jnp.arange(1, nb + 1, dtype=F32) / nb).reshape(n_groups, B_HEADS)

    l = 0
    bw = BRANCH_WIDTH
    w = w_in[l].astype(BF16)
    bq, bk, bv = 3 * bw, 3 * bw + 3 * bw, 3 * bw + 6 * bw
    cq = 3 * bw + 9 * bw
    grp = lambda g: jnp.concatenate(
        [w[:, bq + g * bw:bq + (g + 1) * bw], w[:, bk + g * bw:bk + (g + 1) * bw],
         w[:, bv + g * bw:bv + (g + 1) * bw]], axis=1)
    w_nat = jnp.concatenate([w[:, :3 * bw], grp(0), w[:, cq:cq + bw]], axis=1)
    row = lambda v: v.reshape(1, -1)
    twice = lambda v: jnp.concatenate([v, v]).reshape(1, -1)

    qkv_nat, qkv_g1, qkv_g2 = _project(x, row(norm_mix[l]), w_nat, grp(1), grp(2))
    qkv_nat = qkv_nat.reshape(n, NAT_COLS)
    qkv_g1 = qkv_g1.reshape(n, GRP_COLS)
    qkv_g2 = qkv_g2.reshape(n, GRP_COLS)

    out_a = _attn_a(qkv_nat, slopes_a, twice(a_q_norm[l]), twice(a_k_norm[l]), row(a_lambda_q1[l]),
                    row(a_lambda_k1[l]), row(a_lambda_q2[l]), row(a_lambda_k2[l]), row(a_subln[l]), b, t)
    out_b = _attn_b(qkv_nat, qkv_g1, qkv_g2, slopes_b, row(b_q_norm[l]), row(b_k_norm[l]), b, t)
    out_c = _attn_c(qkv_nat, mem, row(mem_norm[l]), w_mem_kv[l].astype(BF16), row(c_q_norm[l]),
                    row(c_k_norm[l]), b, t)

    x2 = x.reshape(n, d)
    x2 = _merge(x2, out_a, out_b, out_c, row(norm_mix[l]), w_gate[l].astype(BF16), row(b_gate[l]),
                w_branch[l].astype(BF16), w_out[l].astype(BF16))
    x2 = _ffn(x2, row(norm_ffn[l]), w_ffn_gate[l].astype(BF16), w_ffn_up[l].astype(BF16),
              w_ffn_down[l].astype(BF16))
    return x2.reshape(b, t, d)
```

```python
import functools
import math

import jax
import jax.numpy as jnp
from jax import lax
from jax.experimental import pallas as pl
from jax.experimental.pallas import tpu as pltpu

F32 = jnp.float32
BF16 = jnp.bfloat16

D_MODEL = 1024
A_HEADS = 4
A_HEAD_DIM = 64
A_V_DIM = 2 * A_HEAD_DIM
B_GROUPS = ((128, 1), (512, 4), (2048, 16))
B_HEADS = 4
B_HEAD_DIM = 128
C_HEADS = 4
C_HEAD_DIM = 128
BRANCH_WIDTH = 512
N_BRANCHES = 3
EPS = 1e-6
ALIBI_MAX_BIAS = 8.0
LAMBDA_INIT = 0.8 - 0.6 * math.exp(-0.3 * 0)

LANES = 128
VMEM_LIMIT_BYTES = 56 * 1024 * 1024

NAT_COLS = 7 * BRANCH_WIDTH
GRP_COLS = 3 * BRANCH_WIDTH
NEG_BIG = -1e30
LOG2E = 1.4426950408889634
VT_PAD = 16
MAX_SAFE_SCORE_BOUND = 40.0


def _rms(x, gain):
    return x * lax.rsqrt(jnp.mean(x * x, axis=-1, keepdims=True) + EPS) * gain


def _dot(a, b):
    return jnp.dot(a, b, preferred_element_type=F32)


def _dot_nt(a, b):
    return lax.dot_general(a, b, (((1,), (1,)), ((), ())), preferred_element_type=F32)


def _const_spec(shape):
    nd = len(shape)
    return pl.BlockSpec(shape, lambda *_: (0,) * nd)


def _proj_kernel(x_ref, g_ref, wn_ref, w1_ref, w2_ref, on_ref, o1_ref, o2_ref, slab_ref, hp_ref, *, tm):
    h = _rms(x_ref[0], g_ref[...])
    hb = h.astype(BF16)
    for c in range(0, NAT_COLS, BRANCH_WIDTH):
        on_ref[0, :, c:c + BRANCH_WIDTH] = _dot(hb, wn_ref[:, c:c + BRANCH_WIDTH]).astype(BF16)
    n_slabs = D_MODEL // LANES
    for s in range(n_slabs):
        slab_ref[s] = h[:, s * LANES:(s + 1) * LANES]
    for dil, w_ref, o_ref in ((4, w1_ref, o1_ref), (16, w2_ref, o2_ref)):
        n = tm // dil
        for r in range(dil):
            for s in range(n_slabs):
                hp_ref[r * n:(r + 1) * n, s * LANES:(s + 1) * LANES] = (
                    slab_ref[s, pl.ds(r, n, stride=dil), :].astype(BF16))
        hp = hp_ref[...]
        for c in range(0, GRP_COLS, BRANCH_WIDTH):
            res = _dot(hp, w_ref[:, c:c + BRANCH_WIDTH]).astype(BF16)
            for r in range(dil):
                o_ref[0, r, :, c:c + BRANCH_WIDTH] = res[r * n:(r + 1) * n]


def _project(x, norm_g, w_nat, w_g1, w_g2, tm=512):
    b, t, d = x.shape
    grid = (b, t // tm)
    return pl.pallas_call(
        functools.partial(_proj_kernel, tm=tm),
        grid=grid,
        in_specs=[
            pl.BlockSpec((1, tm, d), lambda i, j: (i, j, 0)),
            _const_spec((1, d)),
            _const_spec((d, NAT_COLS)),
            _const_spec((d, GRP_COLS)),
            _const_spec((d, GRP_COLS)),
        ],
        out_specs=[
            pl.BlockSpec((1, tm, NAT_COLS), lambda i, j: (i, j, 0)),
            pl.BlockSpec((1, 4, tm // 4, GRP_COLS), lambda i, j: (i, 0, j, 0)),
            pl.BlockSpec((1, 16, tm // 16, GRP_COLS), lambda i, j: (i, 0, j, 0)),
        ],
        out_shape=[
            jax.ShapeDtypeStruct((b, t, NAT_COLS), BF16),
            jax.ShapeDtypeStruct((b, 4, t // 4, GRP_COLS), BF16),
            jax.ShapeDtypeStruct((b, 16, t // 16, GRP_COLS), BF16),
        ],
        scratch_shapes=[
            pltpu.VMEM((d // LANES, tm, LANES), F32),
            pltpu.VMEM((tm, d), BF16),
        ],
        compiler_params=pltpu.CompilerParams(
            dimension_semantics=("arbitrary", "arbitrary"), vmem_limit_bytes=VMEM_LIMIT_BYTES),
        name="proj",
    )(x, norm_g, w_nat, w_g1, w_g2)


def _half_rms(x, gain2, lo_mask):
    sq = x * x
    s_all = jnp.sum(sq, axis=-1, keepdims=True)
    s_lo = jnp.sum(jnp.where(lo_mask, sq, 0.0), axis=-1, keepdims=True)
    ms = jnp.where(lo_mask, s_lo, s_all - s_lo) * (1.0 / A_HEAD_DIM)
    return x * lax.rsqrt(ms + EPS) * gain2


def _attn_a_kernel(ctl_ref, q_ref, k_ref, v_ref, gq_ref, gk_ref, lq1_ref, lk1_ref, lq2_ref, lk2_ref,
                   sub_ref, o_ref, kn_ref, vt_ref, tab_ref, q2_ref, et_ref, ot_ref, l_ref, *, t, qb, kc):
    h = pl.program_id(0)
    nblk = t // qb
    slope2 = ctl_ref[h] * LOG2E
    shift = ctl_ref[A_HEADS]
    fast = ctl_ref[A_HEADS + 1] > 0.5
    lo_mask = lax.broadcasted_iota(jnp.int32, (1, LANES), 1) < A_HEAD_DIM
    qscale = A_HEAD_DIM ** -0.5 * LOG2E

    lam = (jnp.exp(jnp.sum(lq1_ref[...] * lk1_ref[...], keepdims=True))
           - jnp.exp(jnp.sum(lq2_ref[...] * lk2_ref[...], keepdims=True)) + LAMBDA_INIT)

    kn_ref[...] = _half_rms(k_ref[...].astype(F32), gk_ref[...], lo_mask).astype(BF16)

    eye = (lax.broadcasted_iota(jnp.int32, (LANES, LANES), 0)
           == lax.broadcasted_iota(jnp.int32, (LANES, LANES), 1)).astype(BF16)
    vt_ref[...] = _dot_nt(eye, v_ref[...]).astype(BF16)

    q = _half_rms(q_ref[...].astype(F32), gq_ref[...], lo_mask) * qscale
    q_lo = jnp.where(lo_mask, q, 0.0).astype(BF16)
    q_hi = jnp.where(lo_mask, 0.0, q).astype(BF16)
    for j in range(nblk):
        q2_ref[j, :qb, :] = q_lo[j * qb:(j + 1) * qb]
        q2_ref[j, qb:, :] = q_hi[j * qb:(j + 1) * qb]

    @pl.when(pl.program_id(1) == 0)
    def _():
        cc = lax.broadcasted_iota(jnp.int32, (2 * t - qb, qb), 0)
        il = lax.broadcasted_iota(jnp.int32, (2 * t - qb, qb), 1)
        tab_ref[...] = (-slope2 * jnp.abs(cc - (t - qb) - il).astype(F32)
                        - jnp.where(fast, shift, 0.0))

    def fold8(x, op):
        return op(x.reshape(x.shape[0] // 8, 8, x.shape[1]), axis=0)

    def block(j, slot, exact_max):
        q2 = q2_ref[j]
        off = pl.multiple_of(t - qb - j * qb, qb)

        def scores(c):
            s = _dot_nt(kn_ref[c * kc:(c + 1) * kc, :], q2)
            bias = tab_ref[pl.ds(off + c * kc, kc), :]
            return s + jnp.concatenate([bias, bias], axis=1)

        m = None
        if exact_max:
            for c in range(t // kc):
                cm = fold8(scores(c), jnp.max)
                m = cm if m is None else jnp.maximum(m, cm)
            m = jnp.max(m, axis=0, keepdims=True)
        acc = None
        for c in range(t // kc):
            s = scores(c)
            e = jnp.exp2(s - m if exact_max else s)
            et_ref[slot, c * kc:(c + 1) * kc, :] = e.astype(BF16)
            part = fold8(e, jnp.sum)
            acc = part if acc is None else acc + part
        l_ref[pl.ds(j, 1), :] = jnp.sum(acc, axis=0, keepdims=True)
        ot_ref[j] = _dot(vt_ref[...], et_ref[slot])

    def run(exact_max):
        def pair(i, carry):
            block(2 * i, 0, exact_max)
            block(2 * i + 1, 1, exact_max)
            return carry
        lax.fori_loop(0, nblk // 2, pair, 0)

    @pl.when(fast)
    def _():
        run(False)

    @pl.when(jnp.logical_not(fast))
    def _():
        run(True)

    for j in range(nblk):
        ot = ot_ref[j]
        inv = 1.0 / l_ref[j:j + 1, :]
        o = ot[:, :qb] * inv[:, :qb] - ot[:, qb:] * (lam * inv[:, qb:])
        o = _rms(o.T, sub_ref[...]) * (1.0 - LAMBDA_INIT)
        o_ref[j * qb:(j + 1) * qb, :] = o.astype(BF16)


def _attn_a(qkv_nat, ctl, gq2, gk2, lq1, lk1, lq2, lk2, subln, b, t, qb=128, kc=512):
    n = b * t
    nblk = t // qb
    vec = lambda w: _const_spec((1, w))
    return pl.pallas_call(
        functools.partial(_attn_a_kernel, t=t, qb=qb, kc=kc),
        grid=(A_HEADS, b),
        in_specs=[
            pl.BlockSpec(memory_space=pltpu.SMEM),
            pl.BlockSpec((t, LANES), lambda h, i: (i, h)),
            pl.BlockSpec((t, LANES), lambda h, i: (i, A_HEADS + h)),
            pl.BlockSpec((t, LANES), lambda h, i: (i, 2 * A_HEADS + h)),
            vec(LANES), vec(LANES), vec(A_HEAD_DIM), vec(A_HEAD_DIM), vec(A_HEAD_DIM), vec(A_HEAD_DIM),
            vec(A_V_DIM),
        ],
        out_specs=pl.BlockSpec((t, LANES), lambda h, i: (i, h)),
        out_shape=jax.ShapeDtypeStruct((n, BRANCH_WIDTH), BF16),
        scratch_shapes=[
            pltpu.VMEM((t, LANES), BF16),
            pltpu.VMEM((LANES, t), BF16),
            pltpu.VMEM((2 * t - qb, qb), F32),
            pltpu.VMEM((nblk, 2 * qb, LANES), BF16),
            pltpu.VMEM((2, t, 2 * qb), BF16),
            pltpu.VMEM((nblk, LANES, 2 * qb), F32),
            pltpu.VMEM((nblk, 2 * qb), F32),
        ],
        compiler_params=pltpu.CompilerParams(
            dimension_semantics=("arbitrary", "arbitrary"), vmem_limit_bytes=VMEM_LIMIT_BYTES),
        name="attn_a",
    )(ctl, qkv_nat, qkv_nat, qkv_nat, gq2, gk2, lq1, lk1, lq2, lk2, subln)


def _attn_b_kernel(slopes_ref, q0_ref, k0_ref, v0_ref, q1_ref, k1_ref, v1_ref, q2_ref, k2_ref, v2_ref,
                   gq_ref, gk_ref, o_ref, qn_ref, kn_ref, og_ref, lse_ref, *, t, qb):
    h = pl.program_id(1)
    scale = B_HEAD_DIM ** -0.5
    refs = ((q0_ref, k0_ref, v0_ref), (q1_ref, k1_ref, v1_ref), (q2_ref, k2_ref, v2_ref))
    for g, (window, dil) in enumerate(B_GROUPS):
        q_ref, k_ref, v_ref = refs[g]
        n_side = window // (2 * dil)
        sub = t // dil
        wk = min(2 * qb, sub)
        slope = slopes_ref[g, h] * float(dil)
        qn_ref[...] = (_rms(q_ref[...].astype(F32), gq_ref[...]) * scale).astype(BF16)
        kn_ref[...] = _rms(k_ref[...].astype(F32), gk_ref[...]).astype(BF16)

        biases = {}

        def bias_for(offset):
            if offset not in biases:
                ql = lax.broadcasted_iota(jnp.int32, (qb, wk), 0)
                kl = lax.broadcasted_iota(jnp.int32, (qb, wk), 1)
                dist = jnp.abs(kl - ql - offset)
                biases[offset] = jnp.where(dist <= n_side, -slope * dist.astype(F32), NEG_BIG)
            return biases[offset]

        for r in range(dil):
            for j in range(sub // qb):
                i0 = j * qb
                ws = min(max(i0 - n_side, 0), sub - wk)
                base = r * sub
                s = _dot_nt(qn_ref[base + i0:base + i0 + qb, :], kn_ref[base + ws:base + ws + wk, :])
                s = s + bias_for(i0 - ws)
                m = jnp.max(s, axis=-1, keepdims=True)
                e = jnp.exp(s - m)
                l = jnp.sum(e, axis=-1, keepdims=True)
                o = _dot(e.astype(BF16), v_ref[base + ws:base + ws + wk, :]) * (1.0 / l)
                lse = jnp.broadcast_to(m + jnp.log(l), (qb, LANES))
                if dil == 1:
                    og_ref[g, i0:i0 + qb, :] = o
                    lse_ref[g, i0:i0 + qb, :] = lse
                else:
                    og_ref[g, pl.ds(r + dil * i0, qb, stride=dil), :] = o
                    lse_ref[g, pl.ds(r + dil * i0, qb, stride=dil), :] = lse

    l0, l1, l2 = lse_ref[0], lse_ref[1], lse_ref[2]
    m = jnp.maximum(jnp.maximum(l0, l1), l2)
    e0, e1, e2 = jnp.exp(l0 - m), jnp.exp(l1 - m), jnp.exp(l2 - m)
    inv = 1.0 / (e0 + e1 + e2)
    o_ref[...] = ((e0 * og_ref[0] + e1 * og_ref[1] + e2 * og_ref[2]) * inv).astype(BF16)


def _attn_b(qkv_nat, qkv_g1, qkv_g2, slopes, gq, gk, b, t, qb=128):
    n = b * t
    blk = lambda off: pl.BlockSpec((t, LANES), lambda i, h: (i, off + h))
    return pl.pallas_call(
        functools.partial(_attn_b_kernel, t=t, qb=qb),
        grid=(b, B_HEADS),
        in_specs=[
            pl.BlockSpec(memory_space=pltpu.SMEM),
            blk(12), blk(16), blk(20),
            blk(0), blk(4), blk(8),
            blk(0), blk(4), blk(8),
            _const_spec((1, LANES)), _const_spec((1, LANES)),
        ],
        out_specs=pl.BlockSpec((t, LANES), lambda i, h: (i, h)),
        out_shape=jax.ShapeDtypeStruct((n, BRANCH_WIDTH), BF16),
        scratch_shapes=[
            pltpu.VMEM((t, LANES), BF16),
            pltpu.VMEM((t, LANES), BF16),
            pltpu.VMEM((3, t, LANES), F32),
            pltpu.VMEM((3, t, LANES), F32),
        ],
        compiler_params=pltpu.CompilerParams(
            dimension_semantics=("arbitrary", "arbitrary"), vmem_limit_bytes=VMEM_LIMIT_BYTES),
        name="attn_b",
    )(slopes, qkv_nat, qkv_nat, qkv_nat, qkv_g1, qkv_g1, qkv_g1, qkv_g2, qkv_g2, qkv_g2, gq, gk)


def _attn_c_kernel(q_ref, mem_ref, gm_ref, wk_ref, wv_ref, gq_ref, gk_ref, o_ref, *, t, tq):
    scale = C_HEAD_DIM ** -0.5
    mn = _rms(mem_ref[0], gm_ref[...]).astype(BF16)
    ck = _rms(_dot(mn, wk_ref[...]), gk_ref[...]).astype(BF16)
    cv = _dot(mn, wv_ref[...]).astype(BF16)

    def body(j, carry):
        i0 = pl.multiple_of(j * tq, tq)
        q = (_rms(q_ref[pl.ds(i0, tq), :].astype(F32), gq_ref[...]) * scale).astype(BF16)
        s = _dot_nt(q, ck)
        m = jnp.max(s, axis=-1, keepdims=True)
        e = jnp.exp(s - m)
        inv = 1.0 / jnp.sum(e, axis=-1, keepdims=True)
        o_ref[pl.ds(i0, tq), :] = (_dot(e.astype(BF16), cv) * inv).astype(BF16)
        return carry

    lax.fori_loop(0, t // tq, body, 0)


def _attn_c(qkv_nat, mem, gm, w_kv, gq, gk, b, t, tq=256):
    n = b * t
    n_mem, d = mem.shape[1], mem.shape[2]
    return pl.pallas_call(
        functools.partial(_attn_c_kernel, t=t, tq=tq),
        grid=(b, C_HEADS),
        in_specs=[
            pl.BlockSpec((t, LANES), lambda i, h: (i, 24 + h)),
            pl.BlockSpec((1, n_mem, d), lambda i, h: (i, 0, 0)),
            _const_spec((1, d)),
            pl.BlockSpec((d, LANES), lambda i, h: (0, h)),
            pl.BlockSpec((d, LANES), lambda i, h: (0, C_HEADS + h)),
            _const_spec((1, LANES)), _const_spec((1, LANES)),
        ],
        out_specs=pl.BlockSpec((t, LANES), lambda i, h: (i, h)),
        out_shape=jax.ShapeDtypeStruct((n, BRANCH_WIDTH), BF16),
        compiler_params=pltpu.CompilerParams(
            dimension_semantics=("arbitrary", "arbitrary"), vmem_limit_bytes=VMEM_LIMIT_BYTES),
        name="attn_c",
    )(qkv_nat, mem, gm, w_kv, w_kv, gq, gk)


def _merge_kernel(x_ref, a_ref, b_ref, c_ref, g_ref, wg_ref, bg_ref, wb_ref, wo_ref, o_ref):
    x = x_ref[...]
    hb = _rms(x, g_ref[...]).astype(BF16)
    d = x.shape[-1]
    acc = None
    for g, br_ref in enumerate((a_ref, b_ref, c_ref)):
        gate = jax.nn.sigmoid(_dot(hb, wg_ref[:, g * d:(g + 1) * d]) + bg_ref[:, g * d:(g + 1) * d])
        term = gate * _dot(br_ref[...], wb_ref[g])
        acc = term if acc is None else acc + term
    o_ref[...] = x + _dot(acc.astype(BF16), wo_ref[...])


def _merge(x2, out_a, out_b, out_c, norm_g, w_gate, b_gate, w_branch, w_out, tm=256):
    n, d = x2.shape
    row = lambda w: pl.BlockSpec((tm, w), lambda i: (i, 0))
    return pl.pallas_call(
        _merge_kernel,
        grid=(n // tm,),
        in_specs=[
            row(d), row(BRANCH_WIDTH), row(BRANCH_WIDTH), row(BRANCH_WIDTH),
            _const_spec((1, d)),
            _const_spec((d, N_BRANCHES * d)),
            _const_spec((1, N_BRANCHES * d)),
            _const_spec((N_BRANCHES, BRANCH_WIDTH, d)),
            _const_spec((d, d)),
        ],
        out_specs=row(d),
        out_shape=jax.ShapeDtypeStruct((n, d), F32),
        compiler_params=pltpu.CompilerParams(
            dimension_semantics=("arbitrary",), vmem_limit_bytes=VMEM_LIMIT_BYTES),
        name="merge",
    )(x2, out_a, out_b, out_c, norm_g, w_gate, b_gate, w_branch, w_out)


def _ffn_kernel(x_ref, g_ref, wg_ref, wu_ref, wd_ref, o_ref, *, fc):
    x = x_ref[...]
    hb = _rms(x, g_ref[...]).astype(BF16)
    d_ff = wg_ref.shape[1]
    acc = x
    for c in range(0, d_ff, fc):
        gt = _dot(hb, wg_ref[:, c:c + fc])
        up = _dot(hb, wu_ref[:, c:c + fc])
        acc = acc + _dot((jax.nn.silu(gt) * up).astype(BF16), wd_ref[c:c + fc, :])
    o_ref[...] = acc


def _ffn(x2, norm_g, w_gate, w_up, w_down, tm=256):
    n, d = x2.shape
    d_ff = w_gate.shape[1]
    fc = d_ff // 2
    row = pl.BlockSpec((tm, d), lambda i: (i, 0))
    return pl.pallas_call(
        functools.partial(_ffn_kernel, fc=fc),
        grid=(n // tm,),
        in_specs=[row, _const_spec((1, d)), _const_spec((d, d_ff)), _const_spec((d, d_ff)),
                  _const_spec((d_ff, d))],
        out_specs=row,
        out_shape=jax.ShapeDtypeStruct((n, d), F32),
        compiler_params=pltpu.CompilerParams(
            dimension_semantics=("arbitrary",), vmem_limit_bytes=VMEM_LIMIT_BYTES),
        name="ffn",
    )(x2, norm_g, w_gate, w_up, w_down)


def kernel(x, mem, norm_mix, w_in, w_gate, b_gate, a_q_norm, a_k_norm, a_lambda_q1, a_lambda_k1, a_lambda_q2,
           a_lambda_k2, a_subln, b_q_norm, b_k_norm, mem_norm, w_mem_kv, c_q_norm, c_k_norm, w_branch, w_out,
           norm_ffn, w_ffn_gate, w_ffn_up, w_ffn_down):
    b, t, d = x.shape
    n = b * t
    n_groups = len(B_GROUPS)
    slopes_a = jnp.exp2(-ALIBI_MAX_BIAS * jnp.arange(1, A_HEADS + 1, dtype=F32) / A_HEADS)
    nb = n_groups * B_HEADS
    slopes_b = jnp.exp2(-ALIBI_MAX_BIAS * jnp.arange(1, nb + 1, dtype=F32) / nb).reshape(n_groups, B_HEADS)

    l = 0
    bw = BRANCH_WIDTH
    w = w_in[l].astype(BF16)
    bq, bk, bv = 3 * bw, 3 * bw + 3 * bw, 3 * bw + 6 * bw
    cq = 3 * bw + 9 * bw
    grp = lambda g: jnp.concatenate(
        [w[:, bq + g * bw:bq + (g + 1) * bw], w[:, bk + g * bw:bk + (g + 1) * bw],
         w[:, bv + g * bw:bv + (g + 1) * bw]], axis=1)
    w_nat = jnp.concatenate([w[:, :3 * bw], grp(0), w[:, cq:cq + bw]], axis=1)
    row = lambda v: v.reshape(1, -1)
    twice = lambda v: jnp.concatenate([v, v]).reshape(1, -1)

    qkv_nat, qkv_g1, qkv_g2 = _project(x, row(norm_mix[l]), w_nat, grp(1), grp(2))
    qkv_nat = qkv_nat.reshape(n, NAT_COLS)
    qkv_g1 = qkv_g1.reshape(n, GRP_COLS)
    qkv_g2 = qkv_g2.reshape(n, GRP_COLS)

    bound_a = math.sqrt(A_HEAD_DIM) * jnp.max(jnp.abs(a_q_norm[l])) * jnp.max(jnp.abs(a_k_norm[l]))
    ctl_a = jnp.concatenate([slopes_a, jnp.stack([bound_a * LOG2E,
                                                  (bound_a <= MAX_SAFE_SCORE_BOUND).astype(F32)])])
    out_a = _attn_a(qkv_nat, ctl_a, twice(a_q_norm[l]), twice(a_k_norm[l]), row(a_lambda_q1[l]),
                    row(a_lambda_k1[l]), row(a_lambda_q2[l]), row(a_lambda_k2[l]), row(a_subln[l]), b, t)
    out_b = _attn_b(qkv_nat, qkv_g1, qkv_g2, slopes_b, row(b_q_norm[l]), row(b_k_norm[l]), b, t)
    out_c = _attn_c(qkv_nat, mem, row(mem_norm[l]), w_mem_kv[l].astype(BF16), row(c_q_norm[l]),
                    row(c_k_norm[l]), b, t)

    x2 = x.reshape(n, d)
    x2 = _merge(x2, out_a, out_b, out_c, row(norm_mix[l]), w_gate[l].astype(BF16), row(b_gate[l]),
                w_branch[l].astype(BF16), w_out[l].astype(BF16))
    x2 = _ffn(x2, row(norm_ffn[l]), w_ffn_gate[l].astype(BF16), w_ffn_up[l].astype(BF16),
              w_ffn_down[l].astype(BF16))
    return x2.reshape(b, t, d)
```

```python
import functools
import math

import jax
import jax.numpy as jnp
from jax import lax
from jax.experimental import pallas as pl
from jax.experimental.pallas import tpu as pltpu

F32 = jnp.float32
BF16 = jnp.bfloat16

D_MODEL = 1024
A_HEADS = 4
A_HEAD_DIM = 64
A_V_DIM = 2 * A_HEAD_DIM
B_GROUPS = ((128, 1), (512, 4), (2048, 16))
B_HEADS = 4
B_HEAD_DIM = 128
C_HEADS = 4
C_HEAD_DIM = 128
BRANCH_WIDTH = 512
N_BRANCHES = 3
EPS = 1e-6
ALIBI_MAX_BIAS = 8.0
LAMBDA_INIT = 0.8 - 0.6 * math.exp(-0.3 * 0)

LANES = 128
VMEM_LIMIT_BYTES = 56 * 1024 * 1024

NAT_COLS = 7 * BRANCH_WIDTH
GRP_COLS = 3 * BRANCH_WIDTH
NAT_NORMS = ("half", "half", "none", "full", "full", "none", "full")
GRP_NORMS = ("full", "full", "none")
NEG_BIG = -1e30
LOG2E = 1.4426950408889634
MAX_SAFE_SCORE_BOUND = 40.0


def _rms(x, gain):
    return x * lax.rsqrt(jnp.mean(x * x, axis=-1, keepdims=True) + EPS) * gain


def _dot(a, b):
    return jnp.dot(a, b, preferred_element_type=F32)


def _dot_nt(a, b):
    return lax.dot_general(a, b, (((1,), (1,)), ((), ())), preferred_element_type=F32)


def _const_spec(shape):
    nd = len(shape)
    return pl.BlockSpec(shape, lambda *_: (0,) * nd)


def _head_norm(y, gain, kind):
    if kind == "none":
        return y
    lo_mask = lax.broadcasted_iota(jnp.int32, (1, LANES), 1) < A_HEAD_DIM
    cols = []
    for c in range(0, y.shape[1], LANES):
        z = y[:, c:c + LANES]
        sq = z * z
        s_all = jnp.sum(sq, axis=-1, keepdims=True)
        if kind == "full":
            ms = s_all * (1.0 / LANES)
        else:
            s_lo = jnp.sum(jnp.where(lo_mask, sq, 0.0), axis=-1, keepdims=True)
            ms = jnp.where(lo_mask, s_lo, s_all - s_lo) * (1.0 / A_HEAD_DIM)
        cols.append(z * lax.rsqrt(ms + EPS))
    return jnp.concatenate(cols, axis=1) * gain


def _proj_kernel(x_ref, g_ref, wn_ref, w1_ref, w2_ref, gn_ref, gg_ref, on_ref, o1_ref, o2_ref, slab_ref, hp_ref,
                 *, tm):
    h = _rms(x_ref[0], g_ref[...])
    hb = h.astype(BF16)
    bw = BRANCH_WIDTH
    for ci, kind in enumerate(NAT_NORMS):
        res = _dot(hb, wn_ref[:, ci * bw:(ci + 1) * bw])
        on_ref[0, :, ci * bw:(ci + 1) * bw] = _head_norm(res, gn_ref[ci:ci + 1, :], kind).astype(BF16)
    n_slabs = D_MODEL // LANES
    for s in range(n_slabs):
        slab_ref[s] = h[:, s * LANES:(s + 1) * LANES]
    for dil, w_ref, o_ref in ((4, w1_ref, o1_ref), (16, w2_ref, o2_ref)):
        n = tm // dil
        for r in range(dil):
            for s in range(n_slabs):
                hp_ref[r * n:(r + 1) * n, s * LANES:(s + 1) * LANES] = (
                    slab_ref[s, pl.ds(r, n, stride=dil), :].astype(BF16))
        hp = hp_ref[...]
        for ci, kind in enumerate(GRP_NORMS):
            res = _head_norm(_dot(hp, w_ref[:, ci * bw:(ci + 1) * bw]), gg_ref[ci:ci + 1, :], kind).astype(BF16)
            for r in range(dil):
                o_ref[0, r, :, ci * bw:(ci + 1) * bw] = res[r * n:(r + 1) * n]


def _project(x, norm_g, w_nat, w_g1, w_g2, gain_nat, gain_grp, tm=512):
    b, t, d = x.shape
    grid = (b, t // tm)
    return pl.pallas_call(
        functools.partial(_proj_kernel, tm=tm),
        grid=grid,
        in_specs=[
            pl.BlockSpec((1, tm, d), lambda i, j: (i, j, 0)),
            _const_spec((1, d)),
            _const_spec((d, NAT_COLS)),
            _const_spec((d, GRP_COLS)),
            _const_spec((d, GRP_COLS)),
            _const_spec(gain_nat.shape),
            _const_spec(gain_grp.shape),
        ],
        out_specs=[
            pl.BlockSpec((1, tm, NAT_COLS), lambda i, j: (i, j, 0)),
            pl.BlockSpec((1, 4, tm // 4, GRP_COLS), lambda i, j: (i, 0, j, 0)),
            pl.BlockSpec((1, 16, tm // 16, GRP_COLS), lambda i, j: (i, 0, j, 0)),
        ],
        out_shape=[
            jax.ShapeDtypeStruct((b, t, NAT_COLS), BF16),
            jax.ShapeDtypeStruct((b, 4, t // 4, GRP_COLS), BF16),
            jax.ShapeDtypeStruct((b, 16, t // 16, GRP_COLS), BF16),
        ],
        scratch_shapes=[
            pltpu.VMEM((d // LANES, tm, LANES), F32),
            pltpu.VMEM((tm, d), BF16),
        ],
        compiler_params=pltpu.CompilerParams(
            dimension_semantics=("arbitrary", "arbitrary"), vmem_limit_bytes=VMEM_LIMIT_BYTES),
        name="proj",
    )(x, norm_g, w_nat, w_g1, w_g2, gain_nat, gain_grp)


def _attn_a_kernel(ctl_ref, q_ref, k_ref, v_ref, lq1_ref, lk1_ref, lq2_ref, lk2_ref,
                   sub_ref, o_ref, vt_ref, tab_ref, q2_ref, et_ref, ot_ref, l_ref, *, t, qb, kc):
    h = pl.program_id(0)
    nblk = t // qb
    slope2 = ctl_ref[h] * LOG2E
    shift = ctl_ref[A_HEADS]
    fast = ctl_ref[A_HEADS + 1] > 0.5
    lo_mask = lax.broadcasted_iota(jnp.int32, (1, LANES), 1) < A_HEAD_DIM

    lam = (jnp.exp(jnp.sum(lq1_ref[...] * lk1_ref[...], keepdims=True))
           - jnp.exp(jnp.sum(lq2_ref[...] * lk2_ref[...], keepdims=True)) + LAMBDA_INIT)

    eye = (lax.broadcasted_iota(jnp.int32, (LANES, LANES), 0)
           == lax.broadcasted_iota(jnp.int32, (LANES, LANES), 1)).astype(BF16)
    vt_ref[...] = _dot_nt(eye, v_ref[...]).astype(BF16)

    q = q_ref[...]
    zero = jnp.zeros_like(q)
    q_lo = jnp.where(lo_mask, q, zero)
    q_hi = jnp.where(lo_mask, zero, q)
    for j in range(nblk):
        q2_ref[j, :qb, :] = q_lo[j * qb:(j + 1) * qb]
        q2_ref[j, qb:, :] = q_hi[j * qb:(j + 1) * qb]

    @pl.when(pl.program_id(1) == 0)
    def _():
        cc = lax.broadcasted_iota(jnp.int32, (2 * t - qb, qb), 0)
        il = lax.broadcasted_iota(jnp.int32, (2 * t - qb, qb), 1)
        tab_ref[...] = (-slope2 * jnp.abs(cc - (t - qb) - il).astype(F32)
                        - jnp.where(fast, shift, 0.0))

    def fold8(x, op):
        return op(x.reshape(x.shape[0] // 8, 8, x.shape[1]), axis=0)

    def block(j, slot, exact_max):
        q2 = q2_ref[j]
        off = pl.multiple_of(t - qb - j * qb, qb)

        def scores(c):
            s = _dot_nt(k_ref[c * kc:(c + 1) * kc, :], q2)
            bias = tab_ref[pl.ds(off + c * kc, kc), :]
            return s + jnp.concatenate([bias, bias], axis=1)

        m = None
        if exact_max:
            for c in range(t // kc):
                cm = fold8(scores(c), jnp.max)
                m = cm if m is None else jnp.maximum(m, cm)
            m = jnp.max(m, axis=0, keepdims=True)
        acc = None
        for c in range(t // kc):
            s = scores(c)
            e = jnp.exp2(s - m if exact_max else s)
            et_ref[slot, c * kc:(c + 1) * kc, :] = e.astype(BF16)
            part = fold8(e, jnp.sum)
            acc = part if acc is None else acc + part
        l_ref[pl.ds(j, 1), :] = jnp.sum(acc, axis=0, keepdims=True)
        ot_ref[j] = _dot(vt_ref[...], et_ref[slot])

    def run(exact_max):
        def pair(i, carry):
            block(2 * i, 0, exact_max)
            block(2 * i + 1, 1, exact_max)
            return carry
        lax.fori_loop(0, nblk // 2, pair, 0)

    @pl.when(fast)
    def _():
        run(False)

    @pl.when(jnp.logical_not(fast))
    def _():
        run(True)

    for j in range(nblk):
        ot = ot_ref[j]
        inv = 1.0 / l_ref[j:j + 1, :]
        o = ot[:, :qb] * inv[:, :qb] - ot[:, qb:] * (lam * inv[:, qb:])
        o = _rms(o.T, sub_ref[...]) * (1.0 - LAMBDA_INIT)
        o_ref[j * qb:(j + 1) * qb, :] = o.astype(BF16)


def _attn_a(qkv_nat, ctl, lq1, lk1, lq2, lk2, subln, b, t, qb=128, kc=512):
    n = b * t
    nblk = t // qb
    vec = lambda w: _const_spec((1, w))
    return pl.pallas_call(
        functools.partial(_attn_a_kernel, t=t, qb=qb, kc=kc),
        grid=(A_HEADS, b),
        in_specs=[
            pl.BlockSpec(memory_space=pltpu.SMEM),
            pl.BlockSpec((t, LANES), lambda h, i: (i, h)),
            pl.BlockSpec((t, LANES), lambda h, i: (i, A_HEADS + h)),
            pl.BlockSpec((t, LANES), lambda h, i: (i, 2 * A_HEADS + h)),
            vec(A_HEAD_DIM), vec(A_HEAD_DIM), vec(A_HEAD_DIM), vec(A_HEAD_DIM),
            vec(A_V_DIM),
        ],
        out_specs=pl.BlockSpec((t, LANES), lambda h, i: (i, h)),
        out_shape=jax.ShapeDtypeStruct((n, BRANCH_WIDTH), BF16),
        scratch_shapes=[
            pltpu.VMEM((LANES, t), BF16),
            pltpu.VMEM((2 * t - qb, qb), F32),
            pltpu.VMEM((nblk, 2 * qb, LANES), BF16),
            pltpu.VMEM((2, t, 2 * qb), BF16),
            pltpu.VMEM((nblk, LANES, 2 * qb), F32),
            pltpu.VMEM((nblk, 2 * qb), F32),
        ],
        compiler_params=pltpu.CompilerParams(
            dimension_semantics=("arbitrary", "arbitrary"), vmem_limit_bytes=VMEM_LIMIT_BYTES),
        name="attn_a",
    )(ctl, qkv_nat, qkv_nat, qkv_nat, lq1, lk1, lq2, lk2, subln)


def _attn_b_kernel(ctl_ref, q0_ref, k0_ref, v0_ref, q1_ref, k1_ref, v1_ref, q2_ref, k2_ref, v2_ref,
                   o_ref, acc_ref, den_ref, *, t, qb):
    h = pl.program_id(1)
    n_ctl = len(B_GROUPS) * B_HEADS
    shift = ctl_ref[n_ctl]
    fast = ctl_ref[n_ctl + 1] > 0.5
    refs = ((q0_ref, k0_ref, v0_ref), (q1_ref, k1_ref, v1_ref), (q2_ref, k2_ref, v2_ref))

    def blocks(g):
        window, dil = B_GROUPS[g]
        n_side = window // (2 * dil)
        sub = t // dil
        wk = min(2 * qb, sub)
        for r in range(dil):
            for j in range(sub // qb):
                i0 = j * qb
                ws = min(max(i0 - n_side, 0), sub - wk)
                yield r, i0, r * sub + i0, r * sub + ws, wk, i0 - ws, n_side

    def bias_table(g, wk, offset, n_side, sub_shift):
        slope2 = ctl_ref[g * B_HEADS + h] * (float(B_GROUPS[g][1]) * LOG2E)
        ql = lax.broadcasted_iota(jnp.int32, (qb, wk), 0)
        kl = lax.broadcasted_iota(jnp.int32, (qb, wk), 1)
        dist = jnp.abs(kl - ql - offset)
        return jnp.where(dist <= n_side, -slope2 * dist.astype(F32) - sub_shift, NEG_BIG)

    def store_rows(ref, g, r, i0, val):
        dil = B_GROUPS[g][1]
        if dil == 1:
            ref[g, i0:i0 + qb, :] = val
        else:
            ref[g, pl.ds(r + dil * i0, qb, stride=dil), :] = val

    @pl.when(fast)
    def _():
        for g in range(len(B_GROUPS)):
            q_ref, k_ref, v_ref = refs[g]
            tables = {}
            for r, i0, qrow, krow, wk, offset, n_side in blocks(g):
                if offset not in tables:
                    tables[offset] = bias_table(g, wk, offset, n_side, shift)
                s = _dot_nt(q_ref[qrow:qrow + qb, :], k_ref[krow:krow + wk, :])
                e = jnp.exp2(s + tables[offset])
                den = jnp.sum(e, axis=-1, keepdims=True)
                store_rows(acc_ref, g, r, i0, _dot(e.astype(BF16), v_ref[krow:krow + wk, :]))
                store_rows(den_ref, g, r, i0, jnp.broadcast_to(den, (qb, LANES)))
        o_ref[...] = ((acc_ref[0] + acc_ref[1] + acc_ref[2])
                      / (den_ref[0] + den_ref[1] + den_ref[2])).astype(BF16)

    @pl.when(jnp.logical_not(fast))
    def _():
        for g in range(len(B_GROUPS)):
            q_ref, k_ref, v_ref = refs[g]
            tables = {}
            for r, i0, qrow, krow, wk, offset, n_side in blocks(g):
                if offset not in tables:
                    tables[offset] = bias_table(g, wk, offset, n_side, 0.0)
                s = _dot_nt(q_ref[qrow:qrow + qb, :], k_ref[krow:krow + wk, :]) + tables[offset]
                m = jnp.max(s, axis=-1, keepdims=True)
                e = jnp.exp2(s - m)
                l = jnp.sum(e, axis=-1, keepdims=True)
                o = _dot(e.astype(BF16), v_ref[krow:krow + wk, :]) * (1.0 / l)
                store_rows(acc_ref, g, r, i0, o)
                store_rows(den_ref, g, r, i0, jnp.broadcast_to(m + jnp.log2(l), (qb, LANES)))
        l0, l1, l2 = den_ref[0], den_ref[1], den_ref[2]
        m = jnp.maximum(jnp.maximum(l0, l1), l2)
        e0, e1, e2 = jnp.exp2(l0 - m), jnp.exp2(l1 - m), jnp.exp2(l2 - m)
        o_ref[...] = ((e0 * acc_ref[0] + e1 * acc_ref[1] + e2 * acc_ref[2]) / (e0 + e1 + e2)).astype(BF16)


def _attn_b(qkv_nat, qkv_g1, qkv_g2, ctl, b, t, qb=128):
    n = b * t
    blk = lambda off: pl.BlockSpec((t, LANES), lambda i, h: (i, off + h))
    return pl.pallas_call(
        functools.partial(_attn_b_kernel, t=t, qb=qb),
        grid=(b, B_HEADS),
        in_specs=[
            pl.BlockSpec(memory_space=pltpu.SMEM),
            blk(12), blk(16), blk(20),
            blk(0), blk(4), blk(8),
            blk(0), blk(4), blk(8),
        ],
        out_specs=pl.BlockSpec((t, LANES), lambda i, h: (i, h)),
        out_shape=jax.ShapeDtypeStruct((n, BRANCH_WIDTH), BF16),
        scratch_shapes=[
            pltpu.VMEM((len(B_GROUPS), t, LANES), F32),
            pltpu.VMEM((len(B_GROUPS), t, LANES), F32),
        ],
        compiler_params=pltpu.CompilerParams(
            dimension_semantics=("arbitrary", "arbitrary"), vmem_limit_bytes=VMEM_LIMIT_BYTES),
        name="attn_b",
    )(ctl, qkv_nat, qkv_nat, qkv_nat, qkv_g1, qkv_g1, qkv_g1, qkv_g2, qkv_g2, qkv_g2)


def _mem_kv_kernel(mem_ref, gm_ref, w_ref, gk_ref, o_ref):
    mn = _rms(mem_ref[0], gm_ref[...]).astype(BF16)
    kv = _dot(mn, w_ref[...])
    half = C_HEADS * C_HEAD_DIM
    o_ref[0, :, :half] = _head_norm(kv[:, :half], gk_ref[...], "full").astype(BF16)
    o_ref[0, :, half:] = kv[:, half:].astype(BF16)


def _mem_kv(mem, gm, w_kv, gk4):
    b, n_mem, d = mem.shape
    cols = w_kv.shape[1]
    return pl.pallas_call(
        _mem_kv_kernel,
        grid=(b,),
        in_specs=[
            pl.BlockSpec((1, n_mem, d), lambda i: (i, 0, 0)),
            _const_spec((1, d)), _const_spec((d, cols)), _const_spec((1, cols // 2)),
        ],
        out_specs=pl.BlockSpec((1, n_mem, cols), lambda i: (i, 0, 0)),
        out_shape=jax.ShapeDtypeStruct((b, n_mem, cols), BF16),
        compiler_params=pltpu.CompilerParams(
            dimension_semantics=("arbitrary",), vmem_limit_bytes=VMEM_LIMIT_BYTES),
        name="mem_kv",
    )(mem, gm, w_kv, gk4)


def _merge_kernel(x_ref, a_ref, b_ref, cq_ref, ckv_ref, g_ref, wg_ref, bg_ref, wb_ref, wo_ref, o_ref):
    x = x_ref[...]
    hb = _rms(x, g_ref[...]).astype(BF16)
    d = x.shape[-1]
    half = C_HEADS * C_HEAD_DIM
    heads = []
    for hh in range(C_HEADS):
        cols = slice(hh * C_HEAD_DIM, (hh + 1) * C_HEAD_DIM)
        s = _dot_nt(cq_ref[:, cols], ckv_ref[0, :, cols])
        e = jnp.exp2(s - jnp.max(s, axis=-1, keepdims=True))
        inv = 1.0 / jnp.sum(e, axis=-1, keepdims=True)
        heads.append(_dot(e.astype(BF16), ckv_ref[0, :, half + hh * C_HEAD_DIM:half + (hh + 1) * C_HEAD_DIM]) * inv)
    out_c = jnp.concatenate(heads, axis=1).astype(BF16)
    acc = None
    for g, branch in enumerate((a_ref[...], b_ref[...], out_c)):
        gate = jax.nn.sigmoid(_dot(hb, wg_ref[:, g * d:(g + 1) * d]) + bg_ref[:, g * d:(g + 1) * d])
        term = gate * _dot(branch, wb_ref[g])
        acc = term if acc is None else acc + term
    o_ref[...] = x + _dot(acc.astype(BF16), wo_ref[...])


def _merge(x2, out_a, out_b, qkv_nat, ckv, norm_g, w_gate, b_gate, w_branch, w_out, t, tm=256):
    n, d = x2.shape
    n_mem, kv_cols = ckv.shape[1], ckv.shape[2]
    row = lambda w: pl.BlockSpec((tm, w), lambda i: (i, 0))
    return pl.pallas_call(
        _merge_kernel,
        grid=(n // tm,),
        in_specs=[
            row(d), row(BRANCH_WIDTH), row(BRANCH_WIDTH),
            pl.BlockSpec((tm, BRANCH_WIDTH), lambda i: (i, NAT_COLS // BRANCH_WIDTH - 1)),
            pl.BlockSpec((1, n_mem, kv_cols), lambda i: (i // (t // tm), 0, 0)),
            _const_spec((1, d)),
            _const_spec((d, N_BRANCHES * d)),
            _const_spec((1, N_BRANCHES * d)),
            _const_spec((N_BRANCHES, BRANCH_WIDTH, d)),
            _const_spec((d, d)),
        ],
        out_specs=row(d),
        out_shape=jax.ShapeDtypeStruct((n, d), F32),
        compiler_params=pltpu.CompilerParams(
            dimension_semantics=("arbitrary",), vmem_limit_bytes=VMEM_LIMIT_BYTES),
        name="merge",
    )(x2, out_a, out_b, qkv_nat, ckv, norm_g, w_gate, b_gate, w_branch, w_out)


def _ffn_kernel(x_ref, g_ref, wg_ref, wu_ref, wd_ref, o_ref, *, fc):
    x = x_ref[...]
    hb = _rms(x, g_ref[...]).astype(BF16)
    d_ff = wg_ref.shape[1]
    acc = x
    for c in range(0, d_ff, fc):
        gt = _dot(hb, wg_ref[:, c:c + fc])
        up = _dot(hb, wu_ref[:, c:c + fc])
        acc = acc + _dot((jax.nn.silu(gt) * up).astype(BF16), wd_ref[c:c + fc, :])
    o_ref[...] = acc


def _ffn(x2, norm_g, w_gate, w_up, w_down, tm=256):
    n, d = x2.shape
    d_ff = w_gate.shape[1]
    fc = d_ff // 2
    row = pl.BlockSpec((tm, d), lambda i: (i, 0))
    return pl.pallas_call(
        functools.partial(_ffn_kernel, fc=fc),
        grid=(n // tm,),
        in_specs=[row, _const_spec((1, d)), _const_spec((d, d_ff)), _const_spec((d, d_ff)),
                  _const_spec((d_ff, d))],
        out_specs=row,
        out_shape=jax.ShapeDtypeStruct((n, d), F32),
        compiler_params=pltpu.CompilerParams(
            dimension_semantics=("arbitrary",), vmem_limit_bytes=VMEM_LIMIT_BYTES),
        name="ffn",
    )(x2, norm_g, w_gate, w_up, w_down)


def _score_ctl(slopes, head_dim, gq, gk):
    bound = math.sqrt(head_dim) * jnp.max(jnp.abs(gq)) * jnp.max(jnp.abs(gk))
    return jnp.concatenate([slopes.reshape(-1),
                            jnp.stack([bound * LOG2E, (bound <= MAX_SAFE_SCORE_BOUND).astype(F32)])])


def kernel(x, mem, norm_mix, w_in, w_gate, b_gate, a_q_norm, a_k_norm, a_lambda_q1, a_lambda_k1, a_lambda_q2,
           a_lambda_k2, a_subln, b_q_norm, b_k_norm, mem_norm, w_mem_kv, c_q_norm, c_k_norm, w_branch, w_out,
           norm_ffn, w_ffn_gate, w_ffn_up, w_ffn_down):
    b, t, d = x.shape
    n = b * t
    n_groups = len(B_GROUPS)
    slopes_a = jnp.exp2(-ALIBI_MAX_BIAS * jnp.arange(1, A_HEADS + 1, dtype=F32) / A_HEADS)
    nb = n_groups * B_HEADS
    slopes_b = jnp.exp2(-ALIBI_MAX_BIAS * jnp.arange(1, nb + 1, dtype=F32) / nb)

    l = 0
    bw = BRANCH_WIDTH
    w = w_in[l].astype(BF16)
    bq, bk, bv = 3 * bw, 3 * bw + 3 * bw, 3 * bw + 6 * bw
    cq = 3 * bw + 9 * bw
    grp = lambda g: jnp.concatenate(
        [w[:, bq + g * bw:bq + (g + 1) * bw], w[:, bk + g * bw:bk + (g + 1) * bw],
         w[:, bv + g * bw:bv + (g + 1) * bw]], axis=1)
    w_nat = jnp.concatenate([w[:, :3 * bw], grp(0), w[:, cq:cq + bw]], axis=1)
    row = lambda v: v.reshape(1, -1)
    tiled = lambda v: jnp.tile(v, bw // v.shape[0])
    ones = jnp.ones((bw,), F32)
    a_qs = A_HEAD_DIM ** -0.5 * LOG2E
    b_qs = B_HEAD_DIM ** -0.5 * LOG2E
    c_qs = C_HEAD_DIM ** -0.5 * LOG2E
    gain_nat = jnp.stack([tiled(a_q_norm[l]) * a_qs, tiled(a_k_norm[l]), ones, tiled(b_q_norm[l]) * b_qs,
                          tiled(b_k_norm[l]), ones, tiled(c_q_norm[l]) * c_qs])
    gain_grp = jnp.stack([tiled(b_q_norm[l]) * b_qs, tiled(b_k_norm[l]), ones])

    qkv_nat, qkv_g1, qkv_g2 = _project(x, row(norm_mix[l]), w_nat, grp(1), grp(2), gain_nat, gain_grp)
    qkv_nat = qkv_nat.reshape(n, NAT_COLS)
    qkv_g1 = qkv_g1.reshape(n, GRP_COLS)
    qkv_g2 = qkv_g2.reshape(n, GRP_COLS)

    out_a = _attn_a(qkv_nat, _score_ctl(slopes_a, A_HEAD_DIM, a_q_norm[l], a_k_norm[l]),
                    row(a_lambda_q1[l]), row(a_lambda_k1[l]), row(a_lambda_q2[l]), row(a_lambda_k2[l]),
                    row(a_subln[l]), b, t)
    out_b = _attn_b(qkv_nat, qkv_g1, qkv_g2, _score_ctl(slopes_b, B_HEAD_DIM, b_q_norm[l], b_k_norm[l]), b, t)
    ckv = _mem_kv(mem, row(mem_norm[l]), w_mem_kv[l].astype(BF16), row(tiled(c_k_norm[l])))

    x2 = x.reshape(n, d)
    x2 = _merge(x2, out_a, out_b, qkv_nat, ckv, row(norm_mix[l]), w_gate[l].astype(BF16), row(b_gate[l]),
                w_branch[l].astype(BF16), w_out[l].astype(BF16), t)
    x2 = _ffn(x2, row(norm_ffn[l]), w_ffn_gate[l].astype(BF16), w_ffn_up[l].astype(BF16),
              w_ffn_down[l].astype(BF16))
    return x2.reshape(b, t, d)
```

```python
import functools
import math

import jax
import jax.numpy as jnp
from jax import lax
from jax.experimental import pallas as pl
from jax.experimental.pallas import tpu as pltpu

F32 = jnp.float32
BF16 = jnp.bfloat16

D_MODEL = 1024
A_HEADS = 4
A_HEAD_DIM = 64
A_V_DIM = 2 * A_HEAD_DIM
B_GROUPS = ((128, 1), (512, 4), (2048, 16))
B_HEADS = 4
B_HEAD_DIM = 128
C_HEADS = 4
C_HEAD_DIM = 128
BRANCH_WIDTH = 512
N_BRANCHES = 3
EPS = 1e-6
ALIBI_MAX_BIAS = 8.0
LAMBDA_INIT = 0.8 - 0.6 * math.exp(-0.3 * 0)

LANES = 128
VMEM_LIMIT_BYTES = 56 * 1024 * 1024

NAT_COLS = 7 * BRANCH_WIDTH
GRP_COLS = 3 * BRANCH_WIDTH
NAT_NORMS = ("half", "half", "none", "full", "full", "none", "full")
GRP_NORMS = ("full", "full", "none")
NEG_BIG = -1e30
LOG2E = 1.4426950408889634
MAX_SAFE_SCORE_BOUND = 40.0


def _rms(x, gain):
    return x * lax.rsqrt(jnp.mean(x * x, axis=-1, keepdims=True) + EPS) * gain


def _dot(a, b):
    return jnp.dot(a, b, preferred_element_type=F32)


def _dot_nt(a, b):
    return lax.dot_general(a, b, (((1,), (1,)), ((), ())), preferred_element_type=F32)


def _const_spec(shape):
    nd = len(shape)
    return pl.BlockSpec(shape, lambda *_: (0,) * nd, pipeline_mode=pl.Buffered(1))


def _head_norm(y, gain, kind):
    if kind == "none":
        return y
    lo_mask = lax.broadcasted_iota(jnp.int32, (1, LANES), 1) < A_HEAD_DIM
    cols = []
    for c in range(0, y.shape[1], LANES):
        z = y[:, c:c + LANES]
        sq = z * z
        s_all = jnp.sum(sq, axis=-1, keepdims=True)
        if kind == "full":
            ms = s_all * (1.0 / LANES)
        else:
            s_lo = jnp.sum(jnp.where(lo_mask, sq, 0.0), axis=-1, keepdims=True)
            ms = jnp.where(lo_mask, s_lo, s_all - s_lo) * (1.0 / A_HEAD_DIM)
        cols.append(z * lax.rsqrt(ms + EPS))
    return jnp.concatenate(cols, axis=1) * gain


def _proj_kernel(x_ref, g_ref, wn_ref, w1_ref, w2_ref, gn_ref, gg_ref, on_ref, o1_ref, o2_ref, slab_ref, hp_ref,
                 *, tm):
    h = _rms(x_ref[0], g_ref[...])
    hb = h.astype(BF16)
    bw = BRANCH_WIDTH
    for ci, kind in enumerate(NAT_NORMS):
        res = _dot(hb, wn_ref[:, ci * bw:(ci + 1) * bw])
        on_ref[0, :, ci * bw:(ci + 1) * bw] = _head_norm(res, gn_ref[ci:ci + 1, :], kind).astype(BF16)
    n_slabs = D_MODEL // LANES
    for s in range(n_slabs):
        slab_ref[s] = h[:, s * LANES:(s + 1) * LANES]
    for dil, w_ref, o_ref in ((4, w1_ref, o1_ref), (16, w2_ref, o2_ref)):
        n = tm // dil
        for r in range(dil):
            for s in range(n_slabs):
                hp_ref[r * n:(r + 1) * n, s * LANES:(s + 1) * LANES] = (
                    slab_ref[s, pl.ds(r, n, stride=dil), :].astype(BF16))
        hp = hp_ref[...]
        for ci, kind in enumerate(GRP_NORMS):
            res = _head_norm(_dot(hp, w_ref[:, ci * bw:(ci + 1) * bw]), gg_ref[ci:ci + 1, :], kind).astype(BF16)
            for r in range(dil):
                o_ref[0, r, :, ci * bw:(ci + 1) * bw] = res[r * n:(r + 1) * n]


def _project(x, norm_g, w_nat, w_g1, w_g2, gain_nat, gain_grp, tm=512):
    b, t, d = x.shape
    grid = (b, t // tm)
    return pl.pallas_call(
        functools.partial(_proj_kernel, tm=tm),
        grid=grid,
        in_specs=[
            pl.BlockSpec((1, tm, d), lambda i, j: (i, j, 0)),
            _const_spec((1, d)),
            _const_spec((d, NAT_COLS)),
            _const_spec((d, GRP_COLS)),
            _const_spec((d, GRP_COLS)),
            _const_spec(gain_nat.shape),
            _const_spec(gain_grp.shape),
        ],
        out_specs=[
            pl.BlockSpec((1, tm, NAT_COLS), lambda i, j: (i, j, 0)),
            pl.BlockSpec((1, 4, tm // 4, GRP_COLS), lambda i, j: (i, 0, j, 0)),
            pl.BlockSpec((1, 16, tm // 16, GRP_COLS), lambda i, j: (i, 0, j, 0)),
        ],
        out_shape=[
            jax.ShapeDtypeStruct((b, t, NAT_COLS), BF16),
            jax.ShapeDtypeStruct((b, 4, t // 4, GRP_COLS), BF16),
            jax.ShapeDtypeStruct((b, 16, t // 16, GRP_COLS), BF16),
        ],
        scratch_shapes=[
            pltpu.VMEM((d // LANES, tm, LANES), F32),
            pltpu.VMEM((tm, d), BF16),
        ],
        compiler_params=pltpu.CompilerParams(
            dimension_semantics=("arbitrary", "arbitrary"), vmem_limit_bytes=VMEM_LIMIT_BYTES),
        name="proj",
    )(x, norm_g, w_nat, w_g1, w_g2, gain_nat, gain_grp)


def _attn_a_kernel(ctl_ref, q_ref, k_ref, v_ref, lq1_ref, lk1_ref, lq2_ref, lk2_ref,
                   sub_ref, o_ref, vt_ref, tab_ref, q2_ref, et_ref, ot_ref, l_ref, *, t, qb, kc):
    h = pl.program_id(0)
    nblk = t // qb
    slope2 = ctl_ref[h] * LOG2E
    shift = ctl_ref[A_HEADS]
    fast = ctl_ref[A_HEADS + 1] > 0.5
    lo_mask = lax.broadcasted_iota(jnp.int32, (1, LANES), 1) < A_HEAD_DIM

    lam = (jnp.exp(jnp.sum(lq1_ref[...] * lk1_ref[...], keepdims=True))
           - jnp.exp(jnp.sum(lq2_ref[...] * lk2_ref[...], keepdims=True)) + LAMBDA_INIT)

    eye = (lax.broadcasted_iota(jnp.int32, (LANES, LANES), 0)
           == lax.broadcasted_iota(jnp.int32, (LANES, LANES), 1)).astype(BF16)
    vt_ref[...] = _dot_nt(eye, v_ref[...]).astype(BF16)

    q = q_ref[...]
    zero = jnp.zeros_like(q)
    q_lo = jnp.where(lo_mask, q, zero)
    q_hi = jnp.where(lo_mask, zero, q)
    for j in range(nblk):
        q2_ref[j, :qb, :] = q_lo[j * qb:(j + 1) * qb]
        q2_ref[j, qb:, :] = q_hi[j * qb:(j + 1) * qb]

    @pl.when(pl.program_id(1) == 0)
    def _():
        cc = lax.broadcasted_iota(jnp.int32, (2 * t - qb, qb), 0)
        il = lax.broadcasted_iota(jnp.int32, (2 * t - qb, qb), 1)
        tab_ref[...] = (-slope2 * jnp.abs(cc - (t - qb) - il).astype(F32)
                        - jnp.where(fast, shift, 0.0))

    def fold8(x, op):
        return op(x.reshape(x.shape[0] // 8, 8, x.shape[1]), axis=0)

    def block(j, slot, exact_max):
        q2 = q2_ref[j]
        off = pl.multiple_of(t - qb - j * qb, qb)

        def scores(c):
            s = _dot_nt(k_ref[c * kc:(c + 1) * kc, :], q2)
            bias = tab_ref[pl.ds(off + c * kc, kc), :]
            return s + jnp.concatenate([bias, bias], axis=1)

        m = None
        if exact_max:
            for c in range(t // kc):
                cm = fold8(scores(c), jnp.max)
                m = cm if m is None else jnp.maximum(m, cm)
            m = jnp.max(m, axis=0, keepdims=True)
        acc = None
        for c in range(t // kc):
            s = scores(c)
            e = jnp.exp2(s - m if exact_max else s)
            et_ref[slot, c * kc:(c + 1) * kc, :] = e.astype(BF16)
            part = fold8(e, jnp.sum)
            acc = part if acc is None else acc + part
        l_ref[pl.ds(j, 1), :] = jnp.sum(acc, axis=0, keepdims=True)
        ot_ref[j] = _dot(vt_ref[...], et_ref[slot])

    def run(exact_max):
        def pair(i, carry):
            block(2 * i, 0, exact_max)
            block(2 * i + 1, 1, exact_max)
            return carry
        lax.fori_loop(0, nblk // 2, pair, 0)

    @pl.when(fast)
    def _():
        run(False)

    @pl.when(jnp.logical_not(fast))
    def _():
        run(True)

    for j in range(nblk):
        ot = ot_ref[j]
        inv = 1.0 / l_ref[j:j + 1, :]
        o = ot[:, :qb] * inv[:, :qb] - ot[:, qb:] * (lam * inv[:, qb:])
        o = _rms(o.T, sub_ref[...]) * (1.0 - LAMBDA_INIT)
        o_ref[j * qb:(j + 1) * qb, :] = o.astype(BF16)


def _attn_a(qkv_nat, ctl, lq1, lk1, lq2, lk2, subln, b, t, qb=128, kc=512):
    n = b * t
    nblk = t // qb
    vec = lambda w: _const_spec((1, w))
    return pl.pallas_call(
        functools.partial(_attn_a_kernel, t=t, qb=qb, kc=kc),
        grid=(A_HEADS, b),
        in_specs=[
            pl.BlockSpec(memory_space=pltpu.SMEM),
            pl.BlockSpec((t, LANES), lambda h, i: (i, h)),
            pl.BlockSpec((t, LANES), lambda h, i: (i, A_HEADS + h)),
            pl.BlockSpec((t, LANES), lambda h, i: (i, 2 * A_HEADS + h)),
            vec(A_HEAD_DIM), vec(A_HEAD_DIM), vec(A_HEAD_DIM), vec(A_HEAD_DIM),
            vec(A_V_DIM),
        ],
        out_specs=pl.BlockSpec((t, LANES), lambda h, i: (i, h)),
        out_shape=jax.ShapeDtypeStruct((n, BRANCH_WIDTH), BF16),
        scratch_shapes=[
            pltpu.VMEM((LANES, t), BF16),
            pltpu.VMEM((2 * t - qb, qb), F32),
            pltpu.VMEM((nblk, 2 * qb, LANES), BF16),
            pltpu.VMEM((2, t, 2 * qb), BF16),
            pltpu.VMEM((nblk, LANES, 2 * qb), F32),
            pltpu.VMEM((nblk, 2 * qb), F32),
        ],
        compiler_params=pltpu.CompilerParams(
            dimension_semantics=("arbitrary", "arbitrary"), vmem_limit_bytes=VMEM_LIMIT_BYTES),
        name="attn_a",
    )(ctl, qkv_nat, qkv_nat, qkv_nat, lq1, lk1, lq2, lk2, subln)


def _attn_b_kernel(ctl_ref, q0_ref, k0_ref, v0_ref, q1_ref, k1_ref, v1_ref, q2_ref, k2_ref, v2_ref,
                   o_ref, acc_ref, den_ref, *, t, qb):
    h = pl.program_id(1)
    n_ctl = len(B_GROUPS) * B_HEADS
    shift = ctl_ref[n_ctl]
    fast = ctl_ref[n_ctl + 1] > 0.5
    refs = ((q0_ref, k0_ref, v0_ref), (q1_ref, k1_ref, v1_ref), (q2_ref, k2_ref, v2_ref))

    def blocks(g):
        window, dil = B_GROUPS[g]
        n_side = window // (2 * dil)
        sub = t // dil
        wk = min(2 * qb, sub)
        for r in range(dil):
            for j in range(sub // qb):
                i0 = j * qb
                ws = min(max(i0 - n_side, 0), sub - wk)
                yield r, i0, r * sub + i0, r * sub + ws, wk, i0 - ws, n_side

    def bias_table(g, wk, offset, n_side, sub_shift):
        slope2 = ctl_ref[g * B_HEADS + h] * (float(B_GROUPS[g][1]) * LOG2E)
        ql = lax.broadcasted_iota(jnp.int32, (qb, wk), 0)
        kl = lax.broadcasted_iota(jnp.int32, (qb, wk), 1)
        dist = jnp.abs(kl - ql - offset)
        return jnp.where(dist <= n_side, -slope2 * dist.astype(F32) - sub_shift, NEG_BIG)

    def store_rows(ref, g, r, i0, val):
        dil = B_GROUPS[g][1]
        if dil == 1:
            ref[g, i0:i0 + qb, :] = val
        else:
            ref[g, pl.ds(r + dil * i0, qb, stride=dil), :] = val

    @pl.when(fast)
    def _():
        for g in range(len(B_GROUPS)):
            q_ref, k_ref, v_ref = refs[g]
            tables = {}
            for r, i0, qrow, krow, wk, offset, n_side in blocks(g):
                if offset not in tables:
                    tables[offset] = bias_table(g, wk, offset, n_side, shift)
                s = _dot_nt(q_ref[qrow:qrow + qb, :], k_ref[krow:krow + wk, :])
                e = jnp.exp2(s + tables[offset])
                den = jnp.sum(e, axis=-1, keepdims=True)
                store_rows(acc_ref, g, r, i0, _dot(e.astype(BF16), v_ref[krow:krow + wk, :]))
                store_rows(den_ref, g, r, i0, jnp.broadcast_to(den, (qb, LANES)))
        o_ref[...] = ((acc_ref[0] + acc_ref[1] + acc_ref[2])
                      / (den_ref[0] + den_ref[1] + den_ref[2])).astype(BF16)

    @pl.when(jnp.logical_not(fast))
    def _():
        for g in range(len(B_GROUPS)):
            q_ref, k_ref, v_ref = refs[g]
            tables = {}
            for r, i0, qrow, krow, wk, offset, n_side in blocks(g):
                if offset not in tables:
                    tables[offset] = bias_table(g, wk, offset, n_side, 0.0)
                s = _dot_nt(q_ref[qrow:qrow + qb, :], k_ref[krow:krow + wk, :]) + tables[offset]
                m = jnp.max(s, axis=-1, keepdims=True)
                e = jnp.exp2(s - m)
                l = jnp.sum(e, axis=-1, keepdims=True)
                o = _dot(e.astype(BF16), v_ref[krow:krow + wk, :]) * (1.0 / l)
                store_rows(acc_ref, g, r, i0, o)
                store_rows(den_ref, g, r, i0, jnp.broadcast_to(m + jnp.log2(l), (qb, LANES)))
        l0, l1, l2 = den_ref[0], den_ref[1], den_ref[2]
        m = jnp.maximum(jnp.maximum(l0, l1), l2)
        e0, e1, e2 = jnp.exp2(l0 - m), jnp.exp2(l1 - m), jnp.exp2(l2 - m)
        o_ref[...] = ((e0 * acc_ref[0] + e1 * acc_ref[1] + e2 * acc_ref[2]) / (e0 + e1 + e2)).astype(BF16)


def _attn_b(qkv_nat, qkv_g1, qkv_g2, ctl, b, t, qb=128):
    n = b * t
    blk = lambda off: pl.BlockSpec((t, LANES), lambda i, h: (i, off + h))
    return pl.pallas_call(
        functools.partial(_attn_b_kernel, t=t, qb=qb),
        grid=(b, B_HEADS),
        in_specs=[
            pl.BlockSpec(memory_space=pltpu.SMEM),
            blk(12), blk(16), blk(20),
            blk(0), blk(4), blk(8),
            blk(0), blk(4), blk(8),
        ],
        out_specs=pl.BlockSpec((t, LANES), lambda i, h: (i, h)),
        out_shape=jax.ShapeDtypeStruct((n, BRANCH_WIDTH), BF16),
        scratch_shapes=[
            pltpu.VMEM((len(B_GROUPS), t, LANES), F32),
            pltpu.VMEM((len(B_GROUPS), t, LANES), F32),
        ],
        compiler_params=pltpu.CompilerParams(
            dimension_semantics=("arbitrary", "arbitrary"), vmem_limit_bytes=VMEM_LIMIT_BYTES),
        name="attn_b",
    )(ctl, qkv_nat, qkv_nat, qkv_nat, qkv_g1, qkv_g1, qkv_g1, qkv_g2, qkv_g2, qkv_g2)


def _mem_kv_kernel(mem_ref, gm_ref, w_ref, gk_ref, o_ref):
    mn = _rms(mem_ref[0], gm_ref[...]).astype(BF16)
    kv = _dot(mn, w_ref[...])
    half = C_HEADS * C_HEAD_DIM
    o_ref[0, :, :half] = _head_norm(kv[:, :half], gk_ref[...], "full").astype(BF16)
    o_ref[0, :, half:] = kv[:, half:].astype(BF16)


def _mem_kv(mem, gm, w_kv, gk4):
    b, n_mem, d = mem.shape
    cols = w_kv.shape[1]
    return pl.pallas_call(
        _mem_kv_kernel,
        grid=(b,),
        in_specs=[
            pl.BlockSpec((1, n_mem, d), lambda i: (i, 0, 0)),
            _const_spec((1, d)), _const_spec((d, cols)), _const_spec((1, cols // 2)),
        ],
        out_specs=pl.BlockSpec((1, n_mem, cols), lambda i: (i, 0, 0)),
        out_shape=jax.ShapeDtypeStruct((b, n_mem, cols), BF16),
        compiler_params=pltpu.CompilerParams(
            dimension_semantics=("arbitrary",), vmem_limit_bytes=VMEM_LIMIT_BYTES),
        name="mem_kv",
    )(mem, gm, w_kv, gk4)


def _merge_kernel(x_ref, a_ref, b_ref, cq_ref, ckv_ref, g_ref, wg_ref, bg_ref, wb_ref, wo_ref, o_ref):
    x = x_ref[...]
    hb = _rms(x, g_ref[...]).astype(BF16)
    d = x.shape[-1]
    half = C_HEADS * C_HEAD_DIM

    def gated(g, branch):
        gate = jax.nn.sigmoid(_dot(hb, wg_ref[:, g * d:(g + 1) * d]) + bg_ref[:, g * d:(g + 1) * d])
        return gate * _dot(branch, wb_ref[g])

    acc = gated(0, a_ref[...]) + gated(1, b_ref[...])
    heads = []
    for hh in range(C_HEADS):
        cols = slice(hh * C_HEAD_DIM, (hh + 1) * C_HEAD_DIM)
        s = _dot_nt(cq_ref[:, cols], ckv_ref[0, :, cols])
        e = jnp.exp2(s - jnp.max(s, axis=-1, keepdims=True))
        inv = 1.0 / jnp.sum(e, axis=-1, keepdims=True)
        heads.append(_dot(e.astype(BF16), ckv_ref[0, :, half + hh * C_HEAD_DIM:half + (hh + 1) * C_HEAD_DIM]) * inv)
    acc = acc + gated(2, jnp.concatenate(heads, axis=1).astype(BF16))
    o_ref[...] = x + _dot(acc.astype(BF16), wo_ref[...])


def _merge(x2, out_a, out_b, qkv_nat, ckv, norm_g, w_gate, b_gate, w_branch, w_out, t, tm=512):
    n, d = x2.shape
    n_mem, kv_cols = ckv.shape[1], ckv.shape[2]
    row = lambda w: pl.BlockSpec((tm, w), lambda i: (i, 0))
    return pl.pallas_call(
        _merge_kernel,
        grid=(n // tm,),
        in_specs=[
            row(d), row(BRANCH_WIDTH), row(BRANCH_WIDTH),
            pl.BlockSpec((tm, BRANCH_WIDTH), lambda i: (i, NAT_COLS // BRANCH_WIDTH - 1)),
            pl.BlockSpec((1, n_mem, kv_cols), lambda i: (i // (t // tm), 0, 0)),
            _const_spec((1, d)),
            _const_spec((d, N_BRANCHES * d)),
            _const_spec((1, N_BRANCHES * d)),
            _const_spec((N_BRANCHES, BRANCH_WIDTH, d)),
            _const_spec((d, d)),
        ],
        out_specs=row(d),
        out_shape=jax.ShapeDtypeStruct((n, d), F32),
        compiler_params=pltpu.CompilerParams(
            dimension_semantics=("arbitrary",), vmem_limit_bytes=VMEM_LIMIT_BYTES),
        name="merge",
    )(x2, out_a, out_b, qkv_nat, ckv, norm_g, w_gate, b_gate, w_branch, w_out)


def _ffn_kernel(x_ref, g_ref, wg_ref, wu_ref, wd_ref, o_ref, *, chunks):
    x = x_ref[...]
    hb = _rms(x, g_ref[...]).astype(BF16)
    acc = x
    for c0, c1 in chunks:
        gt = _dot(hb, wg_ref[:, c0:c1])
        up = _dot(hb, wu_ref[:, c0:c1])
        acc = acc + _dot((jax.nn.silu(gt) * up).astype(BF16), wd_ref[c0:c1, :])
    o_ref[...] = acc


def _ffn(x2, norm_g, w_gate, w_up, w_down, tm=512, fc=768):
    n, d = x2.shape
    d_ff = w_gate.shape[1]
    chunks = tuple((c, min(c + fc, d_ff)) for c in range(0, d_ff, fc))
    row = pl.BlockSpec((tm, d), lambda i: (i, 0))
    return pl.pallas_call(
        functools.partial(_ffn_kernel, chunks=chunks),
        grid=(n // tm,),
        in_specs=[row, _const_spec((1, d)), _const_spec((d, d_ff)), _const_spec((d, d_ff)),
                  _const_spec((d_ff, d))],
        out_specs=row,
        out_shape=jax.ShapeDtypeStruct((n, d), F32),
        compiler_params=pltpu.CompilerParams(
            dimension_semantics=("arbitrary",), vmem_limit_bytes=VMEM_LIMIT_BYTES),
        name="ffn",
    )(x2, norm_g, w_gate, w_up, w_down)


def _score_ctl(slopes, head_dim, gq, gk):
    bound = math.sqrt(head_dim) * jnp.max(jnp.abs(gq)) * jnp.max(jnp.abs(gk))
    return jnp.concatenate([slopes.reshape(-1),
                            jnp.stack([bound * LOG2E, (bound <= MAX_SAFE_SCORE_BOUND).astype(F32)])])


def kernel(x, mem, norm_mix, w_in, w_gate, b_gate, a_q_norm, a_k_norm, a_lambda_q1, a_lambda_k1, a_lambda_q2,
           a_lambda_k2, a_subln, b_q_norm, b_k_norm, mem_norm, w_mem_kv, c_q_norm, c_k_norm, w_branch, w_out,
           norm_ffn, w_ffn_gate, w_ffn_up, w_ffn_down):
    b, t, d = x.shape
    n = b * t
    n_groups = len(B_GROUPS)
    slopes_a = jnp.exp2(-ALIBI_MAX_BIAS * jnp.arange(1, A_HEADS + 1, dtype=F32) / A_HEADS)
    nb = n_groups * B_HEADS
    slopes_b = jnp.exp2(-ALIBI_MAX_BIAS * jnp.arange(1, nb + 1, dtype=F32) / nb)

    l = 0
    bw = BRANCH_WIDTH
    w = w_in[l].astype(BF16)
    bq, bk, bv = 3 * bw, 3 * bw + 3 * bw, 3 * bw + 6 * bw
    cq = 3 * bw + 9 * bw
    grp = lambda g: jnp.concatenate(
        [w[:, bq + g * bw:bq + (g + 1) * bw], w[:, bk + g * bw:bk + (g + 1) * bw],
         w[:, bv + g * bw:bv + (g + 1) * bw]], axis=1)
    w_nat = jnp.concatenate([w[:, :3 * bw], grp(0), w[:, cq:cq + bw]], axis=1)
    row = lambda v: v.reshape(1, -1)
    tiled = lambda v: jnp.tile(v, bw // v.shape[0])
    ones = jnp.ones((bw,), F32)
    a_qs = A_HEAD_DIM ** -0.5 * LOG2E
    b_qs = B_HEAD_DIM ** -0.5 * LOG2E
    c_qs = C_HEAD_DIM ** -0.5 * LOG2E
    gain_nat = jnp.stack([tiled(a_q_norm[l]) * a_qs, tiled(a_k_norm[l]), ones, tiled(b_q_norm[l]) * b_qs,
                          tiled(b_k_norm[l]), ones, tiled(c_q_norm[l]) * c_qs])
    gain_grp = jnp.stack([tiled(b_q_norm[l]) * b_qs, tiled(b_k_norm[l]), ones])

    qkv_nat, qkv_g1, qkv_g2 = _project(x, row(norm_mix[l]), w_nat, grp(1), grp(2), gain_nat, gain_grp)
    qkv_nat = qkv_nat.reshape(n, NAT_COLS)
    qkv_g1 = qkv_g1.reshape(n, GRP_COLS)
    qkv_g2 = qkv_g2.reshape(n, GRP_COLS)

    out_a = _attn_a(qkv_nat, _score_ctl(slopes_a, A_HEAD_DIM, a_q_norm[l], a_k_norm[l]),
                    row(a_lambda_q1[l]), row(a_lambda_k1[l]), row(a_lambda_q2[l]), row(a_lambda_k2[l]),
                    row(a_subln[l]), b, t)
    out_b = _attn_b(qkv_nat, qkv_g1, qkv_g2, _score_ctl(slopes_b, B_HEAD_DIM, b_q_norm[l], b_k_norm[l]), b, t)
    ckv = _mem_kv(mem, row(mem_norm[l]), w_mem_kv[l].astype(BF16), row(tiled(c_k_norm[l])))

    x2 = x.reshape(n, d)
    x2 = _merge(x2, out_a, out_b, qkv_nat, ckv, row(norm_mix[l]), w_gate[l].astype(BF16), row(b_gate[l]),
                w_branch[l].astype(BF16), w_out[l].astype(BF16), t)
    x2 = _ffn(x2, row(norm_ffn[l]), w_ffn_gate[l].astype(BF16), w_ffn_up[l].astype(BF16),
              w_ffn_down[l].astype(BF16))
    return x2.reshape(b, t, d)
```

```python
import functools
import math

import jax
import jax.numpy as jnp
from jax import lax
from jax.experimental import pallas as pl
from jax.experimental.pallas import tpu as pltpu

F32 = jnp.float32
BF16 = jnp.bfloat16

D_MODEL = 1024
A_HEADS = 4
A_HEAD_DIM = 64
A_V_DIM = 2 * A_HEAD_DIM
B_GROUPS = ((128, 1), (512, 4), (2048, 16))
B_HEADS = 4
B_HEAD_DIM = 128
C_HEADS = 4
C_HEAD_DIM = 128
BRANCH_WIDTH = 512
N_BRANCHES = 3
EPS = 1e-6
ALIBI_MAX_BIAS = 8.0
LAMBDA_INIT = 0.8 - 0.6 * math.exp(-0.3 * 0)

LANES = 128
VMEM_LIMIT_BYTES = 56 * 1024 * 1024

NAT_COLS = 7 * BRANCH_WIDTH
GRP_COLS = 3 * BRANCH_WIDTH
NAT_NORMS = ("half", "half", "none", "full", "full", "none", "full")
GRP_NORMS = ("full", "full", "none")
NEG_BIG = -1e30
LOG2E = 1.4426950408889634
MAX_SAFE_SCORE_BOUND = 40.0


def _rms(x, gain):
    return x * lax.rsqrt(jnp.mean(x * x, axis=-1, keepdims=True) + EPS) * gain


def _dot(a, b):
    return jnp.dot(a, b, preferred_element_type=F32)


def _dot_nt(a, b):
    return lax.dot_general(a, b, (((1,), (1,)), ((), ())), preferred_element_type=F32)


def _const_spec(shape):
    nd = len(shape)
    return pl.BlockSpec(shape, lambda *_: (0,) * nd, pipeline_mode=pl.Buffered(1))


def _head_norm(y, gain, kind):
    if kind == "none":
        return y
    lo_mask = lax.broadcasted_iota(jnp.int32, (1, LANES), 1) < A_HEAD_DIM
    cols = []
    for c in range(0, y.shape[1], LANES):
        z = y[:, c:c + LANES]
        sq = z * z
        s_all = jnp.sum(sq, axis=-1, keepdims=True)
        if kind == "full":
            ms = s_all * (1.0 / LANES)
        else:
            s_lo = jnp.sum(jnp.where(lo_mask, sq, 0.0), axis=-1, keepdims=True)
            ms = jnp.where(lo_mask, s_lo, s_all - s_lo) * (1.0 / A_HEAD_DIM)
        cols.append(z * lax.rsqrt(ms + EPS))
    return jnp.concatenate(cols, axis=1) * gain


def _proj_kernel(x_ref, g_ref, wn_ref, w1_ref, w2_ref, gn_ref, gg_ref, on_ref, o1_ref, o2_ref, slab_ref, hp_ref,
                 *, tm):
    h = _rms(x_ref[0], g_ref[...])
    hb = h.astype(BF16)
    bw = BRANCH_WIDTH
    for ci, kind in enumerate(NAT_NORMS):
        res = _dot(hb, wn_ref[:, ci * bw:(ci + 1) * bw])
        on_ref[0, :, ci * bw:(ci + 1) * bw] = _head_norm(res, gn_ref[ci:ci + 1, :], kind).astype(BF16)
    n_slabs = D_MODEL // LANES
    for s in range(n_slabs):
        slab_ref[s] = h[:, s * LANES:(s + 1) * LANES]
    for dil, w_ref, o_ref in ((4, w1_ref, o1_ref), (16, w2_ref, o2_ref)):
        n = tm // dil
        for r in range(dil):
            for s in range(n_slabs):
                hp_ref[r * n:(r + 1) * n, s * LANES:(s + 1) * LANES] = (
                    slab_ref[s, pl.ds(r, n, stride=dil), :].astype(BF16))
        hp = hp_ref[...]
        for ci, kind in enumerate(GRP_NORMS):
            res = _head_norm(_dot(hp, w_ref[:, ci * bw:(ci + 1) * bw]), gg_ref[ci:ci + 1, :], kind).astype(BF16)
            for r in range(dil):
                o_ref[0, r, :, ci * bw:(ci + 1) * bw] = res[r * n:(r + 1) * n]


def _project(x, norm_g, w_nat, w_g1, w_g2, gain_nat, gain_grp, tm=512):
    b, t, d = x.shape
    grid = (b, t // tm)
    return pl.pallas_call(
        functools.partial(_proj_kernel, tm=tm),
        grid=grid,
        in_specs=[
            pl.BlockSpec((1, tm, d), lambda i, j: (i, j, 0)),
            _const_spec((1, d)),
            _const_spec((d, NAT_COLS)),
            _const_spec((d, GRP_COLS)),
            _const_spec((d, GRP_COLS)),
            _const_spec(gain_nat.shape),
            _const_spec(gain_grp.shape),
        ],
        out_specs=[
            pl.BlockSpec((1, tm, NAT_COLS), lambda i, j: (i, j, 0)),
            pl.BlockSpec((1, 4, tm // 4, GRP_COLS), lambda i, j: (i, 0, j, 0)),
            pl.BlockSpec((1, 16, tm // 16, GRP_COLS), lambda i, j: (i, 0, j, 0)),
        ],
        out_shape=[
            jax.ShapeDtypeStruct((b, t, NAT_COLS), BF16),
            jax.ShapeDtypeStruct((b, 4, t // 4, GRP_COLS), BF16),
            jax.ShapeDtypeStruct((b, 16, t // 16, GRP_COLS), BF16),
        ],
        scratch_shapes=[
            pltpu.VMEM((d // LANES, tm, LANES), F32),
            pltpu.VMEM((tm, d), BF16),
        ],
        compiler_params=pltpu.CompilerParams(
            dimension_semantics=("arbitrary", "arbitrary"), vmem_limit_bytes=VMEM_LIMIT_BYTES),
        name="proj",
    )(x, norm_g, w_nat, w_g1, w_g2, gain_nat, gain_grp)


def _attn_a_kernel(ctl_ref, q_ref, k_ref, v_ref, lq1_ref, lk1_ref, lq2_ref, lk2_ref,
                   sub_ref, o_ref, vt_ref, tab_ref, q2_ref, et_ref, ot_ref, l_ref, *, t, qb, kc, inflight):
    h = pl.program_id(0)
    nblk = t // qb
    slope2 = ctl_ref[h] * LOG2E
    shift = ctl_ref[A_HEADS]
    fast = ctl_ref[A_HEADS + 1] > 0.5
    lo_mask = lax.broadcasted_iota(jnp.int32, (1, LANES), 1) < A_HEAD_DIM

    lam = (jnp.exp(jnp.sum(lq1_ref[...] * lk1_ref[...], keepdims=True))
           - jnp.exp(jnp.sum(lq2_ref[...] * lk2_ref[...], keepdims=True)) + LAMBDA_INIT)

    eye = (lax.broadcasted_iota(jnp.int32, (LANES, LANES), 0)
           == lax.broadcasted_iota(jnp.int32, (LANES, LANES), 1)).astype(BF16)
    vt_ref[...] = _dot_nt(eye, v_ref[...]).astype(BF16)

    q = q_ref[...]
    zero = jnp.zeros_like(q)
    q_lo = jnp.where(lo_mask, q, zero)
    q_hi = jnp.where(lo_mask, zero, q)
    for j in range(nblk):
        q2_ref[j, :qb, :] = q_lo[j * qb:(j + 1) * qb]
        q2_ref[j, qb:, :] = q_hi[j * qb:(j + 1) * qb]

    @pl.when(pl.program_id(1) == 0)
    def _():
        cc = lax.broadcasted_iota(jnp.int32, (2 * t - qb, qb), 0)
        il = lax.broadcasted_iota(jnp.int32, (2 * t - qb, qb), 1)
        tab_ref[...] = (-slope2 * jnp.abs(cc - (t - qb) - il).astype(F32)
                        - jnp.where(fast, shift, 0.0))

    def fold8(x, op):
        return op(x.reshape(x.shape[0] // 8, 8, x.shape[1]), axis=0)

    def block(j, slot, exact_max):
        q2 = q2_ref[j]
        off = pl.multiple_of(t - qb - j * qb, qb)

        def scores(c):
            s = _dot_nt(k_ref[c * kc:(c + 1) * kc, :], q2)
            bias = tab_ref[pl.ds(off + c * kc, kc), :]
            return s + jnp.concatenate([bias, bias], axis=1)

        m = None
        if exact_max:
            for c in range(t // kc):
                cm = fold8(scores(c), jnp.max)
                m = cm if m is None else jnp.maximum(m, cm)
            m = jnp.max(m, axis=0, keepdims=True)
        acc = None
        for c in range(t // kc):
            s = scores(c)
            e = jnp.exp2(s - m if exact_max else s)
            et_ref[slot, c * kc:(c + 1) * kc, :] = e.astype(BF16)
            part = fold8(e, jnp.sum)
            acc = part if acc is None else acc + part
        l_ref[pl.ds(j, 1), :] = jnp.sum(acc, axis=0, keepdims=True)
        ot_ref[j] = _dot(vt_ref[...], et_ref[slot])

    def finish(j):
        ot = ot_ref[j]
        inv = 1.0 / l_ref[pl.ds(j, 1), :]
        o = ot[:, :qb] * inv[:, :qb] - ot[:, qb:] * (lam * inv[:, qb:])
        o = _rms(o.T, sub_ref[...]) * (1.0 - LAMBDA_INIT)
        o_ref[pl.ds(pl.multiple_of(j * qb, qb), qb), :] = o.astype(BF16)

    def run(exact_max):
        for s in range(inflight):
            block(s, s, exact_max)

        def step(i, carry):
            for s in range(inflight):
                finish(inflight * (i - 1) + s)
            for s in range(inflight):
                block(inflight * i + s, s, exact_max)
            return carry
        lax.fori_loop(1, nblk // inflight, step, 0)
        for s in range(inflight):
            finish(nblk - inflight + s)

    @pl.when(fast)
    def _():
        run(False)

    @pl.when(jnp.logical_not(fast))
    def _():
        run(True)


def _attn_a(qkv_nat, ctl, lq1, lk1, lq2, lk2, subln, b, t, qb=128, kc=512, inflight=4):
    n = b * t
    nblk = t // qb
    vec = lambda w: _const_spec((1, w))
    return pl.pallas_call(
        functools.partial(_attn_a_kernel, t=t, qb=qb, kc=kc, inflight=inflight),
        grid=(A_HEADS, b),
        in_specs=[
            pl.BlockSpec(memory_space=pltpu.SMEM),
            pl.BlockSpec((t, LANES), lambda h, i: (i, h)),
            pl.BlockSpec((t, LANES), lambda h, i: (i, A_HEADS + h)),
            pl.BlockSpec((t, LANES), lambda h, i: (i, 2 * A_HEADS + h)),
            vec(A_HEAD_DIM), vec(A_HEAD_DIM), vec(A_HEAD_DIM), vec(A_HEAD_DIM),
            vec(A_V_DIM),
        ],
        out_specs=pl.BlockSpec((t, LANES), lambda h, i: (i, h)),
        out_shape=jax.ShapeDtypeStruct((n, BRANCH_WIDTH), BF16),
        scratch_shapes=[
            pltpu.VMEM((LANES, t), BF16),
            pltpu.VMEM((2 * t - qb, qb), F32),
            pltpu.VMEM((nblk, 2 * qb, LANES), BF16),
            pltpu.VMEM((inflight, t, 2 * qb), BF16),
            pltpu.VMEM((nblk, LANES, 2 * qb), F32),
            pltpu.VMEM((nblk, 2 * qb), F32),
        ],
        compiler_params=pltpu.CompilerParams(
            dimension_semantics=("arbitrary", "arbitrary"), vmem_limit_bytes=VMEM_LIMIT_BYTES),
        name="attn_a",
    )(ctl, qkv_nat, qkv_nat, qkv_nat, lq1, lk1, lq2, lk2, subln)


def _attn_b_kernel(ctl_ref, q0_ref, k0_ref, v0_ref, q1_ref, k1_ref, v1_ref, q2_ref, k2_ref, v2_ref,
                   o_ref, acc_ref, den_ref, *, t, qb):
    h = pl.program_id(1)
    n_ctl = len(B_GROUPS) * B_HEADS
    shift = ctl_ref[n_ctl]
    fast = ctl_ref[n_ctl + 1] > 0.5
    refs = ((q0_ref, k0_ref, v0_ref), (q1_ref, k1_ref, v1_ref), (q2_ref, k2_ref, v2_ref))

    def blocks(g):
        window, dil = B_GROUPS[g]
        n_side = window // (2 * dil)
        sub = t // dil
        wk = min(2 * qb, sub)
        for r in range(dil):
            for j in range(sub // qb):
                i0 = j * qb
                ws = min(max(i0 - n_side, 0), sub - wk)
                yield r, i0, r * sub + i0, r * sub + ws, wk, i0 - ws, n_side

    def bias_table(g, wk, offset, n_side, sub_shift):
        slope2 = ctl_ref[g * B_HEADS + h] * (float(B_GROUPS[g][1]) * LOG2E)
        ql = lax.broadcasted_iota(jnp.int32, (qb, wk), 0)
        kl = lax.broadcasted_iota(jnp.int32, (qb, wk), 1)
        dist = jnp.abs(kl - ql - offset)
        return jnp.where(dist <= n_side, -slope2 * dist.astype(F32) - sub_shift, NEG_BIG)

    def store_rows(ref, g, r, i0, val):
        dil = B_GROUPS[g][1]
        if dil == 1:
            ref[g, i0:i0 + qb, :] = val
        else:
            ref[g, pl.ds(r + dil * i0, qb, stride=dil), :] = val

    @pl.when(fast)
    def _():
        for g in range(len(B_GROUPS)):
            q_ref, k_ref, v_ref = refs[g]
            tables = {}
            for r, i0, qrow, krow, wk, offset, n_side in blocks(g):
                if offset not in tables:
                    tables[offset] = bias_table(g, wk, offset, n_side, shift)
                s = _dot_nt(q_ref[qrow:qrow + qb, :], k_ref[krow:krow + wk, :])
                e = jnp.exp2(s + tables[offset])
                den = jnp.sum(e, axis=-1, keepdims=True)
                store_rows(acc_ref, g, r, i0, _dot(e.astype(BF16), v_ref[krow:krow + wk, :]))
                store_rows(den_ref, g, r, i0, jnp.broadcast_to(den, (qb, LANES)))
        o_ref[...] = ((acc_ref[0] + acc_ref[1] + acc_ref[2])
                      / (den_ref[0] + den_ref[1] + den_ref[2])).astype(BF16)

    @pl.when(jnp.logical_not(fast))
    def _():
        for g in range(len(B_GROUPS)):
            q_ref, k_ref, v_ref = refs[g]
            tables = {}
            for r, i0, qrow, krow, wk, offset, n_side in blocks(g):
                if offset not in tables:
                    tables[offset] = bias_table(g, wk, offset, n_side, 0.0)
                s = _dot_nt(q_ref[qrow:qrow + qb, :], k_ref[krow:krow + wk, :]) + tables[offset]
                m = jnp.max(s, axis=-1, keepdims=True)
                e = jnp.exp2(s - m)
                l = jnp.sum(e, axis=-1, keepdims=True)
                o = _dot(e.astype(BF16), v_ref[krow:krow + wk, :]) * (1.0 / l)
                store_rows(acc_ref, g, r, i0, o)
                store_rows(den_ref, g, r, i0, jnp.broadcast_to(m + jnp.log2(l), (qb, LANES)))
        l0, l1, l2 = den_ref[0], den_ref[1], den_ref[2]
        m = jnp.maximum(jnp.maximum(l0, l1), l2)
        e0, e1, e2 = jnp.exp2(l0 - m), jnp.exp2(l1 - m), jnp.exp2(l2 - m)
        o_ref[...] = ((e0 * acc_ref[0] + e1 * acc_ref[1] + e2 * acc_ref[2]) / (e0 + e1 + e2)).astype(BF16)


def _attn_b(qkv_nat, qkv_g1, qkv_g2, ctl, b, t, qb=128):
    n = b * t
    blk = lambda off: pl.BlockSpec((t, LANES), lambda i, h: (i, off + h))
    return pl.pallas_call(
        functools.partial(_attn_b_kernel, t=t, qb=qb),
        grid=(b, B_HEADS),
        in_specs=[
            pl.BlockSpec(memory_space=pltpu.SMEM),
            blk(12), blk(16), blk(20),
            blk(0), blk(4), blk(8),
            blk(0), blk(4), blk(8),
        ],
        out_specs=pl.BlockSpec((t, LANES), lambda i, h: (i, h)),
        out_shape=jax.ShapeDtypeStruct((n, BRANCH_WIDTH), BF16),
        scratch_shapes=[
            pltpu.VMEM((len(B_GROUPS), t, LANES), F32),
            pltpu.VMEM((len(B_GROUPS), t, LANES), F32),
        ],
        compiler_params=pltpu.CompilerParams(
            dimension_semantics=("arbitrary", "arbitrary"), vmem_limit_bytes=VMEM_LIMIT_BYTES),
        name="attn_b",
    )(ctl, qkv_nat, qkv_nat, qkv_nat, qkv_g1, qkv_g1, qkv_g1, qkv_g2, qkv_g2, qkv_g2)


def _mem_kv_kernel(mem_ref, gm_ref, w_ref, gk_ref, o_ref):
    mn = _rms(mem_ref[0], gm_ref[...]).astype(BF16)
    kv = _dot(mn, w_ref[...])
    half = C_HEADS * C_HEAD_DIM
    o_ref[0, :, :half] = _head_norm(kv[:, :half], gk_ref[...], "full").astype(BF16)
    o_ref[0, :, half:] = kv[:, half:].astype(BF16)


def _mem_kv(mem, gm, w_kv, gk4):
    b, n_mem, d = mem.shape
    cols = w_kv.shape[1]
    return pl.pallas_call(
        _mem_kv_kernel,
        grid=(b,),
        in_specs=[
            pl.BlockSpec((1, n_mem, d), lambda i: (i, 0, 0)),
            _const_spec((1, d)), _const_spec((d, cols)), _const_spec((1, cols // 2)),
        ],
        out_specs=pl.BlockSpec((1, n_mem, cols), lambda i: (i, 0, 0)),
        out_shape=jax.ShapeDtypeStruct((b, n_mem, cols), BF16),
        compiler_params=pltpu.CompilerParams(
            dimension_semantics=("arbitrary",), vmem_limit_bytes=VMEM_LIMIT_BYTES),
        name="mem_kv",
    )(mem, gm, w_kv, gk4)


def _merge_kernel(x_ref, a_ref, b_ref, cq_ref, ckv_ref, g_ref, wg_ref, bg_ref, wb_ref, wo_ref, o_ref):
    x = x_ref[...]
    hb = _rms(x, g_ref[...]).astype(BF16)
    d = x.shape[-1]
    half = C_HEADS * C_HEAD_DIM

    def gated(g, branch):
        gate = jax.nn.sigmoid(_dot(hb, wg_ref[:, g * d:(g + 1) * d]) + bg_ref[:, g * d:(g + 1) * d])
        return gate * _dot(branch, wb_ref[g])

    acc = gated(0, a_ref[...]) + gated(1, b_ref[...])
    heads = []
    for hh in range(C_HEADS):
        cols = slice(hh * C_HEAD_DIM, (hh + 1) * C_HEAD_DIM)
        s = _dot_nt(cq_ref[:, cols], ckv_ref[0, :, cols])
        e = jnp.exp2(s - jnp.max(s, axis=-1, keepdims=True))
        inv = 1.0 / jnp.sum(e, axis=-1, keepdims=True)
        heads.append(_dot(e.astype(BF16), ckv_ref[0, :, half + hh * C_HEAD_DIM:half + (hh + 1) * C_HEAD_DIM]) * inv)
    acc = acc + gated(2, jnp.concatenate(heads, axis=1).astype(BF16))
    o_ref[...] = x + _dot(acc.astype(BF16), wo_ref[...])


def _merge(x2, out_a, out_b, qkv_nat, ckv, norm_g, w_gate, b_gate, w_branch, w_out, t, tm=512):
    n, d = x2.shape
    n_mem, kv_cols = ckv.shape[1], ckv.shape[2]
    row = lambda w: pl.BlockSpec((tm, w), lambda i: (i, 0))
    return pl.pallas_call(
        _merge_kernel,
        grid=(n // tm,),
        in_specs=[
            row(d), row(BRANCH_WIDTH), row(BRANCH_WIDTH),
            pl.BlockSpec((tm, BRANCH_WIDTH), lambda i: (i, NAT_COLS // BRANCH_WIDTH - 1)),
            pl.BlockSpec((1, n_mem, kv_cols), lambda i: (i // (t // tm), 0, 0)),
            _const_spec((1, d)),
            _const_spec((d, N_BRANCHES * d)),
            _const_spec((1, N_BRANCHES * d)),
            _const_spec((N_BRANCHES, BRANCH_WIDTH, d)),
            _const_spec((d, d)),
        ],
        out_specs=row(d),
        out_shape=jax.ShapeDtypeStruct((n, d), F32),
        compiler_params=pltpu.CompilerParams(
            dimension_semantics=("arbitrary",), vmem_limit_bytes=VMEM_LIMIT_BYTES),
        name="merge",
    )(x2, out_a, out_b, qkv_nat, ckv, norm_g, w_gate, b_gate, w_branch, w_out)


def _ffn_kernel(x_ref, g_ref, wg_ref, wu_ref, wd_ref, o_ref, *, chunks):
    x = x_ref[...]
    hb = _rms(x, g_ref[...]).astype(BF16)
    acc = x
    for c0, c1 in chunks:
        gt = _dot(hb, wg_ref[:, c0:c1])
        up = _dot(hb, wu_ref[:, c0:c1])
        acc = acc + _dot((jax.nn.silu(gt) * up).astype(BF16), wd_ref[c0:c1, :])
    o_ref[...] = acc


def _ffn(x2, norm_g, w_gate, w_up, w_down, tm=512, fc=768):
    n, d = x2.shape
    d_ff = w_gate.shape[1]
    chunks = tuple((c, min(c + fc, d_ff)) for c in range(0, d_ff, fc))
    row = pl.BlockSpec((tm, d), lambda i: (i, 0))
    return pl.pallas_call(
        functools.partial(_ffn_kernel, chunks=chunks),
        grid=(n // tm,),
        in_specs=[row, _const_spec((1, d)), _const_spec((d, d_ff)), _const_spec((d, d_ff)),
                  _const_spec((d_ff, d))],
        out_specs=row,
        out_shape=jax.ShapeDtypeStruct((n, d), F32),
        compiler_params=pltpu.CompilerParams(
            dimension_semantics=("arbitrary",), vmem_limit_bytes=VMEM_LIMIT_BYTES),
        name="ffn",
    )(x2, norm_g, w_gate, w_up, w_down)


def _score_ctl(slopes, head_dim, gq, gk):
    bound = math.sqrt(head_dim) * jnp.max(jnp.abs(gq)) * jnp.max(jnp.abs(gk))
    return jnp.concatenate([slopes.reshape(-1),
                            jnp.stack([bound * LOG2E, (bound <= MAX_SAFE_SCORE_BOUND).astype(F32)])])


def kernel(x, mem, norm_mix, w_in, w_gate, b_gate, a_q_norm, a_k_norm, a_lambda_q1, a_lambda_k1, a_lambda_q2,
           a_lambda_k2, a_subln, b_q_norm, b_k_norm, mem_norm, w_mem_kv, c_q_norm, c_k_norm, w_branch, w_out,
           norm_ffn, w_ffn_gate, w_ffn_up, w_ffn_down):
    b, t, d = x.shape
    n = b * t
    n_groups = len(B_GROUPS)
    slopes_a = jnp.exp2(-ALIBI_MAX_BIAS * jnp.arange(1, A_HEADS + 1, dtype=F32) / A_HEADS)
    nb = n_groups * B_HEADS
    slopes_b = jnp.exp2(-ALIBI_MAX_BIAS * jnp.arange(1, nb + 1, dtype=F32) / nb)

    l = 0
    bw = BRANCH_WIDTH
    w = w_in[l].astype(BF16)
    bq, bk, bv = 3 * bw, 3 * bw + 3 * bw, 3 * bw + 6 * bw
    cq = 3 * bw + 9 * bw
    grp = lambda g: jnp.concatenate(
        [w[:, bq + g * bw:bq + (g + 1) * bw], w[:, bk + g * bw:bk + (g + 1) * bw],
         w[:, bv + g * bw:bv + (g + 1) * bw]], axis=1)
    w_nat = jnp.concatenate([w[:, :3 * bw], grp(0), w[:, cq:cq + bw]], axis=1)
    row = lambda v: v.reshape(1, -1)
    tiled = lambda v: jnp.tile(v, bw // v.shape[0])
    ones = jnp.ones((bw,), F32)
    a_qs = A_HEAD_DIM ** -0.5 * LOG2E
    b_qs = B_HEAD_DIM ** -0.5 * LOG2E
    c_qs = C_HEAD_DIM ** -0.5 * LOG2E
    gain_nat = jnp.stack([tiled(a_q_norm[l]) * a_qs, tiled(a_k_norm[l]), ones, tiled(b_q_norm[l]) * b_qs,
                          tiled(b_k_norm[l]), ones, tiled(c_q_norm[l]) * c_qs])
    gain_grp = jnp.stack([tiled(b_q_norm[l]) * b_qs, tiled(b_k_norm[l]), ones])

    qkv_nat, qkv_g1, qkv_g2 = _project(x, row(norm_mix[l]), w_nat, grp(1), grp(2), gain_nat, gain_grp)
    qkv_nat = qkv_nat.reshape(n, NAT_COLS)
    qkv_g1 = qkv_g1.reshape(n, GRP_COLS)
    qkv_g2 = qkv_g2.reshape(n, GRP_COLS)

    out_a = _attn_a(qkv_nat, _score_ctl(slopes_a, A_HEAD_DIM, a_q_norm[l], a_k_norm[l]),
                    row(a_lambda_q1[l]), row(a_lambda_k1[l]), row(a_lambda_q2[l]), row(a_lambda_k2[l]),
                    row(a_subln[l]), b, t)
    out_b = _attn_b(qkv_nat, qkv_g1, qkv_g2, _score_ctl(slopes_b, B_HEAD_DIM, b_q_norm[l], b_k_norm[l]), b, t)
    ckv = _mem_kv(mem, row(mem_norm[l]), w_mem_kv[l].astype(BF16), row(tiled(c_k_norm[l])))

    x2 = x.reshape(n, d)
    x2 = _merge(x2, out_a, out_b, qkv_nat, ckv, row(norm_mix[l]), w_gate[l].astype(BF16), row(b_gate[l]),
                w_branch[l].astype(BF16), w_out[l].astype(BF16), t)
    x2 = _ffn(x2, row(norm_ffn[l]), w_ffn_gate[l].astype(BF16), w_ffn_up[l].astype(BF16),
              w_ffn_down[l].astype(BF16))
    return x2.reshape(b, t, d)
```

```python
import functools
import math

import jax
import jax.numpy as jnp
from jax import lax
from jax.experimental import pallas as pl
from jax.experimental.pallas import tpu as pltpu

F32 = jnp.float32
BF16 = jnp.bfloat16

D_MODEL = 1024
A_HEADS = 4
A_HEAD_DIM = 64
A_V_DIM = 2 * A_HEAD_DIM
B_GROUPS = ((128, 1), (512, 4), (2048, 16))
B_HEADS = 4
B_HEAD_DIM = 128
C_HEADS = 4
C_HEAD_DIM = 128
BRANCH_WIDTH = 512
N_BRANCHES = 3
EPS = 1e-6
ALIBI_MAX_BIAS = 8.0
LAMBDA_INIT = 0.8 - 0.6 * math.exp(-0.3 * 0)

LANES = 128
VMEM_LIMIT_BYTES = 56 * 1024 * 1024

NAT_COLS = 7 * BRANCH_WIDTH
GRP_COLS = 3 * BRANCH_WIDTH
NAT_NORMS = ("half", "half", "none", "full", "full", "none", "full")
GRP_NORMS = ("full", "full", "none")
NEG_BIG = -1e30
LOG2E = 1.4426950408889634
MAX_SAFE_SCORE_BOUND = 40.0


def _rms(x, gain):
    return x * lax.rsqrt(jnp.mean(x * x, axis=-1, keepdims=True) + EPS) * gain


def _dot(a, b):
    return jnp.dot(a, b, preferred_element_type=F32)


def _dot_nt(a, b):
    return lax.dot_general(a, b, (((1,), (1,)), ((), ())), preferred_element_type=F32)


def _const_spec(shape):
    nd = len(shape)
    return pl.BlockSpec(shape, lambda *_: (0,) * nd, pipeline_mode=pl.Buffered(1))


def _head_norm(y, gain, kind):
    if kind == "none":
        return y
    lo_mask = lax.broadcasted_iota(jnp.int32, (1, LANES), 1) < A_HEAD_DIM
    cols = []
    for c in range(0, y.shape[1], LANES):
        z = y[:, c:c + LANES]
        sq = z * z
        s_all = jnp.sum(sq, axis=-1, keepdims=True)
        if kind == "full":
            ms = s_all * (1.0 / LANES)
        else:
            s_lo = jnp.sum(jnp.where(lo_mask, sq, 0.0), axis=-1, keepdims=True)
            ms = jnp.where(lo_mask, s_lo, s_all - s_lo) * (1.0 / A_HEAD_DIM)
        cols.append(z * lax.rsqrt(ms + EPS))
    return jnp.concatenate(cols, axis=1) * gain


def _proj_kernel(x_ref, g_ref, wn_ref, w1_ref, w2_ref, gn_ref, gg_ref, on_ref, o1_ref, o2_ref, slab_ref, hp_ref,
                 *, tm):
    h = _rms(x_ref[0], g_ref[...])
    hb = h.astype(BF16)
    bw = BRANCH_WIDTH
    for ci, kind in enumerate(NAT_NORMS):
        res = _dot(hb, wn_ref[:, ci * bw:(ci + 1) * bw])
        on_ref[0, :, ci * bw:(ci + 1) * bw] = _head_norm(res, gn_ref[ci:ci + 1, :], kind).astype(BF16)
    n_slabs = D_MODEL // LANES
    for s in range(n_slabs):
        slab_ref[s] = h[:, s * LANES:(s + 1) * LANES]
    for dil, w_ref, o_ref in ((4, w1_ref, o1_ref), (16, w2_ref, o2_ref)):
        n = tm // dil
        for r in range(dil):
            for s in range(n_slabs):
                hp_ref[r * n:(r + 1) * n, s * LANES:(s + 1) * LANES] = (
                    slab_ref[s, pl.ds(r, n, stride=dil), :].astype(BF16))
        hp = hp_ref[...]
        for ci, kind in enumerate(GRP_NORMS):
            res = _head_norm(_dot(hp, w_ref[:, ci * bw:(ci + 1) * bw]), gg_ref[ci:ci + 1, :], kind).astype(BF16)
            for r in range(dil):
                o_ref[0, r, :, ci * bw:(ci + 1) * bw] = res[r * n:(r + 1) * n]


def _project(x, norm_g, w_nat, w_g1, w_g2, gain_nat, gain_grp, tm=512):
    b, t, d = x.shape
    grid = (b, t // tm)
    return pl.pallas_call(
        functools.partial(_proj_kernel, tm=tm),
        grid=grid,
        in_specs=[
            pl.BlockSpec((1, tm, d), lambda i, j: (i, j, 0)),
            _const_spec((1, d)),
            _const_spec((d, NAT_COLS)),
            _const_spec((d, GRP_COLS)),
            _const_spec((d, GRP_COLS)),
            _const_spec(gain_nat.shape),
            _const_spec(gain_grp.shape),
        ],
        out_specs=[
            pl.BlockSpec((1, tm, NAT_COLS), lambda i, j: (i, j, 0)),
            pl.BlockSpec((1, 4, tm // 4, GRP_COLS), lambda i, j: (i, 0, j, 0)),
            pl.BlockSpec((1, 16, tm // 16, GRP_COLS), lambda i, j: (i, 0, j, 0)),
        ],
        out_shape=[
            jax.ShapeDtypeStruct((b, t, NAT_COLS), BF16),
            jax.ShapeDtypeStruct((b, 4, t // 4, GRP_COLS), BF16),
            jax.ShapeDtypeStruct((b, 16, t // 16, GRP_COLS), BF16),
        ],
        scratch_shapes=[
            pltpu.VMEM((d // LANES, tm, LANES), F32),
            pltpu.VMEM((tm, d), BF16),
        ],
        compiler_params=pltpu.CompilerParams(
            dimension_semantics=("arbitrary", "arbitrary"), vmem_limit_bytes=VMEM_LIMIT_BYTES),
        name="proj",
    )(x, norm_g, w_nat, w_g1, w_g2, gain_nat, gain_grp)


def _attn_a_kernel(ctl_ref, q_ref, k_ref, v_ref, lq1_ref, lk1_ref, lq2_ref, lk2_ref,
                   sub_ref, o_ref, vt_ref, tab_ref, q2_ref, et_ref, ot_ref, l_ref, *, t, qb, kc, inflight):
    h = pl.program_id(0)
    nblk = t // qb
    slope2 = ctl_ref[h] * LOG2E
    shift = ctl_ref[A_HEADS]
    fast = ctl_ref[A_HEADS + 1] > 0.5
    lo_mask = lax.broadcasted_iota(jnp.int32, (1, LANES), 1) < A_HEAD_DIM

    lam = (jnp.exp(jnp.sum(lq1_ref[...] * lk1_ref[...], keepdims=True))
           - jnp.exp(jnp.sum(lq2_ref[...] * lk2_ref[...], keepdims=True)) + LAMBDA_INIT)

    eye = (lax.broadcasted_iota(jnp.int32, (LANES, LANES), 0)
           == lax.broadcasted_iota(jnp.int32, (LANES, LANES), 1)).astype(BF16)
    vt_ref[...] = _dot_nt(eye, v_ref[...]).astype(BF16)

    q = q_ref[...]
    zero = jnp.zeros_like(q)
    q_lo = jnp.where(lo_mask, q, zero)
    q_hi = jnp.where(lo_mask, zero, q)
    for j in range(nblk):
        q2_ref[j, :qb, :] = q_lo[j * qb:(j + 1) * qb]
        q2_ref[j, qb:, :] = q_hi[j * qb:(j + 1) * qb]

    @pl.when(pl.program_id(1) == 0)
    def _():
        cc = lax.broadcasted_iota(jnp.int32, (2 * t - qb, qb), 0)
        il = lax.broadcasted_iota(jnp.int32, (2 * t - qb, qb), 1)
        tab_ref[...] = (-slope2 * jnp.abs(cc - (t - qb) - il).astype(F32)
                        - jnp.where(fast, shift, 0.0))

    def fold8(x, op):
        return op(x.reshape(x.shape[0] // 8, 8, x.shape[1]), axis=0)

    def block(j, slot, exact_max):
        q2 = q2_ref[j]
        off = pl.multiple_of(t - qb - j * qb, qb)

        def scores(c):
            s = _dot_nt(k_ref[c * kc:(c + 1) * kc, :], q2)
            bias = tab_ref[pl.ds(off + c * kc, kc), :]
            return s + jnp.concatenate([bias, bias], axis=1)

        m = None
        if exact_max:
            for c in range(t // kc):
                cm = fold8(scores(c), jnp.max)
                m = cm if m is None else jnp.maximum(m, cm)
            m = jnp.max(m, axis=0, keepdims=True)
        acc = None
        for c in range(t // kc):
            s = scores(c)
            e = jnp.exp2(s - m if exact_max else s)
            et_ref[slot, c * kc:(c + 1) * kc, :] = e.astype(BF16)
            part = fold8(e, jnp.sum)
            acc = part if acc is None else acc + part
        l_ref[pl.ds(j, 1), :] = jnp.sum(acc, axis=0, keepdims=True)
        ot_ref[j] = _dot(vt_ref[...], et_ref[slot])

    def finish(j):
        ot = ot_ref[j]
        inv = 1.0 / l_ref[pl.ds(j, 1), :]
        o = ot[:, :qb] * inv[:, :qb] - ot[:, qb:] * (lam * inv[:, qb:])
        o = _rms(o.T, sub_ref[...]) * (1.0 - LAMBDA_INIT)
        o_ref[pl.ds(pl.multiple_of(j * qb, qb), qb), :] = o.astype(BF16)

    def run(exact_max):
        for s in range(inflight):
            block(s, s, exact_max)

        def step(i, carry):
            for s in range(inflight):
                finish(inflight * (i - 1) + s)
            for s in range(inflight):
                block(inflight * i + s, s, exact_max)
            return carry
        lax.fori_loop(1, nblk // inflight, step, 0)
        for s in range(inflight):
            finish(nblk - inflight + s)

    @pl.when(fast)
    def _():
        run(False)

    @pl.when(jnp.logical_not(fast))
    def _():
        run(True)


def _attn_a(qkv_nat, ctl, lq1, lk1, lq2, lk2, subln, b, t, qb=128, kc=512, inflight=4):
    n = b * t
    nblk = t // qb
    vec = lambda w: _const_spec((1, w))
    return pl.pallas_call(
        functools.partial(_attn_a_kernel, t=t, qb=qb, kc=kc, inflight=inflight),
        grid=(A_HEADS, b),
        in_specs=[
            pl.BlockSpec(memory_space=pltpu.SMEM),
            pl.BlockSpec((t, LANES), lambda h, i: (i, h)),
            pl.BlockSpec((t, LANES), lambda h, i: (i, A_HEADS + h)),
            pl.BlockSpec((t, LANES), lambda h, i: (i, 2 * A_HEADS + h)),
            vec(A_HEAD_DIM), vec(A_HEAD_DIM), vec(A_HEAD_DIM), vec(A_HEAD_DIM),
            vec(A_V_DIM),
        ],
        out_specs=pl.BlockSpec((t, LANES), lambda h, i: (i, h)),
        out_shape=jax.ShapeDtypeStruct((n, BRANCH_WIDTH), BF16),
        scratch_shapes=[
            pltpu.VMEM((LANES, t), BF16),
            pltpu.VMEM((2 * t - qb, qb), F32),
            pltpu.VMEM((nblk, 2 * qb, LANES), BF16),
            pltpu.VMEM((inflight, t, 2 * qb), BF16),
            pltpu.VMEM((nblk, LANES, 2 * qb), F32),
            pltpu.VMEM((nblk, 2 * qb), F32),
        ],
        compiler_params=pltpu.CompilerParams(
            dimension_semantics=("arbitrary", "arbitrary"), vmem_limit_bytes=VMEM_LIMIT_BYTES),
        name="attn_a",
    )(ctl, qkv_nat, qkv_nat, qkv_nat, lq1, lk1, lq2, lk2, subln)


def _attn_b_kernel(ctl_ref, q0_ref, k0_ref, v0_ref, q1_ref, k1_ref, v1_ref, q2_ref, k2_ref, v2_ref,
                   o_ref, acc_ref, den_ref, kt_ref, *, t, qb):
    h = pl.program_id(1)
    n_ctl = len(B_GROUPS) * B_HEADS
    shift = ctl_ref[n_ctl]
    fast = ctl_ref[n_ctl + 1] > 0.5
    refs = ((q0_ref, k0_ref, v0_ref), (q1_ref, k1_ref, v1_ref), (q2_ref, k2_ref, v2_ref))

    def blocks(g):
        window, dil = B_GROUPS[g]
        n_side = window // (2 * dil)
        sub = t // dil
        wk = min(2 * qb, sub)
        for r in range(dil):
            for j in range(sub // qb):
                i0 = j * qb
                ws = min(max(i0 - n_side, 0), sub - wk)
                yield r, i0, r * sub + i0, r * sub + ws, wk, i0 - ws, n_side

    def bias_table(g, wk, offset, n_side, sub_shift):
        slope2 = ctl_ref[g * B_HEADS + h] * (float(B_GROUPS[g][1]) * LOG2E)
        ql = lax.broadcasted_iota(jnp.int32, (qb, wk), 0)
        kl = lax.broadcasted_iota(jnp.int32, (qb, wk), 1)
        dist = jnp.abs(kl - ql - offset)
        return jnp.where(dist <= n_side, -slope2 * dist.astype(F32) - sub_shift, NEG_BIG)

    def store_rows(ref, g, r, i0, val):
        dil = B_GROUPS[g][1]
        if dil == 1:
            ref[g, i0:i0 + qb, :] = val
        else:
            ref[g, pl.ds(r + dil * i0, qb, stride=dil), :] = val

    @pl.when(fast)
    def _():
        half = qb // 2
        for g in range(len(B_GROUPS)):
            q_ref, k_ref, v_ref = refs[g]
            kt_ref[0] = k_ref[...].T
            kt_ref[1] = jnp.concatenate([k_ref[half:, :], k_ref[:half, :]], axis=0).T
            tables = {}
            for r, i0, qrow, krow, wk, offset, n_side in blocks(g):
                if offset not in tables:
                    tables[offset] = bias_table(g, wk, offset, n_side, shift)
                if krow % qb == 0:
                    kwt = kt_ref[0, :, krow:krow + wk]
                else:
                    kwt = kt_ref[1, :, krow - half:krow - half + wk]
                s = _dot(q_ref[qrow:qrow + qb, :], kwt)
                e = jnp.exp2(s + tables[offset])
                den = jnp.sum(e, axis=-1, keepdims=True)
                store_rows(acc_ref, g, r, i0, _dot(e.astype(BF16), v_ref[krow:krow + wk, :]))
                store_rows(den_ref, g, r, i0, jnp.broadcast_to(den, (qb, LANES)))
        o_ref[...] = ((acc_ref[0] + acc_ref[1] + acc_ref[2])
                      / (den_ref[0] + den_ref[1] + den_ref[2])).astype(BF16)

    @pl.when(jnp.logical_not(fast))
    def _():
        for g in range(len(B_GROUPS)):
            q_ref, k_ref, v_ref = refs[g]
            tables = {}
            for r, i0, qrow, krow, wk, offset, n_side in blocks(g):
                if offset not in tables:
                    tables[offset] = bias_table(g, wk, offset, n_side, 0.0)
                s = _dot_nt(q_ref[qrow:qrow + qb, :], k_ref[krow:krow + wk, :]) + tables[offset]
                m = jnp.max(s, axis=-1, keepdims=True)
                e = jnp.exp2(s - m)
                l = jnp.sum(e, axis=-1, keepdims=True)
                o = _dot(e.astype(BF16), v_ref[krow:krow + wk, :]) * (1.0 / l)
                store_rows(acc_ref, g, r, i0, o)
                store_rows(den_ref, g, r, i0, jnp.broadcast_to(m + jnp.log2(l), (qb, LANES)))
        l0, l1, l2 = den_ref[0], den_ref[1], den_ref[2]
        m = jnp.maximum(jnp.maximum(l0, l1), l2)
        e0, e1, e2 = jnp.exp2(l0 - m), jnp.exp2(l1 - m), jnp.exp2(l2 - m)
        o_ref[...] = ((e0 * acc_ref[0] + e1 * acc_ref[1] + e2 * acc_ref[2]) / (e0 + e1 + e2)).astype(BF16)


def _attn_b(qkv_nat, qkv_g1, qkv_g2, ctl, b, t, qb=128):
    n = b * t
    blk = lambda off: pl.BlockSpec((t, LANES), lambda i, h: (i, off + h))
    return pl.pallas_call(
        functools.partial(_attn_b_kernel, t=t, qb=qb),
        grid=(b, B_HEADS),
        in_specs=[
            pl.BlockSpec(memory_space=pltpu.SMEM),
            blk(12), blk(16), blk(20),
            blk(0), blk(4), blk(8),
            blk(0), blk(4), blk(8),
        ],
        out_specs=pl.BlockSpec((t, LANES), lambda i, h: (i, h)),
        out_shape=jax.ShapeDtypeStruct((n, BRANCH_WIDTH), BF16),
        scratch_shapes=[
            pltpu.VMEM((len(B_GROUPS), t, LANES), F32),
            pltpu.VMEM((len(B_GROUPS), t, LANES), F32),
            pltpu.VMEM((2, LANES, t), BF16),
        ],
        compiler_params=pltpu.CompilerParams(
            dimension_semantics=("arbitrary", "arbitrary"), vmem_limit_bytes=VMEM_LIMIT_BYTES),
        name="attn_b",
    )(ctl, qkv_nat, qkv_nat, qkv_nat, qkv_g1, qkv_g1, qkv_g1, qkv_g2, qkv_g2, qkv_g2)


def _mem_kv_kernel(mem_ref, gm_ref, w_ref, gk_ref, o_ref):
    mn = _rms(mem_ref[0], gm_ref[...]).astype(BF16)
    kv = _dot(mn, w_ref[...])
    half = C_HEADS * C_HEAD_DIM
    o_ref[0, :, :half] = _head_norm(kv[:, :half], gk_ref[...], "full").astype(BF16)
    o_ref[0, :, half:] = kv[:, half:].astype(BF16)


def _mem_kv(mem, gm, w_kv, gk4):
    b, n_mem, d = mem.shape
    cols = w_kv.shape[1]
    return pl.pallas_call(
        _mem_kv_kernel,
        grid=(b,),
        in_specs=[
            pl.BlockSpec((1, n_mem, d), lambda i: (i, 0, 0)),
            _const_spec((1, d)), _const_spec((d, cols)), _const_spec((1, cols // 2)),
        ],
        out_specs=pl.BlockSpec((1, n_mem, cols), lambda i: (i, 0, 0)),
        out_shape=jax.ShapeDtypeStruct((b, n_mem, cols), BF16),
        compiler_params=pltpu.CompilerParams(
            dimension_semantics=("arbitrary",), vmem_limit_bytes=VMEM_LIMIT_BYTES),
        name="mem_kv",
    )(mem, gm, w_kv, gk4)


def _merge_kernel(x_ref, a_ref, b_ref, cq_ref, ckv_ref, g_ref, wg_ref, bg_ref, wb_ref, wo_ref, o_ref):
    x = x_ref[...]
    hb = _rms(x, g_ref[...]).astype(BF16)
    d = x.shape[-1]
    half = C_HEADS * C_HEAD_DIM

    def gated(g, branch):
        gate = jax.nn.sigmoid(_dot(hb, wg_ref[:, g * d:(g + 1) * d]) + bg_ref[:, g * d:(g + 1) * d])
        return gate * _dot(branch, wb_ref[g])

    acc = gated(0, a_ref[...]) + gated(1, b_ref[...])
    heads = []
    for hh in range(C_HEADS):
        cols = slice(hh * C_HEAD_DIM, (hh + 1) * C_HEAD_DIM)
        s = _dot_nt(cq_ref[:, cols], ckv_ref[0, :, cols])
        e = jnp.exp2(s - jnp.max(s, axis=-1, keepdims=True))
        inv = 1.0 / jnp.sum(e, axis=-1, keepdims=True)
        heads.append(_dot(e.astype(BF16), ckv_ref[0, :, half + hh * C_HEAD_DIM:half + (hh + 1) * C_HEAD_DIM]) * inv)
    acc = acc + gated(2, jnp.concatenate(heads, axis=1).astype(BF16))
    o_ref[...] = x + _dot(acc.astype(BF16), wo_ref[...])


def _merge(x2, out_a, out_b, qkv_nat, ckv, norm_g, w_gate, b_gate, w_branch, w_out, t, tm=512):
    n, d = x2.shape
    n_mem, kv_cols = ckv.shape[1], ckv.shape[2]
    row = lambda w: pl.BlockSpec((tm, w), lambda i: (i, 0))
    return pl.pallas_call(
        _merge_kernel,
        grid=(n // tm,),
        in_specs=[
            row(d), row(BRANCH_WIDTH), row(BRANCH_WIDTH),
            pl.BlockSpec((tm, BRANCH_WIDTH), lambda i: (i, NAT_COLS // BRANCH_WIDTH - 1)),
            pl.BlockSpec((1, n_mem, kv_cols), lambda i: (i // (t // tm), 0, 0)),
            _const_spec((1, d)),
            _const_spec((d, N_BRANCHES * d)),
            _const_spec((1, N_BRANCHES * d)),
            _const_spec((N_BRANCHES, BRANCH_WIDTH, d)),
            _const_spec((d, d)),
        ],
        out_specs=row(d),
        out_shape=jax.ShapeDtypeStruct((n, d), F32),
        compiler_params=pltpu.CompilerParams(
            dimension_semantics=("arbitrary",), vmem_limit_bytes=VMEM_LIMIT_BYTES),
        name="merge",
    )(x2, out_a, out_b, qkv_nat, ckv, norm_g, w_gate, b_gate, w_branch, w_out)


def _ffn_kernel(x_ref, g_ref, wg_ref, wu_ref, wd_ref, o_ref, *, chunks):
    x = x_ref[...]
    hb = _rms(x, g_ref[...]).astype(BF16)
    acc = x
    for c0, c1 in chunks:
        gt = _dot(hb, wg_ref[:, c0:c1])
        up = _dot(hb, wu_ref[:, c0:c1])
        acc = acc + _dot((jax.nn.silu(gt) * up).astype(BF16), wd_ref[c0:c1, :])
    o_ref[...] = acc


def _ffn(x2, norm_g, w_gate, w_up, w_down, tm=512, fc=768):
    n, d = x2.shape
    d_ff = w_gate.shape[1]
    chunks = tuple((c, min(c + fc, d_ff)) for c in range(0, d_ff, fc))
    row = pl.BlockSpec((tm, d), lambda i: (i, 0))
    return pl.pallas_call(
        functools.partial(_ffn_kernel, chunks=chunks),
        grid=(n // tm,),
        in_specs=[row, _const_spec((1, d)), _const_spec((d, d_ff)), _const_spec((d, d_ff)),
                  _const_spec((d_ff, d))],
        out_specs=row,
        out_shape=jax.ShapeDtypeStruct((n, d), F32),
        compiler_params=pltpu.CompilerParams(
            dimension_semantics=("arbitrary",), vmem_limit_bytes=VMEM_LIMIT_BYTES),
        name="ffn",
    )(x2, norm_g, w_gate, w_up, w_down)


def _score_ctl(slopes, head_dim, gq, gk):
    bound = math.sqrt(head_dim) * jnp.max(jnp.abs(gq)) * jnp.max(jnp.abs(gk))
    return jnp.concatenate([slopes.reshape(-1),
                            jnp.stack([bound * LOG2E, (bound <= MAX_SAFE_SCORE_BOUND).astype(F32)])])


def kernel(x, mem, norm_mix, w_in, w_gate, b_gate, a_q_norm, a_k_norm, a_lambda_q1, a_lambda_k1, a_lambda_q2,
           a_lambda_k2, a_subln, b_q_norm, b_k_norm, mem_norm, w_mem_kv, c_q_norm, c_k_norm, w_branch, w_out,
           norm_ffn, w_ffn_gate, w_ffn_up, w_ffn_down):
    b, t, d = x.shape
    n = b * t
    n_groups = len(B_GROUPS)
    slopes_a = jnp.exp2(-ALIBI_MAX_BIAS * jnp.arange(1, A_HEADS + 1, dtype=F32) / A_HEADS)
    nb = n_groups * B_HEADS
    slopes_b = jnp.exp2(-ALIBI_MAX_BIAS * jnp.arange(1, nb + 1, dtype=F32) / nb)

    l = 0
    bw = BRANCH_WIDTH
    w = w_in[l].astype(BF16)
    bq, bk, bv = 3 * bw, 3 * bw + 3 * bw, 3 * bw + 6 * bw
    cq = 3 * bw + 9 * bw
    grp = lambda g: jnp.concatenate(
        [w[:, bq + g * bw:bq + (g + 1) * bw], w[:, bk + g * bw:bk + (g + 1) * bw],
         w[:, bv + g * bw:bv + (g + 1) * bw]], axis=1)
    w_nat = jnp.concatenate([w[:, :3 * bw], grp(0), w[:, cq:cq + bw]], axis=1)
    row = lambda v: v.reshape(1, -1)
    tiled = lambda v: jnp.tile(v, bw // v.shape[0])
    ones = jnp.ones((bw,), F32)
    a_qs = A_HEAD_DIM ** -0.5 * LOG2E
    b_qs = B_HEAD_DIM ** -0.5 * LOG2E
    c_qs = C_HEAD_DIM ** -0.5 * LOG2E
    gain_nat = jnp.stack([tiled(a_q_norm[l]) * a_qs, tiled(a_k_norm[l]), ones, tiled(b_q_norm[l]) * b_qs,
                          tiled(b_k_norm[l]), ones, tiled(c_q_norm[l]) * c_qs])
    gain_grp = jnp.stack([tiled(b_q_norm[l]) * b_qs, tiled(b_k_norm[l]), ones])

    qkv_nat, qkv_g1, qkv_g2 = _project(x, row(norm_mix[l]), w_nat, grp(1), grp(2), gain_nat, gain_grp)
    qkv_nat = qkv_nat.reshape(n, NAT_COLS)
    qkv_g1 = qkv_g1.reshape(n, GRP_COLS)
    qkv_g2 = qkv_g2.reshape(n, GRP_COLS)

    out_a = _attn_a(qkv_nat, _score_ctl(slopes_a, A_HEAD_DIM, a_q_norm[l], a_k_norm[l]),
                    row(a_lambda_q1[l]), row(a_lambda_k1[l]), row(a_lambda_q2[l]), row(a_lambda_k2[l]),
                    row(a_subln[l]), b, t)
    out_b = _attn_b(qkv_nat, qkv_g1, qkv_g2, _score_ctl(slopes_b, B_HEAD_DIM, b_q_norm[l], b_k_norm[l]), b, t)
    ckv = _mem_kv(mem, row(mem_norm[l]), w_mem_kv[l].astype(BF16), row(tiled(c_k_norm[l])))

    x2 = x.reshape(n, d)
    x2 = _merge(x2, out_a, out_b, qkv_nat, ckv, row(norm_mix[l]), w_gate[l].astype(BF16), row(b_gate[l]),
                w_branch[l].astype(BF16), w_out[l].astype(BF16), t)
    x2 = _ffn(x2, row(norm_ffn[l]), w_ffn_gate[l].astype(BF16), w_ffn_up[l].astype(BF16),
              w_ffn_down[l].astype(BF16))
    return x2.reshape(b, t, d)
```

```python
import functools
import math

import jax
import jax.numpy as jnp
from jax import lax
from jax.experimental import pallas as pl
from jax.experimental.pallas import tpu as pltpu

F32 = jnp.float32
BF16 = jnp.bfloat16

D_MODEL = 1024
A_HEADS = 4
A_HEAD_DIM = 64
A_V_DIM = 2 * A_HEAD_DIM
B_GROUPS = ((128, 1), (512, 4), (2048, 16))
B_HEADS = 4
B_HEAD_DIM = 128
C_HEADS = 4
C_HEAD_DIM = 128
BRANCH_WIDTH = 512
N_BRANCHES = 3
EPS = 1e-6
ALIBI_MAX_BIAS = 8.0
LAMBDA_INIT = 0.8 - 0.6 * math.exp(-0.3 * 0)

LANES = 128
VMEM_LIMIT_BYTES = 56 * 1024 * 1024

NAT_COLS = 7 * BRANCH_WIDTH
GRP_COLS = 3 * BRANCH_WIDTH
HEADS_PER_CHUNK = BRANCH_WIDTH // LANES
NAT_HEADS = NAT_COLS // LANES
GRP_HEADS = GRP_COLS // LANES
NAT_NORMS = ("half", "half", "none", "full", "full", "none", "full")
GRP_NORMS = ("full", "full", "none")
NEG_BIG = -1e30
LOG2E = 1.4426950408889634
MAX_SAFE_SCORE_BOUND = 40.0


def _rms(x, gain):
    return x * lax.rsqrt(jnp.mean(x * x, axis=-1, keepdims=True) + EPS) * gain


def _dot(a, b):
    return jnp.dot(a, b, preferred_element_type=F32)


def _dot_nt(a, b):
    return lax.dot_general(a, b, (((1,), (1,)), ((), ())), preferred_element_type=F32)


def _const_spec(shape):
    nd = len(shape)
    return pl.BlockSpec(shape, lambda *_: (0,) * nd, pipeline_mode=pl.Buffered(1))


def _head_norm(y, gain, kind):
    if kind == "none":
        return y
    lo_mask = lax.broadcasted_iota(jnp.int32, (1, LANES), 1) < A_HEAD_DIM
    cols = []
    for c in range(0, y.shape[1], LANES):
        z = y[:, c:c + LANES]
        sq = z * z
        s_all = jnp.sum(sq, axis=-1, keepdims=True)
        if kind == "full":
            ms = s_all * (1.0 / LANES)
        else:
            s_lo = jnp.sum(jnp.where(lo_mask, sq, 0.0), axis=-1, keepdims=True)
            ms = jnp.where(lo_mask, s_lo, s_all - s_lo) * (1.0 / A_HEAD_DIM)
        cols.append(z * lax.rsqrt(ms + EPS))
    return jnp.concatenate(cols, axis=1) * gain


def _proj_kernel(x_ref, g_ref, wn_ref, w1_ref, w2_ref, gn_ref, gg_ref, on_ref, o1_ref, o2_ref, slab_ref, hp_ref,
                 *, tm):
    h = _rms(x_ref[0], g_ref[...])
    hb = h.astype(BF16)
    bw = BRANCH_WIDTH
    for ci, kind in enumerate(NAT_NORMS):
        res = _head_norm(_dot(hb, wn_ref[:, ci * bw:(ci + 1) * bw]), gn_ref[ci:ci + 1, :], kind).astype(BF16)
        for hh in range(HEADS_PER_CHUNK):
            on_ref[ci * HEADS_PER_CHUNK + hh, 0] = res[:, hh * LANES:(hh + 1) * LANES]
    n_slabs = D_MODEL // LANES
    for s in range(n_slabs):
        slab_ref[s] = h[:, s * LANES:(s + 1) * LANES]
    for dil, w_ref, o_ref in ((4, w1_ref, o1_ref), (16, w2_ref, o2_ref)):
        n = tm // dil
        for r in range(dil):
            for s in range(n_slabs):
                hp_ref[r * n:(r + 1) * n, s * LANES:(s + 1) * LANES] = (
                    slab_ref[s, pl.ds(r, n, stride=dil), :].astype(BF16))
        hp = hp_ref[...]
        for ci, kind in enumerate(GRP_NORMS):
            res = _head_norm(_dot(hp, w_ref[:, ci * bw:(ci + 1) * bw]), gg_ref[ci:ci + 1, :], kind).astype(BF16)
            for hh in range(HEADS_PER_CHUNK):
                for r in range(dil):
                    o_ref[ci * HEADS_PER_CHUNK + hh, 0, r] = res[r * n:(r + 1) * n, hh * LANES:(hh + 1) * LANES]


def _project(x, norm_g, w_nat, w_g1, w_g2, gain_nat, gain_grp, tm=512):
    b, t, d = x.shape
    grid = (b, t // tm)
    return pl.pallas_call(
        functools.partial(_proj_kernel, tm=tm),
        grid=grid,
        in_specs=[
            pl.BlockSpec((1, tm, d), lambda i, j: (i, j, 0)),
            _const_spec((1, d)),
            _const_spec((d, NAT_COLS)),
            _const_spec((d, GRP_COLS)),
            _const_spec((d, GRP_COLS)),
            _const_spec(gain_nat.shape),
            _const_spec(gain_grp.shape),
        ],
        out_specs=[
            pl.BlockSpec((NAT_HEADS, 1, tm, LANES), lambda i, j: (0, i, j, 0)),
            pl.BlockSpec((GRP_HEADS, 1, 4, tm // 4, LANES), lambda i, j: (0, i, 0, j, 0)),
            pl.BlockSpec((GRP_HEADS, 1, 16, tm // 16, LANES), lambda i, j: (0, i, 0, j, 0)),
        ],
        out_shape=[
            jax.ShapeDtypeStruct((NAT_HEADS, b, t, LANES), BF16),
            jax.ShapeDtypeStruct((GRP_HEADS, b, 4, t // 4, LANES), BF16),
            jax.ShapeDtypeStruct((GRP_HEADS, b, 16, t // 16, LANES), BF16),
        ],
        scratch_shapes=[
            pltpu.VMEM((d // LANES, tm, LANES), F32),
            pltpu.VMEM((tm, d), BF16),
        ],
        compiler_params=pltpu.CompilerParams(
            dimension_semantics=("arbitrary", "arbitrary"), vmem_limit_bytes=VMEM_LIMIT_BYTES),
        name="proj",
    )(x, norm_g, w_nat, w_g1, w_g2, gain_nat, gain_grp)


def _attn_a_kernel(ctl_ref, q_ref, k_ref, v_ref, lq1_ref, lk1_ref, lq2_ref, lk2_ref,
                   sub_ref, o_ref, vt_ref, tab_ref, q2_ref, et_ref, ot_ref, l_ref, *, t, qb, kc, inflight):
    h = pl.program_id(0)
    nblk = t // qb
    slope2 = ctl_ref[h] * LOG2E
    shift = ctl_ref[A_HEADS]
    fast = ctl_ref[A_HEADS + 1] > 0.5
    lo_mask = lax.broadcasted_iota(jnp.int32, (1, LANES), 1) < A_HEAD_DIM

    lam = (jnp.exp(jnp.sum(lq1_ref[...] * lk1_ref[...], keepdims=True))
           - jnp.exp(jnp.sum(lq2_ref[...] * lk2_ref[...], keepdims=True)) + LAMBDA_INIT)

    eye = (lax.broadcasted_iota(jnp.int32, (LANES, LANES), 0)
           == lax.broadcasted_iota(jnp.int32, (LANES, LANES), 1)).astype(BF16)
    vt_ref[...] = _dot_nt(eye, v_ref[...]).astype(BF16)

    q = q_ref[...]
    zero = jnp.zeros_like(q)
    q_lo = jnp.where(lo_mask, q, zero)
    q_hi = jnp.where(lo_mask, zero, q)
    for j in range(nblk):
        q2_ref[j, :qb, :] = q_lo[j * qb:(j + 1) * qb]
        q2_ref[j, qb:, :] = q_hi[j * qb:(j + 1) * qb]

    @pl.when(pl.program_id(1) == 0)
    def _():
        cc = lax.broadcasted_iota(jnp.int32, (2 * t - qb, qb), 0)
        il = lax.broadcasted_iota(jnp.int32, (2 * t - qb, qb), 1)
        tab_ref[...] = (-slope2 * jnp.abs(cc - (t - qb) - il).astype(F32)
                        - jnp.where(fast, shift, 0.0))

    def fold8(x, op):
        return op(x.reshape(x.shape[0] // 8, 8, x.shape[1]), axis=0)

    def block(j, slot, exact_max):
        q2 = q2_ref[j]
        off = pl.multiple_of(t - qb - j * qb, qb)

        def scores(c):
            s = _dot_nt(k_ref[c * kc:(c + 1) * kc, :], q2)
            bias = tab_ref[pl.ds(off + c * kc, kc), :]
            return s + jnp.concatenate([bias, bias], axis=1)

        m = None
        if exact_max:
            for c in range(t // kc):
                cm = fold8(scores(c), jnp.max)
                m = cm if m is None else jnp.maximum(m, cm)
            m = jnp.max(m, axis=0, keepdims=True)
        acc = None
        for c in range(t // kc):
            s = scores(c)
            e = jnp.exp2(s - m if exact_max else s)
            et_ref[slot, c * kc:(c + 1) * kc, :] = e.astype(BF16)
            part = fold8(e, jnp.sum)
            acc = part if acc is None else acc + part
        l_ref[pl.ds(j, 1), :] = jnp.sum(acc, axis=0, keepdims=True)
        ot_ref[j] = _dot(vt_ref[...], et_ref[slot])

    def finish(j):
        ot = ot_ref[j]
        inv = 1.0 / l_ref[pl.ds(j, 1), :]
        o = ot[:, :qb] * inv[:, :qb] - ot[:, qb:] * (lam * inv[:, qb:])
        o = _rms(o.T, sub_ref[...]) * (1.0 - LAMBDA_INIT)
        o_ref[pl.ds(pl.multiple_of(j * qb, qb), qb), :] = o.astype(BF16)

    def run(exact_max):
        for s in range(inflight):
            block(s, s, exact_max)

        def step(i, carry):
            for s in range(inflight):
                finish(inflight * (i - 1) + s)
            for s in range(inflight):
                block(inflight * i + s, s, exact_max)
            return carry
        lax.fori_loop(1, nblk // inflight, step, 0)
        for s in range(inflight):
            finish(nblk - inflight + s)

    @pl.when(fast)
    def _():
        run(False)

    @pl.when(jnp.logical_not(fast))
    def _():
        run(True)


def _attn_a(qkv_nat, ctl, lq1, lk1, lq2, lk2, subln, b, t, qb=128, kc=512, inflight=4):
    n = b * t
    nblk = t // qb
    vec = lambda w: _const_spec((1, w))
    return pl.pallas_call(
        functools.partial(_attn_a_kernel, t=t, qb=qb, kc=kc, inflight=inflight),
        grid=(A_HEADS, b),
        in_specs=[
            pl.BlockSpec(memory_space=pltpu.SMEM),
            pl.BlockSpec((None, t, LANES), lambda h, i: (h, i, 0)),
            pl.BlockSpec((None, t, LANES), lambda h, i: (A_HEADS + h, i, 0)),
            pl.BlockSpec((None, t, LANES), lambda h, i: (2 * A_HEADS + h, i, 0)),
            vec(A_HEAD_DIM), vec(A_HEAD_DIM), vec(A_HEAD_DIM), vec(A_HEAD_DIM),
            vec(A_V_DIM),
        ],
        out_specs=pl.BlockSpec((None, t, LANES), lambda h, i: (h, i, 0)),
        out_shape=jax.ShapeDtypeStruct((A_HEADS, n, LANES), BF16),
        scratch_shapes=[
            pltpu.VMEM((LANES, t), BF16),
            pltpu.VMEM((2 * t - qb, qb), F32),
            pltpu.VMEM((nblk, 2 * qb, LANES), BF16),
            pltpu.VMEM((inflight, t, 2 * qb), BF16),
            pltpu.VMEM((nblk, LANES, 2 * qb), F32),
            pltpu.VMEM((nblk, 2 * qb), F32),
        ],
        compiler_params=pltpu.CompilerParams(
            dimension_semantics=("arbitrary", "arbitrary"), vmem_limit_bytes=VMEM_LIMIT_BYTES),
        name="attn_a",
    )(ctl, qkv_nat, qkv_nat, qkv_nat, lq1, lk1, lq2, lk2, subln)


def _attn_b_kernel(ctl_ref, q0_ref, k0_ref, v0_ref, q1_ref, k1_ref, v1_ref, q2_ref, k2_ref, v2_ref,
                   o_ref, acc_ref, den_ref, kt_ref, *, t, qb):
    h = pl.program_id(1)
    n_ctl = len(B_GROUPS) * B_HEADS
    shift = ctl_ref[n_ctl]
    fast = ctl_ref[n_ctl + 1] > 0.5
    refs = ((q0_ref, k0_ref, v0_ref), (q1_ref, k1_ref, v1_ref), (q2_ref, k2_ref, v2_ref))

    def blocks(g):
        window, dil = B_GROUPS[g]
        n_side = window // (2 * dil)
        sub = t // dil
        wk = min(2 * qb, sub)
        for r in range(dil):
            for j in range(sub // qb):
                i0 = j * qb
                ws = min(max(i0 - n_side, 0), sub - wk)
                yield r, i0, r * sub + i0, r * sub + ws, wk, i0 - ws, n_side

    def bias_table(g, wk, offset, n_side, sub_shift):
        slope2 = ctl_ref[g * B_HEADS + h] * (float(B_GROUPS[g][1]) * LOG2E)
        ql = lax.broadcasted_iota(jnp.int32, (qb, wk), 0)
        kl = lax.broadcasted_iota(jnp.int32, (qb, wk), 1)
        dist = jnp.abs(kl - ql - offset)
        return jnp.where(dist <= n_side, -slope2 * dist.astype(F32) - sub_shift, NEG_BIG)

    def store_rows(ref, g, r, i0, val):
        dil = B_GROUPS[g][1]
        if dil == 1:
            ref[g, i0:i0 + qb, :] = val
        else:
            ref[g, pl.ds(r + dil * i0, qb, stride=dil), :] = val

    @pl.when(fast)
    def _():
        half = qb // 2
        for g in range(len(B_GROUPS)):
            q_ref, k_ref, v_ref = refs[g]
            kt_ref[0] = k_ref[...].T
            kt_ref[1] = jnp.concatenate([k_ref[half:, :], k_ref[:half, :]], axis=0).T
            tables = {}
            for r, i0, qrow, krow, wk, offset, n_side in blocks(g):
                if offset not in tables:
                    tables[offset] = bias_table(g, wk, offset, n_side, shift)
                if krow % qb == 0:
                    kwt = kt_ref[0, :, krow:krow + wk]
                else:
                    kwt = kt_ref[1, :, krow - half:krow - half + wk]
                s = _dot(q_ref[qrow:qrow + qb, :], kwt)
                e = jnp.exp2(s + tables[offset])
                den = jnp.sum(e, axis=-1, keepdims=True)
                store_rows(acc_ref, g, r, i0, _dot(e.astype(BF16), v_ref[krow:krow + wk, :]))
                store_rows(den_ref, g, r, i0, jnp.broadcast_to(den, (qb, LANES)))
        o_ref[...] = ((acc_ref[0] + acc_ref[1] + acc_ref[2])
                      / (den_ref[0] + den_ref[1] + den_ref[2])).astype(BF16)

    @pl.when(jnp.logical_not(fast))
    def _():
        for g in range(len(B_GROUPS)):
            q_ref, k_ref, v_ref = refs[g]
            tables = {}
            for r, i0, qrow, krow, wk, offset, n_side in blocks(g):
                if offset not in tables:
                    tables[offset] = bias_table(g, wk, offset, n_side, 0.0)
                s = _dot_nt(q_ref[qrow:qrow + qb, :], k_ref[krow:krow + wk, :]) + tables[offset]
                m = jnp.max(s, axis=-1, keepdims=True)
                e = jnp.exp2(s - m)
                l = jnp.sum(e, axis=-1, keepdims=True)
                o = _dot(e.astype(BF16), v_ref[krow:krow + wk, :]) * (1.0 / l)
                store_rows(acc_ref, g, r, i0, o)
                store_rows(den_ref, g, r, i0, jnp.broadcast_to(m + jnp.log2(l), (qb, LANES)))
        l0, l1, l2 = den_ref[0], den_ref[1], den_ref[2]
        m = jnp.maximum(jnp.maximum(l0, l1), l2)
        e0, e1, e2 = jnp.exp2(l0 - m), jnp.exp2(l1 - m), jnp.exp2(l2 - m)
        o_ref[...] = ((e0 * acc_ref[0] + e1 * acc_ref[1] + e2 * acc_ref[2]) / (e0 + e1 + e2)).astype(BF16)


def _attn_b(qkv_nat, qkv_g1, qkv_g2, ctl, b, t, qb=128):
    n = b * t
    blk = lambda off: pl.BlockSpec((None, t, LANES), lambda i, h: (off + h, i, 0))
    return pl.pallas_call(
        functools.partial(_attn_b_kernel, t=t, qb=qb),
        grid=(b, B_HEADS),
        in_specs=[
            pl.BlockSpec(memory_space=pltpu.SMEM),
            blk(12), blk(16), blk(20),
            blk(0), blk(4), blk(8),
            blk(0), blk(4), blk(8),
        ],
        out_specs=pl.BlockSpec((None, t, LANES), lambda i, h: (h, i, 0)),
        out_shape=jax.ShapeDtypeStruct((B_HEADS, n, LANES), BF16),
        scratch_shapes=[
            pltpu.VMEM((len(B_GROUPS), t, LANES), F32),
            pltpu.VMEM((len(B_GROUPS), t, LANES), F32),
            pltpu.VMEM((2, LANES, t), BF16),
        ],
        compiler_params=pltpu.CompilerParams(
            dimension_semantics=("arbitrary", "arbitrary"), vmem_limit_bytes=VMEM_LIMIT_BYTES),
        name="attn_b",
    )(ctl, qkv_nat, qkv_nat, qkv_nat, qkv_g1, qkv_g1, qkv_g1, qkv_g2, qkv_g2, qkv_g2)


def _mem_kv_kernel(mem_ref, gm_ref, w_ref, gk_ref, o_ref):
    mn = _rms(mem_ref[0], gm_ref[...]).astype(BF16)
    kv = _dot(mn, w_ref[...])
    half = C_HEADS * C_HEAD_DIM
    o_ref[0, :, :half] = _head_norm(kv[:, :half], gk_ref[...], "full").astype(BF16)
    o_ref[0, :, half:] = kv[:, half:].astype(BF16)


def _mem_kv(mem, gm, w_kv, gk4):
    b, n_mem, d = mem.shape
    cols = w_kv.shape[1]
    return pl.pallas_call(
        _mem_kv_kernel,
        grid=(b,),
        in_specs=[
            pl.BlockSpec((1, n_mem, d), lambda i: (i, 0, 0)),
            _const_spec((1, d)), _const_spec((d, cols)), _const_spec((1, cols // 2)),
        ],
        out_specs=pl.BlockSpec((1, n_mem, cols), lambda i: (i, 0, 0)),
        out_shape=jax.ShapeDtypeStruct((b, n_mem, cols), BF16),
        compiler_params=pltpu.CompilerParams(
            dimension_semantics=("arbitrary",), vmem_limit_bytes=VMEM_LIMIT_BYTES),
        name="mem_kv",
    )(mem, gm, w_kv, gk4)


def _merge_kernel(x_ref, a_ref, b_ref, cq_ref, ckv_ref, g_ref, wg_ref, bg_ref, wb_ref, wo_ref, o_ref):
    x = x_ref[...]
    hb = _rms(x, g_ref[...]).astype(BF16)
    d = x.shape[-1]
    half = C_HEADS * C_HEAD_DIM

    def gated(g, branch):
        gate = jax.nn.sigmoid(_dot(hb, wg_ref[:, g * d:(g + 1) * d]) + bg_ref[:, g * d:(g + 1) * d])
        return gate * _dot(branch, wb_ref[g])

    wide = lambda ref: jnp.concatenate([ref[hh] for hh in range(HEADS_PER_CHUNK)], axis=1)
    acc = gated(0, wide(a_ref)) + gated(1, wide(b_ref))
    heads = []
    for hh in range(C_HEADS):
        cols = slice(hh * C_HEAD_DIM, (hh + 1) * C_HEAD_DIM)
        s = _dot_nt(cq_ref[hh], ckv_ref[0, :, cols])
        e = jnp.exp2(s - jnp.max(s, axis=-1, keepdims=True))
        inv = 1.0 / jnp.sum(e, axis=-1, keepdims=True)
        heads.append(_dot(e.astype(BF16), ckv_ref[0, :, half + hh * C_HEAD_DIM:half + (hh + 1) * C_HEAD_DIM]) * inv)
    acc = acc + gated(2, jnp.concatenate(heads, axis=1).astype(BF16))
    o_ref[...] = x + _dot(acc.astype(BF16), wo_ref[...])


def _merge(x2, out_a, out_b, qkv_nat, ckv, norm_g, w_gate, b_gate, w_branch, w_out, t, tm=512):
    n, d = x2.shape
    n_mem, kv_cols = ckv.shape[1], ckv.shape[2]
    row = lambda w: pl.BlockSpec((tm, w), lambda i: (i, 0))
    heads = lambda chunk: pl.BlockSpec((HEADS_PER_CHUNK, tm, LANES), lambda i: (chunk, i, 0))
    return pl.pallas_call(
        _merge_kernel,
        grid=(n // tm,),
        in_specs=[
            row(d), heads(0), heads(0), heads(NAT_HEADS // HEADS_PER_CHUNK - 1),
            pl.BlockSpec((1, n_mem, kv_cols), lambda i: (i // (t // tm), 0, 0)),
            _const_spec((1, d)),
            _const_spec((d, N_BRANCHES * d)),
            _const_spec((1, N_BRANCHES * d)),
            _const_spec((N_BRANCHES, BRANCH_WIDTH, d)),
            _const_spec((d, d)),
        ],
        out_specs=row(d),
        out_shape=jax.ShapeDtypeStruct((n, d), F32),
        compiler_params=pltpu.CompilerParams(
            dimension_semantics=("arbitrary",), vmem_limit_bytes=VMEM_LIMIT_BYTES),
        name="merge",
    )(x2, out_a, out_b, qkv_nat, ckv, norm_g, w_gate, b_gate, w_branch, w_out)


def _ffn_kernel(x_ref, g_ref, wg_ref, wu_ref, wd_ref, o_ref, *, chunks):
    x = x_ref[...]
    hb = _rms(x, g_ref[...]).astype(BF16)
    acc = x
    for c0, c1 in chunks:
        gt = _dot(hb, wg_ref[:, c0:c1])
        up = _dot(hb, wu_ref[:, c0:c1])
        acc = acc + _dot((jax.nn.silu(gt) * up).astype(BF16), wd_ref[c0:c1, :])
    o_ref[...] = acc


def _ffn(x2, norm_g, w_gate, w_up, w_down, tm=512, fc=768):
    n, d = x2.shape
    d_ff = w_gate.shape[1]
    chunks = tuple((c, min(c + fc, d_ff)) for c in range(0, d_ff, fc))
    row = pl.BlockSpec((tm, d), lambda i: (i, 0))
    return pl.pallas_call(
        functools.partial(_ffn_kernel, chunks=chunks),
        grid=(n // tm,),
        in_specs=[row, _const_spec((1, d)), _const_spec((d, d_ff)), _const_spec((d, d_ff)),
                  _const_spec((d_ff, d))],
        out_specs=row,
        out_shape=jax.ShapeDtypeStruct((n, d), F32),
        compiler_params=pltpu.CompilerParams(
            dimension_semantics=("arbitrary",), vmem_limit_bytes=VMEM_LIMIT_BYTES),
        name="ffn",
    )(x2, norm_g, w_gate, w_up, w_down)


def _score_ctl(slopes, head_dim, gq, gk):
    bound = math.sqrt(head_dim) * jnp.max(jnp.abs(gq)) * jnp.max(jnp.abs(gk))
    return jnp.concatenate([slopes.reshape(-1),
                            jnp.stack([bound * LOG2E, (bound <= MAX_SAFE_SCORE_BOUND).astype(F32)])])


def kernel(x, mem, norm_mix, w_in, w_gate, b_gate, a_q_norm, a_k_norm, a_lambda_q1, a_lambda_k1, a_lambda_q2,
           a_lambda_k2, a_subln, b_q_norm, b_k_norm, mem_norm, w_mem_kv, c_q_norm, c_k_norm, w_branch, w_out,
           norm_ffn, w_ffn_gate, w_ffn_up, w_ffn_down):
    b, t, d = x.shape
    n = b * t
    n_groups = len(B_GROUPS)
    slopes_a = jnp.exp2(-ALIBI_MAX_BIAS * jnp.arange(1, A_HEADS + 1, dtype=F32) / A_HEADS)
    nb = n_groups * B_HEADS
    slopes_b = jnp.exp2(-ALIBI_MAX_BIAS * jnp.arange(1, nb + 1, dtype=F32) / nb)

    l = 0
    bw = BRANCH_WIDTH
    w = w_in[l].astype(BF16)
    bq, bk, bv = 3 * bw, 3 * bw + 3 * bw, 3 * bw + 6 * bw
    cq = 3 * bw + 9 * bw
    grp = lambda g: jnp.concatenate(
        [w[:, bq + g * bw:bq + (g + 1) * bw], w[:, bk + g * bw:bk + (g + 1) * bw],
         w[:, bv + g * bw:bv + (g + 1) * bw]], axis=1)
    w_nat = jnp.concatenate([w[:, :3 * bw], grp(0), w[:, cq:cq + bw]], axis=1)
    row = lambda v: v.reshape(1, -1)
    tiled = lambda v: jnp.tile(v, bw // v.shape[0])
    ones = jnp.ones((bw,), F32)
    a_qs = A_HEAD_DIM ** -0.5 * LOG2E
    b_qs = B_HEAD_DIM ** -0.5 * LOG2E
    c_qs = C_HEAD_DIM ** -0.5 * LOG2E
    gain_nat = jnp.stack([tiled(a_q_norm[l]) * a_qs, tiled(a_k_norm[l]), ones, tiled(b_q_norm[l]) * b_qs,
                          tiled(b_k_norm[l]), ones, tiled(c_q_norm[l]) * c_qs])
    gain_grp = jnp.stack([tiled(b_q_norm[l]) * b_qs, tiled(b_k_norm[l]), ones])

    qkv_nat, qkv_g1, qkv_g2 = _project(x, row(norm_mix[l]), w_nat, grp(1), grp(2), gain_nat, gain_grp)
    qkv_nat = qkv_nat.reshape(NAT_HEADS, n, LANES)
    qkv_g1 = qkv_g1.reshape(GRP_HEADS, n, LANES)
    qkv_g2 = qkv_g2.reshape(GRP_HEADS, n, LANES)

    out_a = _attn_a(qkv_nat, _score_ctl(slopes_a, A_HEAD_DIM, a_q_norm[l], a_k_norm[l]),
                    row(a_lambda_q1[l]), row(a_lambda_k1[l]), row(a_lambda_q2[l]), row(a_lambda_k2[l]),
                    row(a_subln[l]), b, t)
    out_b = _attn_b(qkv_nat, qkv_g1, qkv_g2, _score_ctl(slopes_b, B_HEAD_DIM, b_q_norm[l], b_k_norm[l]), b, t)
    ckv = _mem_kv(mem, row(mem_norm[l]), w_mem_kv[l].astype(BF16), row(tiled(c_k_norm[l])))

    x2 = x.reshape(n, d)
    x2 = _merge(x2, out_a, out_b, qkv_nat, ckv, row(norm_mix[l]), w_gate[l].astype(BF16), row(b_gate[l]),
                w_branch[l].astype(BF16), w_out[l].astype(BF16), t)
    x2 = _ffn(x2, row(norm_ffn[l]), w_ffn_gate[l].astype(BF16), w_ffn_up[l].astype(BF16),
              w_ffn_down[l].astype(BF16))
    return x2.reshape(b, t, d)
```

```python
import functools
import math

import jax
import jax.numpy as jnp
from jax import lax
from jax.experimental import pallas as pl
from jax.experimental.pallas import tpu as pltpu

F32 = jnp.float32
BF16 = jnp.bfloat16

D_MODEL = 1024
A_HEADS = 4
A_HEAD_DIM = 64
A_V_DIM = 2 * A_HEAD_DIM
B_GROUPS = ((128, 1), (512, 4), (2048, 16))
B_HEADS = 4
B_HEAD_DIM = 128
C_HEADS = 4
C_HEAD_DIM = 128
BRANCH_WIDTH = 512
N_BRANCHES = 3
EPS = 1e-6
ALIBI_MAX_BIAS = 8.0
LAMBDA_INIT = 0.8 - 0.6 * math.exp(-0.3 * 0)

LANES = 128
VMEM_LIMIT_BYTES = 56 * 1024 * 1024

NAT_COLS = 7 * BRANCH_WIDTH
GRP_COLS = 3 * BRANCH_WIDTH
HEADS_PER_CHUNK = BRANCH_WIDTH // LANES
NAT_HEADS = NAT_COLS // LANES
GRP_HEADS = GRP_COLS // LANES
NAT_NORMS = ("half", "half", "none", "full", "full", "none", "full")
GRP_COL_OFFS = (3 * BRANCH_WIDTH, 6 * BRANCH_WIDTH, 9 * BRANCH_WIDTH)
NAT_COL_OFFS = (0, BRANCH_WIDTH, 2 * BRANCH_WIDTH) + GRP_COL_OFFS + (12 * BRANCH_WIDTH,)
GRP_NORMS = ("full", "full", "none")
NEG_BIG = -1e30
LOG2E = 1.4426950408889634
MAX_SAFE_SCORE_BOUND = 40.0


def _rms(x, gain):
    return x * lax.rsqrt(jnp.mean(x * x, axis=-1, keepdims=True) + EPS) * gain


def _dot(a, b):
    return jnp.dot(a, b, preferred_element_type=F32)


def _dot_nt(a, b):
    return lax.dot_general(a, b, (((1,), (1,)), ((), ())), preferred_element_type=F32)


def _const_spec(shape):
    nd = len(shape)
    return pl.BlockSpec(shape, lambda *_: (0,) * nd, pipeline_mode=pl.Buffered(1))


def _head_norm(y, gain, kind):
    if kind == "none":
        return y
    lo_mask = lax.broadcasted_iota(jnp.int32, (1, LANES), 1) < A_HEAD_DIM
    cols = []
    for c in range(0, y.shape[1], LANES):
        z = y[:, c:c + LANES]
        sq = z * z
        s_all = jnp.sum(sq, axis=-1, keepdims=True)
        if kind == "full":
            ms = s_all * (1.0 / LANES)
        else:
            s_lo = jnp.sum(jnp.where(lo_mask, sq, 0.0), axis=-1, keepdims=True)
            ms = jnp.where(lo_mask, s_lo, s_all - s_lo) * (1.0 / A_HEAD_DIM)
        cols.append(z * lax.rsqrt(ms + EPS))
    return jnp.concatenate(cols, axis=1) * gain


def _proj_kernel(x_ref, g_ref, w_ref, gn_ref, gg_ref, on_ref, o1_ref, o2_ref, slab_ref, hp_ref, *, tm):
    h = _rms(x_ref[0], g_ref[...])
    hb = h.astype(BF16)
    bw = BRANCH_WIDTH
    for ci, kind in enumerate(NAT_NORMS):
        c0 = NAT_COL_OFFS[ci]
        res = _head_norm(_dot(hb, w_ref[:, c0:c0 + bw]), gn_ref[ci:ci + 1, :], kind).astype(BF16)
        for hh in range(HEADS_PER_CHUNK):
            on_ref[ci * HEADS_PER_CHUNK + hh, 0] = res[:, hh * LANES:(hh + 1) * LANES]
    n_slabs = D_MODEL // LANES
    for s in range(n_slabs):
        slab_ref[s] = h[:, s * LANES:(s + 1) * LANES]
    for g, dil, o_ref in ((1, 4, o1_ref), (2, 16, o2_ref)):
        n = tm // dil
        for r in range(dil):
            for s in range(n_slabs):
                hp_ref[r * n:(r + 1) * n, s * LANES:(s + 1) * LANES] = (
                    slab_ref[s, pl.ds(r, n, stride=dil), :].astype(BF16))
        hp = hp_ref[...]
        for ci, kind in enumerate(GRP_NORMS):
            c0 = GRP_COL_OFFS[ci] + g * bw
            res = _head_norm(_dot(hp, w_ref[:, c0:c0 + bw]), gg_ref[ci:ci + 1, :], kind).astype(BF16)
            for hh in range(HEADS_PER_CHUNK):
                for r in range(dil):
                    o_ref[ci * HEADS_PER_CHUNK + hh, 0, r] = res[r * n:(r + 1) * n, hh * LANES:(hh + 1) * LANES]


def _project(x, norm_g, w_in, gain_nat, gain_grp, tm=512):
    b, t, d = x.shape
    grid = (b, t // tm)
    return pl.pallas_call(
        functools.partial(_proj_kernel, tm=tm),
        grid=grid,
        in_specs=[
            pl.BlockSpec((1, tm, d), lambda i, j: (i, j, 0)),
            _const_spec((1, d)),
            _const_spec(w_in.shape),
            _const_spec(gain_nat.shape),
            _const_spec(gain_grp.shape),
        ],
        out_specs=[
            pl.BlockSpec((NAT_HEADS, 1, tm, LANES), lambda i, j: (0, i, j, 0)),
            pl.BlockSpec((GRP_HEADS, 1, 4, tm // 4, LANES), lambda i, j: (0, i, 0, j, 0)),
            pl.BlockSpec((GRP_HEADS, 1, 16, tm // 16, LANES), lambda i, j: (0, i, 0, j, 0)),
        ],
        out_shape=[
            jax.ShapeDtypeStruct((NAT_HEADS, b, t, LANES), BF16),
            jax.ShapeDtypeStruct((GRP_HEADS, b, 4, t // 4, LANES), BF16),
            jax.ShapeDtypeStruct((GRP_HEADS, b, 16, t // 16, LANES), BF16),
        ],
        scratch_shapes=[
            pltpu.VMEM((d // LANES, tm, LANES), F32),
            pltpu.VMEM((tm, d), BF16),
        ],
        compiler_params=pltpu.CompilerParams(
            dimension_semantics=("arbitrary", "arbitrary"), vmem_limit_bytes=VMEM_LIMIT_BYTES),
        name="proj",
    )(x, norm_g, w_in, gain_nat, gain_grp)


def _attn_a_kernel(ctl_ref, q_ref, k_ref, v_ref, lq1_ref, lk1_ref, lq2_ref, lk2_ref,
                   sub_ref, o_ref, vt_ref, tab_ref, q2_ref, et_ref, ot_ref, l_ref, *, t, qb, kc, inflight):
    h = pl.program_id(0)
    nblk = t // qb
    slope2 = ctl_ref[h] * LOG2E
    shift = ctl_ref[A_HEADS]
    fast = ctl_ref[A_HEADS + 1] > 0.5
    lo_mask = lax.broadcasted_iota(jnp.int32, (1, LANES), 1) < A_HEAD_DIM

    lam = (jnp.exp(jnp.sum(lq1_ref[...] * lk1_ref[...], keepdims=True))
           - jnp.exp(jnp.sum(lq2_ref[...] * lk2_ref[...], keepdims=True)) + LAMBDA_INIT)

    eye = (lax.broadcasted_iota(jnp.int32, (LANES, LANES), 0)
           == lax.broadcasted_iota(jnp.int32, (LANES, LANES), 1)).astype(BF16)
    vt_ref[...] = _dot_nt(eye, v_ref[...]).astype(BF16)

    q = q_ref[...]
    zero = jnp.zeros_like(q)
    q_lo = jnp.where(lo_mask, q, zero)
    q_hi = jnp.where(lo_mask, zero, q)
    for j in range(nblk):
        q2_ref[j, :qb, :] = q_lo[j * qb:(j + 1) * qb]
        q2_ref[j, qb:, :] = q_hi[j * qb:(j + 1) * qb]

    @pl.when(pl.program_id(1) == 0)
    def _():
        cc = lax.broadcasted_iota(jnp.int32, (2 * t - qb, qb), 0)
        il = lax.broadcasted_iota(jnp.int32, (2 * t - qb, qb), 1)
        tab_ref[...] = (-slope2 * jnp.abs(cc - (t - qb) - il).astype(F32)
                        - jnp.where(fast, shift, 0.0))

    def fold8(x, op):
        return op(x.reshape(x.shape[0] // 8, 8, x.shape[1]), axis=0)

    def block(j, slot, exact_max):
        q2 = q2_ref[j]
        off = pl.multiple_of(t - qb - j * qb, qb)

        def scores(c):
            s = _dot_nt(k_ref[c * kc:(c + 1) * kc, :], q2)
            bias = tab_ref[pl.ds(off + c * kc, kc), :]
            return s + jnp.concatenate([bias, bias], axis=1)

        m = None
        if exact_max:
            for c in range(t // kc):
                cm = fold8(scores(c), jnp.max)
                m = cm if m is None else jnp.maximum(m, cm)
            m = jnp.max(m, axis=0, keepdims=True)
        acc = None
        for c in range(t // kc):
            s = scores(c)
            e = jnp.exp2(s - m if exact_max else s)
            et_ref[slot, c * kc:(c + 1) * kc, :] = e.astype(BF16)
            part = fold8(e, jnp.sum)
            acc = part if acc is None else acc + part
        l_ref[pl.ds(j, 1), :] = jnp.sum(acc, axis=0, keepdims=True)
        ot_ref[j] = _dot(vt_ref[...], et_ref[slot])

    def finish(j):
        ot = ot_ref[j]
        inv = 1.0 / l_ref[pl.ds(j, 1), :]
        o = ot[:, :qb] * inv[:, :qb] - ot[:, qb:] * (lam * inv[:, qb:])
        o = _rms(o.T, sub_ref[...]) * (1.0 - LAMBDA_INIT)
        o_ref[pl.ds(pl.multiple_of(j * qb, qb), qb), :] = o.astype(BF16)

    def run(exact_max):
        for s in range(inflight):
            block(s, s, exact_max)

        def step(i, carry):
            for s in range(inflight):
                finish(inflight * (i - 1) + s)
            for s in range(inflight):
                block(inflight * i + s, s, exact_max)
            return carry
        lax.fori_loop(1, nblk // inflight, step, 0)
        for s in range(inflight):
            finish(nblk - inflight + s)

    @pl.when(fast)
    def _():
        run(False)

    @pl.when(jnp.logical_not(fast))
    def _():
        run(True)


def _attn_a(qkv_nat, ctl, lq1, lk1, lq2, lk2, subln, b, t, qb=128, kc=512, inflight=4):
    n = b * t
    nblk = t // qb
    vec = lambda w: _const_spec((1, w))
    return pl.pallas_call(
        functools.partial(_attn_a_kernel, t=t, qb=qb, kc=kc, inflight=inflight),
        grid=(A_HEADS, b),
        in_specs=[
            pl.BlockSpec(memory_space=pltpu.SMEM),
            pl.BlockSpec((None, t, LANES), lambda h, i: (h, i, 0)),
            pl.BlockSpec((None, t, LANES), lambda h, i: (A_HEADS + h, i, 0)),
            pl.BlockSpec((None, t, LANES), lambda h, i: (2 * A_HEADS + h, i, 0)),
            vec(A_HEAD_DIM), vec(A_HEAD_DIM), vec(A_HEAD_DIM), vec(A_HEAD_DIM),
            vec(A_V_DIM),
        ],
        out_specs=pl.BlockSpec((None, t, LANES), lambda h, i: (h, i, 0)),
        out_shape=jax.ShapeDtypeStruct((A_HEADS, n, LANES), BF16),
        scratch_shapes=[
            pltpu.VMEM((LANES, t), BF16),
            pltpu.VMEM((2 * t - qb, qb), F32),
            pltpu.VMEM((nblk, 2 * qb, LANES), BF16),
            pltpu.VMEM((inflight, t, 2 * qb), BF16),
            pltpu.VMEM((nblk, LANES, 2 * qb), F32),
            pltpu.VMEM((nblk, 2 * qb), F32),
        ],
        compiler_params=pltpu.CompilerParams(
            dimension_semantics=("arbitrary", "arbitrary"), vmem_limit_bytes=VMEM_LIMIT_BYTES),
        name="attn_a",
    )(ctl, qkv_nat, qkv_nat, qkv_nat, lq1, lk1, lq2, lk2, subln)


def _attn_b_kernel(ctl_ref, q0_ref, k0_ref, v0_ref, q1_ref, k1_ref, v1_ref, q2_ref, k2_ref, v2_ref,
                   o_ref, acc_ref, den_ref, kt_ref, *, t, qb):
    h = pl.program_id(1)
    n_ctl = len(B_GROUPS) * B_HEADS
    shift = ctl_ref[n_ctl]
    fast = ctl_ref[n_ctl + 1] > 0.5
    refs = ((q0_ref, k0_ref, v0_ref), (q1_ref, k1_ref, v1_ref), (q2_ref, k2_ref, v2_ref))

    def blocks(g):
        window, dil = B_GROUPS[g]
        n_side = window // (2 * dil)
        sub = t // dil
        wk = min(2 * qb, sub)
        for r in range(dil):
            for j in range(sub // qb):
                i0 = j * qb
                ws = min(max(i0 - n_side, 0), sub - wk)
                yield r, i0, r * sub + i0, r * sub + ws, wk, i0 - ws, n_side

    def bias_table(g, wk, offset, n_side, sub_shift):
        slope2 = ctl_ref[g * B_HEADS + h] * (float(B_GROUPS[g][1]) * LOG2E)
        ql = lax.broadcasted_iota(jnp.int32, (qb, wk), 0)
        kl = lax.broadcasted_iota(jnp.int32, (qb, wk), 1)
        dist = jnp.abs(kl - ql - offset)
        return jnp.where(dist <= n_side, -slope2 * dist.astype(F32) - sub_shift, NEG_BIG)

    def store_rows(ref, g, r, i0, val):
        dil = B_GROUPS[g][1]
        if dil == 1:
            ref[g, i0:i0 + qb, :] = val
        else:
            ref[g, pl.ds(r + dil * i0, qb, stride=dil), :] = val

    @pl.when(fast)
    def _():
        half = qb // 2
        for g in range(len(B_GROUPS)):
            q_ref, k_ref, v_ref = refs[g]
            kt_ref[0] = k_ref[...].T
            kt_ref[1] = jnp.concatenate([k_ref[half:, :], k_ref[:half, :]], axis=0).T
            tables = {}
            for r, i0, qrow, krow, wk, offset, n_side in blocks(g):
                if offset not in tables:
                    tables[offset] = bias_table(g, wk, offset, n_side, shift)
                if krow % qb == 0:
                    kwt = kt_ref[0, :, krow:krow + wk]
                else:
                    kwt = kt_ref[1, :, krow - half:krow - half + wk]
                s = _dot(q_ref[qrow:qrow + qb, :], kwt)
                e = jnp.exp2(s + tables[offset])
                den = jnp.sum(e, axis=-1, keepdims=True)
                store_rows(acc_ref, g, r, i0, _dot(e.astype(BF16), v_ref[krow:krow + wk, :]))
                store_rows(den_ref, g, r, i0, jnp.broadcast_to(den, (qb, LANES)))
        o_ref[...] = ((acc_ref[0] + acc_ref[1] + acc_ref[2])
                      / (den_ref[0] + den_ref[1] + den_ref[2])).astype(BF16)

    @pl.when(jnp.logical_not(fast))
    def _():
        for g in range(len(B_GROUPS)):
            q_ref, k_ref, v_ref = refs[g]
            tables = {}
            for r, i0, qrow, krow, wk, offset, n_side in blocks(g):
                if offset not in tables:
                    tables[offset] = bias_table(g, wk, offset, n_side, 0.0)
                s = _dot_nt(q_ref[qrow:qrow + qb, :], k_ref[krow:krow + wk, :]) + tables[offset]
                m = jnp.max(s, axis=-1, keepdims=True)
                e = jnp.exp2(s - m)
                l = jnp.sum(e, axis=-1, keepdims=True)
                o = _dot(e.astype(BF16), v_ref[krow:krow + wk, :]) * (1.0 / l)
                store_rows(acc_ref, g, r, i0, o)
                store_rows(den_ref, g, r, i0, jnp.broadcast_to(m + jnp.log2(l), (qb, LANES)))
        l0, l1, l2 = den_ref[0], den_ref[1], den_ref[2]
        m = jnp.maximum(jnp.maximum(l0, l1), l2)
        e0, e1, e2 = jnp.exp2(l0 - m), jnp.exp2(l1 - m), jnp.exp2(l2 - m)
        o_ref[...] = ((e0 * acc_ref[0] + e1 * acc_ref[1] + e2 * acc_ref[2]) / (e0 + e1 + e2)).astype(BF16)


def _attn_b(qkv_nat, qkv_g1, qkv_g2, ctl, b, t, qb=128):
    n = b * t
    blk = lambda off: pl.BlockSpec((None, t, LANES), lambda i, h: (off + h, i, 0))
    return pl.pallas_call(
        functools.partial(_attn_b_kernel, t=t, qb=qb),
        grid=(b, B_HEADS),
        in_specs=[
            pl.BlockSpec(memory_space=pltpu.SMEM),
            blk(12), blk(16), blk(20),
            blk(0), blk(4), blk(8),
            blk(0), blk(4), blk(8),
        ],
        out_specs=pl.BlockSpec((None, t, LANES), lambda i, h: (h, i, 0)),
        out_shape=jax.ShapeDtypeStruct((B_HEADS, n, LANES), BF16),
        scratch_shapes=[
            pltpu.VMEM((len(B_GROUPS), t, LANES), F32),
            pltpu.VMEM((len(B_GROUPS), t, LANES), F32),
            pltpu.VMEM((2, LANES, t), BF16),
        ],
        compiler_params=pltpu.CompilerParams(
            dimension_semantics=("arbitrary", "arbitrary"), vmem_limit_bytes=VMEM_LIMIT_BYTES),
        name="attn_b",
    )(ctl, qkv_nat, qkv_nat, qkv_nat, qkv_g1, qkv_g1, qkv_g1, qkv_g2, qkv_g2, qkv_g2)


def _mem_kv_kernel(mem_ref, gm_ref, w_ref, gk_ref, o_ref):
    mn = _rms(mem_ref[0], gm_ref[...]).astype(BF16)
    kv = _dot(mn, w_ref[...])
    half = C_HEADS * C_HEAD_DIM
    o_ref[0, :, :half] = _head_norm(kv[:, :half], gk_ref[...], "full").astype(BF16)
    o_ref[0, :, half:] = kv[:, half:].astype(BF16)


def _mem_kv(mem, gm, w_kv, gk4):
    b, n_mem, d = mem.shape
    cols = w_kv.shape[1]
    return pl.pallas_call(
        _mem_kv_kernel,
        grid=(b,),
        in_specs=[
            pl.BlockSpec((1, n_mem, d), lambda i: (i, 0, 0)),
            _const_spec((1, d)), _const_spec((d, cols)), _const_spec((1, cols // 2)),
        ],
        out_specs=pl.BlockSpec((1, n_mem, cols), lambda i: (i, 0, 0)),
        out_shape=jax.ShapeDtypeStruct((b, n_mem, cols), BF16),
        compiler_params=pltpu.CompilerParams(
            dimension_semantics=("arbitrary",), vmem_limit_bytes=VMEM_LIMIT_BYTES),
        name="mem_kv",
    )(mem, gm, w_kv, gk4)


def _merge_kernel(x_ref, a_ref, b_ref, cq_ref, ckv_ref, g_ref, wg_ref, bg_ref, wb_ref, wo_ref, o_ref):
    x = x_ref[...]
    hb = _rms(x, g_ref[...]).astype(BF16)
    d = x.shape[-1]
    half = C_HEADS * C_HEAD_DIM

    def gated(g, branch):
        gate = jax.nn.sigmoid(_dot(hb, wg_ref[:, g * d:(g + 1) * d]) + bg_ref[:, g * d:(g + 1) * d])
        return gate * _dot(branch, wb_ref[g])

    wide = lambda ref: jnp.concatenate([ref[hh] for hh in range(HEADS_PER_CHUNK)], axis=1)
    acc = gated(0, wide(a_ref)) + gated(1, wide(b_ref))
    heads = []
    for hh in range(C_HEADS):
        cols = slice(hh * C_HEAD_DIM, (hh + 1) * C_HEAD_DIM)
        s = _dot_nt(cq_ref[hh], ckv_ref[0, :, cols])
        e = jnp.exp2(s - jnp.max(s, axis=-1, keepdims=True))
        inv = 1.0 / jnp.sum(e, axis=-1, keepdims=True)
        heads.append(_dot(e.astype(BF16), ckv_ref[0, :, half + hh * C_HEAD_DIM:half + (hh + 1) * C_HEAD_DIM]) * inv)
    acc = acc + gated(2, jnp.concatenate(heads, axis=1).astype(BF16))
    o_ref[...] = x + _dot(acc.astype(BF16), wo_ref[...])


def _merge(x2, out_a, out_b, qkv_nat, ckv, norm_g, w_gate, b_gate, w_branch, w_out, t, tm=512):
    n, d = x2.shape
    n_mem, kv_cols = ckv.shape[1], ckv.shape[2]
    row = lambda w: pl.BlockSpec((tm, w), lambda i: (i, 0))
    heads = lambda chunk: pl.BlockSpec((HEADS_PER_CHUNK, tm, LANES), lambda i: (chunk, i, 0))
    return pl.pallas_call(
        _merge_kernel,
        grid=(n // tm,),
        in_specs=[
            row(d), heads(0), heads(0), heads(NAT_HEADS // HEADS_PER_CHUNK - 1),
            pl.BlockSpec((1, n_mem, kv_cols), lambda i: (i // (t // tm), 0, 0)),
            _const_spec((1, d)),
            _const_spec((d, N_BRANCHES * d)),
            _const_spec((1, N_BRANCHES * d)),
            _const_spec((N_BRANCHES, BRANCH_WIDTH, d)),
            _const_spec((d, d)),
        ],
        out_specs=row(d),
        out_shape=jax.ShapeDtypeStruct((n, d), F32),
        compiler_params=pltpu.CompilerParams(
            dimension_semantics=("arbitrary",), vmem_limit_bytes=VMEM_LIMIT_BYTES),
        name="merge",
    )(x2, out_a, out_b, qkv_nat, ckv, norm_g, w_gate, b_gate, w_branch, w_out)


def _ffn_kernel(x_ref, g_ref, wg_ref, wu_ref, wd_ref, o_ref, *, chunks):
    x = x_ref[...]
    hb = _rms(x, g_ref[...]).astype(BF16)
    acc = x
    for c0, c1 in chunks:
        gt = _dot(hb, wg_ref[:, c0:c1])
        up = _dot(hb, wu_ref[:, c0:c1])
        acc = acc + _dot((jax.nn.silu(gt) * up).astype(BF16), wd_ref[c0:c1, :])
    o_ref[...] = acc


def _ffn(x2, norm_g, w_gate, w_up, w_down, tm=512, fc=768):
    n, d = x2.shape
    d_ff = w_gate.shape[1]
    chunks = tuple((c, min(c + fc, d_ff)) for c in range(0, d_ff, fc))
    row = pl.BlockSpec((tm, d), lambda i: (i, 0))
    return pl.pallas_call(
        functools.partial(_ffn_kernel, chunks=chunks),
        grid=(n // tm,),
        in_specs=[row, _const_spec((1, d)), _const_spec((d, d_ff)), _const_spec((d, d_ff)),
                  _const_spec((d_ff, d))],
        out_specs=row,
        out_shape=jax.ShapeDtypeStruct((n, d), F32),
        compiler_params=pltpu.CompilerParams(
            dimension_semantics=("arbitrary",), vmem_limit_bytes=VMEM_LIMIT_BYTES),
        name="ffn",
    )(x2, norm_g, w_gate, w_up, w_down)


def _score_ctl(slopes, head_dim, gq, gk):
    bound = math.sqrt(head_dim) * jnp.max(jnp.abs(gq)) * jnp.max(jnp.abs(gk))
    return jnp.concatenate([slopes.reshape(-1),
                            jnp.stack([bound * LOG2E, (bound <= MAX_SAFE_SCORE_BOUND).astype(F32)])])


def kernel(x, mem, norm_mix, w_in, w_gate, b_gate, a_q_norm, a_k_norm, a_lambda_q1, a_lambda_k1, a_lambda_q2,
           a_lambda_k2, a_subln, b_q_norm, b_k_norm, mem_norm, w_mem_kv, c_q_norm, c_k_norm, w_branch, w_out,
           norm_ffn, w_ffn_gate, w_ffn_up, w_ffn_down):
    b, t, d = x.shape
    n = b * t
    n_groups = len(B_GROUPS)
    slopes_a = jnp.exp2(-ALIBI_MAX_BIAS * jnp.arange(1, A_HEADS + 1, dtype=F32) / A_HEADS)
    nb = n_groups * B_HEADS
    slopes_b = jnp.exp2(-ALIBI_MAX_BIAS * jnp.arange(1, nb + 1, dtype=F32) / nb)

    l = 0
    bw = BRANCH_WIDTH
    row = lambda v: v.reshape(1, -1)
    tiled = lambda v: jnp.tile(v, bw // v.shape[0])
    ones = jnp.ones((bw,), F32)
    a_qs = A_HEAD_DIM ** -0.5 * LOG2E
    b_qs = B_HEAD_DIM ** -0.5 * LOG2E
    c_qs = C_HEAD_DIM ** -0.5 * LOG2E
    gain_nat = jnp.stack([tiled(a_q_norm[l]) * a_qs, tiled(a_k_norm[l]), ones, tiled(b_q_norm[l]) * b_qs,
                          tiled(b_k_norm[l]), ones, tiled(c_q_norm[l]) * c_qs])
    gain_grp = jnp.stack([tiled(b_q_norm[l]) * b_qs, tiled(b_k_norm[l]), ones])

    qkv_nat, qkv_g1, qkv_g2 = _project(x, row(norm_mix[l]), w_in[l], gain_nat, gain_grp)
    qkv_nat = qkv_nat.reshape(NAT_HEADS, n, LANES)
    qkv_g1 = qkv_g1.reshape(GRP_HEADS, n, LANES)
    qkv_g2 = qkv_g2.reshape(GRP_HEADS, n, LANES)

    out_a = _attn_a(qkv_nat, _score_ctl(slopes_a, A_HEAD_DIM, a_q_norm[l], a_k_norm[l]),
                    row(a_lambda_q1[l]), row(a_lambda_k1[l]), row(a_lambda_q2[l]), row(a_lambda_k2[l]),
                    row(a_subln[l]), b, t)
    out_b = _attn_b(qkv_nat, qkv_g1, qkv_g2, _score_ctl(slopes_b, B_HEAD_DIM, b_q_norm[l], b_k_norm[l]), b, t)
    ckv = _mem_kv(mem, row(mem_norm[l]), w_mem_kv[l], row(tiled(c_k_norm[l])))

    x2 = x.reshape(n, d)
    x2 = _merge(x2, out_a, out_b, qkv_nat, ckv, row(norm_mix[l]), w_gate[l], row(b_gate[l]), w_branch[l], w_out[l], t)
    x2 = _ffn(x2, row(norm_ffn[l]), w_ffn_gate[l], w_ffn_up[l], w_ffn_down[l])
    return x2.reshape(b, t, d)
```

```python
import functools
import math

import jax
import jax.numpy as jnp
from jax import lax
from jax.experimental import pallas as pl
from jax.experimental.pallas import tpu as pltpu

F32 = jnp.float32
BF16 = jnp.bfloat16

D_MODEL = 1024
A_HEADS = 4
A_HEAD_DIM = 64
A_V_DIM = 2 * A_HEAD_DIM
B_GROUPS = ((128, 1), (512, 4), (2048, 16))
B_HEADS = 4
B_HEAD_DIM = 128
C_HEADS = 4
C_HEAD_DIM = 128
BRANCH_WIDTH = 512
N_BRANCHES = 3
EPS = 1e-6
ALIBI_MAX_BIAS = 8.0
LAMBDA_INIT = 0.8 - 0.6 * math.exp(-0.3 * 0)

LANES = 128
VMEM_LIMIT_BYTES = 56 * 1024 * 1024

NAT_COLS = 7 * BRANCH_WIDTH
GRP_COLS = 3 * BRANCH_WIDTH
HEADS_PER_CHUNK = BRANCH_WIDTH // LANES
NAT_HEADS = NAT_COLS // LANES
GRP_HEADS = GRP_COLS // LANES
NAT_NORMS = ("half", "half", "none", "full", "full", "none", "full")
GRP_COL_OFFS = (3 * BRANCH_WIDTH, 6 * BRANCH_WIDTH, 9 * BRANCH_WIDTH)
NAT_COL_OFFS = (0, BRANCH_WIDTH, 2 * BRANCH_WIDTH) + GRP_COL_OFFS + (12 * BRANCH_WIDTH,)
GRP_NORMS = ("full", "full", "none")
NEG_BIG = -1e30
LOG2E = 1.4426950408889634
MAX_SAFE_SCORE_BOUND = 40.0


def _rms(x, gain):
    return x * lax.rsqrt(jnp.mean(x * x, axis=-1, keepdims=True) + EPS) * gain


def _dot(a, b):
    return jnp.dot(a, b, preferred_element_type=F32)


def _dot_nt(a, b):
    return lax.dot_general(a, b, (((1,), (1,)), ((), ())), preferred_element_type=F32)


def _const_spec(shape):
    nd = len(shape)
    return pl.BlockSpec(shape, lambda *_: (0,) * nd, pipeline_mode=pl.Buffered(1))


def _head_norm(y, gain, kind):
    if kind == "none":
        return y
    lo_mask = lax.broadcasted_iota(jnp.int32, (1, LANES), 1) < A_HEAD_DIM
    cols = []
    for c in range(0, y.shape[1], LANES):
        z = y[:, c:c + LANES]
        sq = z * z
        s_all = jnp.sum(sq, axis=-1, keepdims=True)
        if kind == "full":
            ms = s_all * (1.0 / LANES)
        else:
            s_lo = jnp.sum(jnp.where(lo_mask, sq, 0.0), axis=-1, keepdims=True)
            ms = jnp.where(lo_mask, s_lo, s_all - s_lo) * (1.0 / A_HEAD_DIM)
        cols.append(z * lax.rsqrt(ms + EPS))
    return jnp.concatenate(cols, axis=1) * gain


def _proj_kernel(x_ref, g_ref, w_ref, gn_ref, gg_ref, on_ref, o1_ref, o2_ref, slab_ref, hp_ref, *, tm):
    h = _rms(x_ref[0], g_ref[...])
    hb = h.astype(BF16)
    bw = BRANCH_WIDTH
    for ci, kind in enumerate(NAT_NORMS):
        c0 = NAT_COL_OFFS[ci]
        res = _head_norm(_dot(hb, w_ref[:, c0:c0 + bw]), gn_ref[ci:ci + 1, :], kind).astype(BF16)
        for hh in range(HEADS_PER_CHUNK):
            on_ref[ci * HEADS_PER_CHUNK + hh, 0] = res[:, hh * LANES:(hh + 1) * LANES]
    n_slabs = D_MODEL // LANES
    for s in range(n_slabs):
        slab_ref[s] = h[:, s * LANES:(s + 1) * LANES]
    for g, dil, o_ref in ((1, 4, o1_ref), (2, 16, o2_ref)):
        n = tm // dil
        for r in range(dil):
            for s in range(n_slabs):
                hp_ref[r * n:(r + 1) * n, s * LANES:(s + 1) * LANES] = (
                    slab_ref[s, pl.ds(r, n, stride=dil), :].astype(BF16))
        hp = hp_ref[...]
        for ci, kind in enumerate(GRP_NORMS):
            c0 = GRP_COL_OFFS[ci] + g * bw
            res = _head_norm(_dot(hp, w_ref[:, c0:c0 + bw]), gg_ref[ci:ci + 1, :], kind).astype(BF16)
            for hh in range(HEADS_PER_CHUNK):
                for r in range(dil):
                    o_ref[ci * HEADS_PER_CHUNK + hh, 0, r] = res[r * n:(r + 1) * n, hh * LANES:(hh + 1) * LANES]


def _project(x, norm_g, w_in, gain_nat, gain_grp, tm=512):
    b, t, d = x.shape
    grid = (b, t // tm)
    return pl.pallas_call(
        functools.partial(_proj_kernel, tm=tm),
        grid=grid,
        in_specs=[
            pl.BlockSpec((1, tm, d), lambda i, j: (i, j, 0)),
            _const_spec((1, d)),
            _const_spec(w_in.shape),
            _const_spec(gain_nat.shape),
            _const_spec(gain_grp.shape),
        ],
        out_specs=[
            pl.BlockSpec((NAT_HEADS, 1, tm, LANES), lambda i, j: (0, i, j, 0)),
            pl.BlockSpec((GRP_HEADS, 1, 4, tm // 4, LANES), lambda i, j: (0, i, 0, j, 0)),
            pl.BlockSpec((GRP_HEADS, 1, 16, tm // 16, LANES), lambda i, j: (0, i, 0, j, 0)),
        ],
        out_shape=[
            jax.ShapeDtypeStruct((NAT_HEADS, b, t, LANES), BF16),
            jax.ShapeDtypeStruct((GRP_HEADS, b, 4, t // 4, LANES), BF16),
            jax.ShapeDtypeStruct((GRP_HEADS, b, 16, t // 16, LANES), BF16),
        ],
        scratch_shapes=[
            pltpu.VMEM((d // LANES, tm, LANES), F32),
            pltpu.VMEM((tm, d), BF16),
        ],
        compiler_params=pltpu.CompilerParams(
            dimension_semantics=("arbitrary", "arbitrary"), vmem_limit_bytes=VMEM_LIMIT_BYTES),
        name="proj",
    )(x, norm_g, w_in, gain_nat, gain_grp)


def _attn_a_kernel(ctl_ref, q_ref, k_ref, v_ref, lq1_ref, lk1_ref, lq2_ref, lk2_ref,
                   sub_ref, o_ref, vt_ref, tab_ref, et_ref, ot_ref, l_ref, *, t, qb, kc, inflight):
    h = pl.program_id(0)
    nblk = t // qb
    slope2 = ctl_ref[h] * LOG2E
    shift = ctl_ref[A_HEADS]
    fast = ctl_ref[A_HEADS + 1] > 0.5
    lo_mask = lax.broadcasted_iota(jnp.int32, (1, LANES), 1) < A_HEAD_DIM

    lam = (jnp.exp(jnp.sum(lq1_ref[...] * lk1_ref[...], keepdims=True))
           - jnp.exp(jnp.sum(lq2_ref[...] * lk2_ref[...], keepdims=True)) + LAMBDA_INIT)

    vt_ref[...] = v_ref[...].T

    @pl.when(pl.program_id(1) == 0)
    def _():
        cc = lax.broadcasted_iota(jnp.int32, (2 * t - qb, qb), 0)
        il = lax.broadcasted_iota(jnp.int32, (2 * t - qb, qb), 1)
        tab_ref[...] = (-slope2 * jnp.abs(cc - (t - qb) - il).astype(F32)
                        - jnp.where(fast, shift, 0.0))

    def fold8(x, op):
        return op(x.reshape(x.shape[0] // 8, 8, x.shape[1]), axis=0)

    def block(j, slot, exact_max):
        q = q_ref[pl.ds(pl.multiple_of(j * qb, qb), qb), :]
        zero = jnp.zeros_like(q)
        q2 = jnp.concatenate([jnp.where(lo_mask, q, zero), jnp.where(lo_mask, zero, q)], axis=0)
        off = pl.multiple_of(t - qb - j * qb, qb)

        def scores(c):
            s = _dot_nt(k_ref[c * kc:(c + 1) * kc, :], q2)
            bias = tab_ref[pl.ds(off + c * kc, kc), :]
            return s + jnp.concatenate([bias, bias], axis=1)

        m = None
        if exact_max:
            for c in range(t // kc):
                cm = fold8(scores(c), jnp.max)
                m = cm if m is None else jnp.maximum(m, cm)
            m = jnp.max(m, axis=0, keepdims=True)
        acc = None
        for c in range(t // kc):
            s = scores(c)
            e = jnp.exp2(s - m if exact_max else s)
            et_ref[slot, c * kc:(c + 1) * kc, :] = e.astype(BF16)
            part = fold8(e, jnp.sum)
            acc = part if acc is None else acc + part
        l_ref[pl.ds(j, 1), :] = jnp.sum(acc, axis=0, keepdims=True)
        ot_ref[j] = _dot(vt_ref[...], et_ref[slot])

    def finish(j):
        ot = ot_ref[j]
        inv = 1.0 / l_ref[pl.ds(j, 1), :]
        o = ot[:, :qb] * inv[:, :qb] - ot[:, qb:] * (lam * inv[:, qb:])
        o = _rms(o.T, sub_ref[...]) * (1.0 - LAMBDA_INIT)
        o_ref[pl.ds(pl.multiple_of(j * qb, qb), qb), :] = o.astype(BF16)

    def run(exact_max):
        for s in range(inflight):
            block(s, s, exact_max)

        def step(i, carry):
            for s in range(inflight):
                finish(inflight * (i - 1) + s)
            for s in range(inflight):
                block(inflight * i + s, s, exact_max)
            return carry
        lax.fori_loop(1, nblk // inflight, step, 0)
        for s in range(inflight):
            finish(nblk - inflight + s)

    @pl.when(fast)
    def _():
        run(False)

    @pl.when(jnp.logical_not(fast))
    def _():
        run(True)


def _attn_a(qkv_nat, ctl, lq1, lk1, lq2, lk2, subln, b, t, qb=128, kc=512, inflight=4):
    n = b * t
    nblk = t // qb
    vec = lambda w: _const_spec((1, w))
    return pl.pallas_call(
        functools.partial(_attn_a_kernel, t=t, qb=qb, kc=kc, inflight=inflight),
        grid=(A_HEADS, b),
        in_specs=[
            pl.BlockSpec(memory_space=pltpu.SMEM),
            pl.BlockSpec((None, t, LANES), lambda h, i: (h, i, 0)),
            pl.BlockSpec((None, t, LANES), lambda h, i: (A_HEADS + h, i, 0)),
            pl.BlockSpec((None, t, LANES), lambda h, i: (2 * A_HEADS + h, i, 0)),
            vec(A_HEAD_DIM), vec(A_HEAD_DIM), vec(A_HEAD_DIM), vec(A_HEAD_DIM),
            vec(A_V_DIM),
        ],
        out_specs=pl.BlockSpec((None, t, LANES), lambda h, i: (h, i, 0)),
        out_shape=jax.ShapeDtypeStruct((A_HEADS, n, LANES), BF16),
        scratch_shapes=[
            pltpu.VMEM((LANES, t), BF16),
            pltpu.VMEM((2 * t - qb, qb), F32),
            pltpu.VMEM((inflight, t, 2 * qb), BF16),
            pltpu.VMEM((nblk, LANES, 2 * qb), F32),
            pltpu.VMEM((nblk, 2 * qb), F32),
        ],
        compiler_params=pltpu.CompilerParams(
            dimension_semantics=("arbitrary", "arbitrary"), vmem_limit_bytes=VMEM_LIMIT_BYTES),
        name="attn_a",
    )(ctl, qkv_nat, qkv_nat, qkv_nat, lq1, lk1, lq2, lk2, subln)


def _attn_b_kernel(ctl_ref, q0_ref, k0_ref, v0_ref, q1_ref, k1_ref, v1_ref, q2_ref, k2_ref, v2_ref,
                   o_ref, acc_ref, den_ref, kt_ref, *, t, qb):
    h = pl.program_id(1)
    n_ctl = len(B_GROUPS) * B_HEADS
    shift = ctl_ref[n_ctl]
    fast = ctl_ref[n_ctl + 1] > 0.5
    refs = ((q0_ref, k0_ref, v0_ref), (q1_ref, k1_ref, v1_ref), (q2_ref, k2_ref, v2_ref))

    def blocks(g):
        window, dil = B_GROUPS[g]
        n_side = window // (2 * dil)
        sub = t // dil
        wk = min(2 * qb, sub)
        for r in range(dil):
            for j in range(sub // qb):
                i0 = j * qb
                ws = min(max(i0 - n_side, 0), sub - wk)
                yield r, i0, r * sub + i0, r * sub + ws, wk, i0 - ws, n_side

    def bias_table(g, wk, offset, n_side, sub_shift):
        slope2 = ctl_ref[g * B_HEADS + h] * (float(B_GROUPS[g][1]) * LOG2E)
        ql = lax.broadcasted_iota(jnp.int32, (qb, wk), 0)
        kl = lax.broadcasted_iota(jnp.int32, (qb, wk), 1)
        dist = jnp.abs(kl - ql - offset)
        return jnp.where(dist <= n_side, -slope2 * dist.astype(F32) - sub_shift, NEG_BIG)

    def store_rows(ref, g, r, i0, val):
        dil = B_GROUPS[g][1]
        if dil == 1:
            ref[g, i0:i0 + qb, :] = val
        else:
            ref[g, pl.ds(r + dil * i0, qb, stride=dil), :] = val

    @pl.when(fast)
    def _():
        half = qb // 2
        for g in range(len(B_GROUPS)):
            q_ref, k_ref, v_ref = refs[g]
            kt_ref[0] = k_ref[...].T
            kt_ref[1] = jnp.concatenate([k_ref[half:, :], k_ref[:half, :]], axis=0).T
            tables = {}
            for r, i0, qrow, krow, wk, offset, n_side in blocks(g):
                if offset not in tables:
                    tables[offset] = bias_table(g, wk, offset, n_side, shift)
                if krow % qb == 0:
                    kwt = kt_ref[0, :, krow:krow + wk]
                else:
                    kwt = kt_ref[1, :, krow - half:krow - half + wk]
                s = _dot(q_ref[qrow:qrow + qb, :], kwt)
                e = jnp.exp2(s + tables[offset])
                den = jnp.sum(e, axis=-1, keepdims=True)
                store_rows(acc_ref, g, r, i0, _dot(e.astype(BF16), v_ref[krow:krow + wk, :]))
                store_rows(den_ref, g, r, i0, jnp.broadcast_to(den, (qb, LANES)))
        o_ref[...] = ((acc_ref[0] + acc_ref[1] + acc_ref[2])
                      / (den_ref[0] + den_ref[1] + den_ref[2])).astype(BF16)

    @pl.when(jnp.logical_not(fast))
    def _():
        for g in range(len(B_GROUPS)):
            q_ref, k_ref, v_ref = refs[g]
            tables = {}
            for r, i0, qrow, krow, wk, offset, n_side in blocks(g):
                if offset not in tables:
                    tables[offset] = bias_table(g, wk, offset, n_side, 0.0)
                s = _dot_nt(q_ref[qrow:qrow + qb, :], k_ref[krow:krow + wk, :]) + tables[offset]
                m = jnp.max(s, axis=-1, keepdims=True)
                e = jnp.exp2(s - m)
                l = jnp.sum(e, axis=-1, keepdims=True)
                o = _dot(e.astype(BF16), v_ref[krow:krow + wk, :]) * (1.0 / l)
                store_rows(acc_ref, g, r, i0, o)
                store_rows(den_ref, g, r, i0, jnp.broadcast_to(m + jnp.log2(l), (qb, LANES)))
        l0, l1, l2 = den_ref[0], den_ref[1], den_ref[2]
        m = jnp.maximum(jnp.maximum(l0, l1), l2)
        e0, e1, e2 = jnp.exp2(l0 - m), jnp.exp2(l1 - m), jnp.exp2(l2 - m)
        o_ref[...] = ((e0 * acc_ref[0] + e1 * acc_ref[1] + e2 * acc_ref[2]) / (e0 + e1 + e2)).astype(BF16)


def _attn_b(qkv_nat, qkv_g1, qkv_g2, ctl, b, t, qb=128):
    n = b * t
    blk = lambda off: pl.BlockSpec((None, t, LANES), lambda i, h: (off + h, i, 0))
    return pl.pallas_call(
        functools.partial(_attn_b_kernel, t=t, qb=qb),
        grid=(b, B_HEADS),
        in_specs=[
            pl.BlockSpec(memory_space=pltpu.SMEM),
            blk(12), blk(16), blk(20),
            blk(0), blk(4), blk(8),
            blk(0), blk(4), blk(8),
        ],
        out_specs=pl.BlockSpec((None, t, LANES), lambda i, h: (h, i, 0)),
        out_shape=jax.ShapeDtypeStruct((B_HEADS, n, LANES), BF16),
        scratch_shapes=[
            pltpu.VMEM((len(B_GROUPS), t, LANES), F32),
            pltpu.VMEM((len(B_GROUPS), t, LANES), F32),
            pltpu.VMEM((2, LANES, t), BF16),
        ],
        compiler_params=pltpu.CompilerParams(
            dimension_semantics=("arbitrary", "arbitrary"), vmem_limit_bytes=VMEM_LIMIT_BYTES),
        name="attn_b",
    )(ctl, qkv_nat, qkv_nat, qkv_nat, qkv_g1, qkv_g1, qkv_g1, qkv_g2, qkv_g2, qkv_g2)


def _merge_kernel(x_ref, a_ref, b_ref, cq_ref, mem_ref, gm_ref, wkv_ref, gk_ref, g_ref, wg_ref, bg_ref, wb_ref,
                  wo_ref, o_ref, ckv_ref, *, tiles_per_batch):
    half = C_HEADS * C_HEAD_DIM

    @pl.when(pl.program_id(0) % tiles_per_batch == 0)
    def _():
        mn = _rms(mem_ref[0], gm_ref[...]).astype(BF16)
        kv = _dot(mn, wkv_ref[...])
        ckv_ref[:, :half] = _head_norm(kv[:, :half], gk_ref[...], "full").astype(BF16)
        ckv_ref[:, half:] = kv[:, half:].astype(BF16)

    x = x_ref[...]
    hb = _rms(x, g_ref[...]).astype(BF16)
    d = x.shape[-1]

    def gated(g, branch):
        gate = jax.nn.sigmoid(_dot(hb, wg_ref[:, g * d:(g + 1) * d]) + bg_ref[:, g * d:(g + 1) * d])
        return gate * _dot(branch, wb_ref[g])

    wide = lambda ref: jnp.concatenate([ref[hh] for hh in range(HEADS_PER_CHUNK)], axis=1)
    acc = gated(0, wide(a_ref)) + gated(1, wide(b_ref))
    heads = []
    for hh in range(C_HEADS):
        cols = slice(hh * C_HEAD_DIM, (hh + 1) * C_HEAD_DIM)
        s = _dot_nt(cq_ref[hh], ckv_ref[:, cols])
        e = jnp.exp2(s - jnp.max(s, axis=-1, keepdims=True))
        inv = 1.0 / jnp.sum(e, axis=-1, keepdims=True)
        heads.append(_dot(e.astype(BF16), ckv_ref[:, half + hh * C_HEAD_DIM:half + (hh + 1) * C_HEAD_DIM]) * inv)
    acc = acc + gated(2, jnp.concatenate(heads, axis=1).astype(BF16))
    o_ref[...] = x + _dot(acc.astype(BF16), wo_ref[...])


def _merge(x2, out_a, out_b, qkv_nat, mem, gm, w_kv, gk4, norm_g, w_gate, b_gate, w_branch, w_out, t, tm=512):
    n, d = x2.shape
    n_mem, kv_cols = mem.shape[1], w_kv.shape[1]
    row = lambda w: pl.BlockSpec((tm, w), lambda i: (i, 0))
    heads = lambda chunk: pl.BlockSpec((HEADS_PER_CHUNK, tm, LANES), lambda i: (chunk, i, 0))
    return pl.pallas_call(
        functools.partial(_merge_kernel, tiles_per_batch=t // tm),
        grid=(n // tm,),
        in_specs=[
            row(d), heads(0), heads(0), heads(NAT_HEADS // HEADS_PER_CHUNK - 1),
            pl.BlockSpec((1, n_mem, d), lambda i: (i // (t // tm), 0, 0)),
            _const_spec((1, d)), _const_spec((d, kv_cols)), _const_spec((1, kv_cols // 2)),
            _const_spec((1, d)),
            _const_spec((d, N_BRANCHES * d)),
            _const_spec((1, N_BRANCHES * d)),
            _const_spec((N_BRANCHES, BRANCH_WIDTH, d)),
            _const_spec((d, d)),
        ],
        out_specs=row(d),
        out_shape=jax.ShapeDtypeStruct((n, d), F32),
        scratch_shapes=[pltpu.VMEM((n_mem, kv_cols), BF16)],
        compiler_params=pltpu.CompilerParams(
            dimension_semantics=("arbitrary",), vmem_limit_bytes=VMEM_LIMIT_BYTES),
        name="merge",
    )(x2, out_a, out_b, qkv_nat, mem, gm, w_kv, gk4, norm_g, w_gate, b_gate, w_branch, w_out)


def _ffn_kernel(x_ref, g_ref, wg_ref, wu_ref, wd_ref, o_ref, *, chunks):
    x = x_ref[...]
    hb = _rms(x, g_ref[...]).astype(BF16)
    acc = x
    for c0, c1 in chunks:
        gt = _dot(hb, wg_ref[:, c0:c1])
        up = _dot(hb, wu_ref[:, c0:c1])
        acc = acc + _dot((jax.nn.silu(gt) * up).astype(BF16), wd_ref[c0:c1, :])
    o_ref[...] = acc


def _ffn(x2, norm_g, w_gate, w_up, w_down, tm=512, fc=768):
    n, d = x2.shape
    d_ff = w_gate.shape[1]
    chunks = tuple((c, min(c + fc, d_ff)) for c in range(0, d_ff, fc))
    row = pl.BlockSpec((tm, d), lambda i: (i, 0))
    return pl.pallas_call(
        functools.partial(_ffn_kernel, chunks=chunks),
        grid=(n // tm,),
        in_specs=[row, _const_spec((1, d)), _const_spec((d, d_ff)), _const_spec((d, d_ff)),
                  _const_spec((d_ff, d))],
        out_specs=row,
        out_shape=jax.ShapeDtypeStruct((n, d), F32),
        compiler_params=pltpu.CompilerParams(
            dimension_semantics=("arbitrary",), vmem_limit_bytes=VMEM_LIMIT_BYTES),
        name="ffn",
    )(x2, norm_g, w_gate, w_up, w_down)


def _score_ctl(slopes, head_dim, gq, gk):
    bound = math.sqrt(head_dim) * jnp.max(jnp.abs(gq)) * jnp.max(jnp.abs(gk))
    return jnp.concatenate([slopes.reshape(-1),
                            jnp.stack([bound * LOG2E, (bound <= MAX_SAFE_SCORE_BOUND).astype(F32)])])


def kernel(x, mem, norm_mix, w_in, w_gate, b_gate, a_q_norm, a_k_norm, a_lambda_q1, a_lambda_k1, a_lambda_q2,
           a_lambda_k2, a_subln, b_q_norm, b_k_norm, mem_norm, w_mem_kv, c_q_norm, c_k_norm, w_branch, w_out,
           norm_ffn, w_ffn_gate, w_ffn_up, w_ffn_down):
    b, t, d = x.shape
    n = b * t
    n_groups = len(B_GROUPS)
    slopes_a = jnp.exp2(-ALIBI_MAX_BIAS * jnp.arange(1, A_HEADS + 1, dtype=F32) / A_HEADS)
    nb = n_groups * B_HEADS
    slopes_b = jnp.exp2(-ALIBI_MAX_BIAS * jnp.arange(1, nb + 1, dtype=F32) / nb)

    l = 0
    bw = BRANCH_WIDTH
    row = lambda v: v.reshape(1, -1)
    tiled = lambda v: jnp.tile(v, bw // v.shape[0])
    ones = jnp.ones((bw,), F32)
    a_qs = A_HEAD_DIM ** -0.5 * LOG2E
    b_qs = B_HEAD_DIM ** -0.5 * LOG2E
    c_qs = C_HEAD_DIM ** -0.5 * LOG2E
    gain_nat = jnp.stack([tiled(a_q_norm[l]) * a_qs, tiled(a_k_norm[l]), ones, tiled(b_q_norm[l]) * b_qs,
                          tiled(b_k_norm[l]), ones, tiled(c_q_norm[l]) * c_qs])
    gain_grp = jnp.stack([tiled(b_q_norm[l]) * b_qs, tiled(b_k_norm[l]), ones])

    qkv_nat, qkv_g1, qkv_g2 = _project(x, row(norm_mix[l]), w_in[l], gain_nat, gain_grp)
    qkv_nat = qkv_nat.reshape(NAT_HEADS, n, LANES)
    qkv_g1 = qkv_g1.reshape(GRP_HEADS, n, LANES)
    qkv_g2 = qkv_g2.reshape(GRP_HEADS, n, LANES)

    out_a = _attn_a(qkv_nat, _score_ctl(slopes_a, A_HEAD_DIM, a_q_norm[l], a_k_norm[l]),
                    row(a_lambda_q1[l]), row(a_lambda_k1[l]), row(a_lambda_q2[l]), row(a_lambda_k2[l]),
                    row(a_subln[l]), b, t)
    out_b = _attn_b(qkv_nat, qkv_g1, qkv_g2, _score_ctl(slopes_b, B_HEAD_DIM, b_q_norm[l], b_k_norm[l]), b, t)
    x2 = x.reshape(n, d)
    x2 = _merge(x2, out_a, out_b, qkv_nat, mem, row(mem_norm[l]), w_mem_kv[l], row(tiled(c_k_norm[l])),
                row(norm_mix[l]), w_gate[l], row(b_gate[l]), w_branch[l], w_out[l], t)
    x2 = _ffn(x2, row(norm_ffn[l]), w_ffn_gate[l], w_ffn_up[l], w_ffn_down[l])
    return x2.reshape(b, t, d)
```

```python
import functools
import math

import jax
import jax.numpy as jnp
from jax import lax
from jax.experimental import pallas as pl
from jax.experimental.pallas import tpu as pltpu

F32 = jnp.float32
BF16 = jnp.bfloat16

D_MODEL = 1024
A_HEADS = 4
A_HEAD_DIM = 64
A_V_DIM = 2 * A_HEAD_DIM
B_GROUPS = ((128, 1), (512, 4), (2048, 16))
B_HEADS = 4
B_HEAD_DIM = 128
C_HEADS = 4
C_HEAD_DIM = 128
BRANCH_WIDTH = 512
N_BRANCHES = 3
EPS = 1e-6
ALIBI_MAX_BIAS = 8.0
LAMBDA_INIT = 0.8 - 0.6 * math.exp(-0.3 * 0)

LANES = 128
VMEM_LIMIT_BYTES = 56 * 1024 * 1024

NAT_COLS = 7 * BRANCH_WIDTH
GRP_COLS = 3 * BRANCH_WIDTH
HEADS_PER_CHUNK = BRANCH_WIDTH // LANES
NAT_HEADS = NAT_COLS // LANES
GRP_HEADS = GRP_COLS // LANES
NAT_NORMS = ("half", "half", "none", "full", "full", "none", "full")
GRP_COL_OFFS = (3 * BRANCH_WIDTH, 6 * BRANCH_WIDTH, 9 * BRANCH_WIDTH)
NAT_COL_OFFS = (0, BRANCH_WIDTH, 2 * BRANCH_WIDTH) + GRP_COL_OFFS + (12 * BRANCH_WIDTH,)
GRP_NORMS = ("full", "full", "none")
NEG_BIG = -1e30
LOG2E = 1.4426950408889634
MAX_SAFE_SCORE_BOUND = 40.0


def _rms(x, gain):
    return x * lax.rsqrt(jnp.mean(x * x, axis=-1, keepdims=True) + EPS) * gain


def _dot(a, b):
    return jnp.dot(a, b, preferred_element_type=F32)


def _dot_nt(a, b):
    return lax.dot_general(a, b, (((1,), (1,)), ((), ())), preferred_element_type=F32)


def _const_spec(shape):
    nd = len(shape)
    return pl.BlockSpec(shape, lambda *_: (0,) * nd, pipeline_mode=pl.Buffered(1))


def _head_norm(y, gain, kind):
    if kind == "none":
        return y
    lo_mask = lax.broadcasted_iota(jnp.int32, (1, LANES), 1) < A_HEAD_DIM
    cols = []
    for c in range(0, y.shape[1], LANES):
        z = y[:, c:c + LANES]
        sq = z * z
        s_all = jnp.sum(sq, axis=-1, keepdims=True)
        if kind == "full":
            ms = s_all * (1.0 / LANES)
        else:
            s_lo = jnp.sum(jnp.where(lo_mask, sq, 0.0), axis=-1, keepdims=True)
            ms = jnp.where(lo_mask, s_lo, s_all - s_lo) * (1.0 / A_HEAD_DIM)
        cols.append(z * lax.rsqrt(ms + EPS))
    return jnp.concatenate(cols, axis=1) * gain


def _proj_kernel(x_ref, g_ref, w_ref, gn_ref, gg_ref, on_ref, o1_ref, o2_ref, slab_ref, hp_ref, *, tm):
    h = _rms(x_ref[0], g_ref[...])
    hb = h.astype(BF16)
    bw = BRANCH_WIDTH
    for ci, kind in enumerate(NAT_NORMS):
        c0 = NAT_COL_OFFS[ci]
        res = _head_norm(_dot(hb, w_ref[:, c0:c0 + bw]), gn_ref[ci:ci + 1, :], kind).astype(BF16)
        for hh in range(HEADS_PER_CHUNK):
            on_ref[ci * HEADS_PER_CHUNK + hh, 0] = res[:, hh * LANES:(hh + 1) * LANES]
    n_slabs = D_MODEL // LANES
    for s in range(n_slabs):
        slab_ref[s] = h[:, s * LANES:(s + 1) * LANES]
    for g, dil, o_ref in ((1, 4, o1_ref), (2, 16, o2_ref)):
        n = tm // dil
        for r in range(dil):
            for s in range(n_slabs):
                hp_ref[r * n:(r + 1) * n, s * LANES:(s + 1) * LANES] = (
                    slab_ref[s, pl.ds(r, n, stride=dil), :].astype(BF16))
        hp = hp_ref[...]
        for ci, kind in enumerate(GRP_NORMS):
            c0 = GRP_COL_OFFS[ci] + g * bw
            res = _head_norm(_dot(hp, w_ref[:, c0:c0 + bw]), gg_ref[ci:ci + 1, :], kind).astype(BF16)
            for hh in range(HEADS_PER_CHUNK):
                for r in range(dil):
                    o_ref[ci * HEADS_PER_CHUNK + hh, 0, r] = res[r * n:(r + 1) * n, hh * LANES:(hh + 1) * LANES]


def _project(x, norm_g, w_in, gain_nat, gain_grp, tm=512):
    b, t, d = x.shape
    grid = (b, t // tm)
    return pl.pallas_call(
        functools.partial(_proj_kernel, tm=tm),
        grid=grid,
        in_specs=[
            pl.BlockSpec((1, tm, d), lambda i, j: (i, j, 0)),
            _const_spec((1, d)),
            _const_spec(w_in.shape),
            _const_spec(gain_nat.shape),
            _const_spec(gain_grp.shape),
        ],
        out_specs=[
            pl.BlockSpec((NAT_HEADS, 1, tm, LANES), lambda i, j: (0, i, j, 0)),
            pl.BlockSpec((GRP_HEADS, 1, 4, tm // 4, LANES), lambda i, j: (0, i, 0, j, 0)),
            pl.BlockSpec((GRP_HEADS, 1, 16, tm // 16, LANES), lambda i, j: (0, i, 0, j, 0)),
        ],
        out_shape=[
            jax.ShapeDtypeStruct((NAT_HEADS, b, t, LANES), BF16),
            jax.ShapeDtypeStruct((GRP_HEADS, b, 4, t // 4, LANES), BF16),
            jax.ShapeDtypeStruct((GRP_HEADS, b, 16, t // 16, LANES), BF16),
        ],
        scratch_shapes=[
            pltpu.VMEM((d // LANES, tm, LANES), F32),
            pltpu.VMEM((tm, d), BF16),
        ],
        compiler_params=pltpu.CompilerParams(
            dimension_semantics=("arbitrary", "arbitrary"), vmem_limit_bytes=VMEM_LIMIT_BYTES),
        name="proj",
    )(x, norm_g, w_in, gain_nat, gain_grp)


def _attn_a_kernel(ctl_ref, q_ref, k_ref, v_ref, lq1_ref, lk1_ref, lq2_ref, lk2_ref,
                   sub_ref, o_ref, vt_ref, tab_ref, et_ref, ot_ref, l_ref, *, t, qb, kc, inflight):
    h = pl.program_id(0)
    nblk = t // qb
    slope2 = ctl_ref[h] * LOG2E
    shift = ctl_ref[A_HEADS]
    fast = ctl_ref[A_HEADS + 1] > 0.5
    lo_mask = lax.broadcasted_iota(jnp.int32, (1, LANES), 1) < A_HEAD_DIM

    lam = (jnp.exp(jnp.sum(lq1_ref[...] * lk1_ref[...], keepdims=True))
           - jnp.exp(jnp.sum(lq2_ref[...] * lk2_ref[...], keepdims=True)) + LAMBDA_INIT)

    vt_ref[...] = v_ref[...].T

    @pl.when(pl.program_id(1) == 0)
    def _():
        cc = lax.broadcasted_iota(jnp.int32, (2 * t - qb, qb), 0)
        il = lax.broadcasted_iota(jnp.int32, (2 * t - qb, qb), 1)
        tab_ref[...] = (-slope2 * jnp.abs(cc - (t - qb) - il).astype(F32)
                        - jnp.where(fast, shift, 0.0))

    def fold8(x, op):
        return op(x.reshape(x.shape[0] // 8, 8, x.shape[1]), axis=0)

    def block(j, slot, exact_max):
        q = q_ref[pl.ds(pl.multiple_of(j * qb, qb), qb), :]
        zero = jnp.zeros_like(q)
        q2 = jnp.concatenate([jnp.where(lo_mask, q, zero), jnp.where(lo_mask, zero, q)], axis=0)
        off = pl.multiple_of(t - qb - j * qb, qb)

        def scores(c):
            s = _dot_nt(k_ref[c * kc:(c + 1) * kc, :], q2)
            bias = tab_ref[pl.ds(off + c * kc, kc), :]
            return s + jnp.concatenate([bias, bias], axis=1)

        m = None
        if exact_max:
            for c in range(t // kc):
                cm = fold8(scores(c), jnp.max)
                m = cm if m is None else jnp.maximum(m, cm)
            m = jnp.max(m, axis=0, keepdims=True)
        acc = None
        for c in range(t // kc):
            s = scores(c)
            e = jnp.exp2(s - m if exact_max else s)
            et_ref[slot, c * kc:(c + 1) * kc, :] = e.astype(BF16)
            part = fold8(e, jnp.sum)
            acc = part if acc is None else acc + part
        l_ref[pl.ds(j, 1), :] = jnp.sum(acc, axis=0, keepdims=True)
        ot_ref[j] = _dot(vt_ref[...], et_ref[slot])

    def finish(j):
        ot = ot_ref[j]
        inv = 1.0 / l_ref[pl.ds(j, 1), :]
        o = ot[:, :qb] * inv[:, :qb] - ot[:, qb:] * (lam * inv[:, qb:])
        o = _rms(o.T, sub_ref[...]) * (1.0 - LAMBDA_INIT)
        o_ref[pl.ds(pl.multiple_of(j * qb, qb), qb), :] = o.astype(BF16)

    def run(exact_max):
        for s in range(inflight):
            block(s, s, exact_max)

        def step(i, carry):
            for s in range(inflight):
                finish(inflight * (i - 1) + s)
            for s in range(inflight):
                block(inflight * i + s, s, exact_max)
            return carry
        lax.fori_loop(1, nblk // inflight, step, 0)
        for s in range(inflight):
            finish(nblk - inflight + s)

    @pl.when(fast)
    def _():
        run(False)

    @pl.when(jnp.logical_not(fast))
    def _():
        run(True)


def _attn_a(qkv_nat, ctl, lq1, lk1, lq2, lk2, subln, b, t, qb=128, kc=512, inflight=8):
    n = b * t
    nblk = t // qb
    vec = lambda w: _const_spec((1, w))
    return pl.pallas_call(
        functools.partial(_attn_a_kernel, t=t, qb=qb, kc=kc, inflight=inflight),
        grid=(A_HEADS, b),
        in_specs=[
            pl.BlockSpec(memory_space=pltpu.SMEM),
            pl.BlockSpec((None, t, LANES), lambda h, i: (h, i, 0)),
            pl.BlockSpec((None, t, LANES), lambda h, i: (A_HEADS + h, i, 0)),
            pl.BlockSpec((None, t, LANES), lambda h, i: (2 * A_HEADS + h, i, 0)),
            vec(A_HEAD_DIM), vec(A_HEAD_DIM), vec(A_HEAD_DIM), vec(A_HEAD_DIM),
            vec(A_V_DIM),
        ],
        out_specs=pl.BlockSpec((None, t, LANES), lambda h, i: (h, i, 0)),
        out_shape=jax.ShapeDtypeStruct((A_HEADS, n, LANES), BF16),
        scratch_shapes=[
            pltpu.VMEM((LANES, t), BF16),
            pltpu.VMEM((2 * t - qb, qb), F32),
            pltpu.VMEM((inflight, t, 2 * qb), BF16),
            pltpu.VMEM((nblk, LANES, 2 * qb), F32),
            pltpu.VMEM((nblk, 2 * qb), F32),
        ],
        compiler_params=pltpu.CompilerParams(
            dimension_semantics=("arbitrary", "arbitrary"), vmem_limit_bytes=VMEM_LIMIT_BYTES),
        name="attn_a",
    )(ctl, qkv_nat, qkv_nat, qkv_nat, lq1, lk1, lq2, lk2, subln)


def _attn_b_kernel(ctl_ref, q0_ref, k0_ref, v0_ref, q1_ref, k1_ref, v1_ref, q2_ref, k2_ref, v2_ref,
                   o_ref, acc_ref, den_ref, kt_ref, *, t, qb):
    h = pl.program_id(1)
    n_ctl = len(B_GROUPS) * B_HEADS
    shift = ctl_ref[n_ctl]
    fast = ctl_ref[n_ctl + 1] > 0.5
    refs = ((q0_ref, k0_ref, v0_ref), (q1_ref, k1_ref, v1_ref), (q2_ref, k2_ref, v2_ref))

    def blocks(g):
        window, dil = B_GROUPS[g]
        n_side = window // (2 * dil)
        sub = t // dil
        wk = min(2 * qb, sub)
        for r in range(dil):
            for j in range(sub // qb):
                i0 = j * qb
                ws = min(max(i0 - n_side, 0), sub - wk)
                yield r, i0, r * sub + i0, r * sub + ws, wk, i0 - ws, n_side

    def bias_table(g, wk, offset, n_side, sub_shift):
        slope2 = ctl_ref[g * B_HEADS + h] * (float(B_GROUPS[g][1]) * LOG2E)
        ql = lax.broadcasted_iota(jnp.int32, (qb, wk), 0)
        kl = lax.broadcasted_iota(jnp.int32, (qb, wk), 1)
        dist = jnp.abs(kl - ql - offset)
        return jnp.where(dist <= n_side, -slope2 * dist.astype(F32) - sub_shift, NEG_BIG)

    def store_rows(ref, g, r, i0, val):
        dil = B_GROUPS[g][1]
        if dil == 1:
            ref[g, i0:i0 + qb, :] = val
        else:
            ref[g, pl.ds(r + dil * i0, qb, stride=dil), :] = val

    @pl.when(fast)
    def _():
        half = qb // 2
        for g in range(len(B_GROUPS)):
            q_ref, k_ref, v_ref = refs[g]
            kt_ref[0] = k_ref[...].T
            kt_ref[1] = jnp.concatenate([k_ref[half:, :], k_ref[:half, :]], axis=0).T
            tables = {}
            for r, i0, qrow, krow, wk, offset, n_side in blocks(g):
                if offset not in tables:
                    tables[offset] = bias_table(g, wk, offset, n_side, shift)
                if krow % qb == 0:
                    kwt = kt_ref[0, :, krow:krow + wk]
                else:
                    kwt = kt_ref[1, :, krow - half:krow - half + wk]
                s = _dot(q_ref[qrow:qrow + qb, :], kwt)
                e = jnp.exp2(s + tables[offset])
                den = jnp.sum(e, axis=-1, keepdims=True)
                store_rows(acc_ref, g, r, i0, _dot(e.astype(BF16), v_ref[krow:krow + wk, :]))
                store_rows(den_ref, g, r, i0, jnp.broadcast_to(den, (qb, LANES)))
        o_ref[...] = ((acc_ref[0] + acc_ref[1] + acc_ref[2])
                      / (den_ref[0] + den_ref[1] + den_ref[2])).astype(BF16)

    @pl.when(jnp.logical_not(fast))
    def _():
        for g in range(len(B_GROUPS)):
            q_ref, k_ref, v_ref = refs[g]
            tables = {}
            for r, i0, qrow, krow, wk, offset, n_side in blocks(g):
                if offset not in tables:
                    tables[offset] = bias_table(g, wk, offset, n_side, 0.0)
                s = _dot_nt(q_ref[qrow:qrow + qb, :], k_ref[krow:krow + wk, :]) + tables[offset]
                m = jnp.max(s, axis=-1, keepdims=True)
                e = jnp.exp2(s - m)
                l = jnp.sum(e, axis=-1, keepdims=True)
                o = _dot(e.astype(BF16), v_ref[krow:krow + wk, :]) * (1.0 / l)
                store_rows(acc_ref, g, r, i0, o)
                store_rows(den_ref, g, r, i0, jnp.broadcast_to(m + jnp.log2(l), (qb, LANES)))
        l0, l1, l2 = den_ref[0], den_ref[1], den_ref[2]
        m = jnp.maximum(jnp.maximum(l0, l1), l2)
        e0, e1, e2 = jnp.exp2(l0 - m), jnp.exp2(l1 - m), jnp.exp2(l2 - m)
        o_ref[...] = ((e0 * acc_ref[0] + e1 * acc_ref[1] + e2 * acc_ref[2]) / (e0 + e1 + e2)).astype(BF16)


def _attn_b(qkv_nat, qkv_g1, qkv_g2, ctl, b, t, qb=128):
    n = b * t
    blk = lambda off: pl.BlockSpec((None, t, LANES), lambda i, h: (off + h, i, 0))
    return pl.pallas_call(
        functools.partial(_attn_b_kernel, t=t, qb=qb),
        grid=(b, B_HEADS),
        in_specs=[
            pl.BlockSpec(memory_space=pltpu.SMEM),
            blk(12), blk(16), blk(20),
            blk(0), blk(4), blk(8),
            blk(0), blk(4), blk(8),
        ],
        out_specs=pl.BlockSpec((None, t, LANES), lambda i, h: (h, i, 0)),
        out_shape=jax.ShapeDtypeStruct((B_HEADS, n, LANES), BF16),
        scratch_shapes=[
            pltpu.VMEM((len(B_GROUPS), t, LANES), F32),
            pltpu.VMEM((len(B_GROUPS), t, LANES), F32),
            pltpu.VMEM((2, LANES, t), BF16),
        ],
        compiler_params=pltpu.CompilerParams(
            dimension_semantics=("arbitrary", "arbitrary"), vmem_limit_bytes=VMEM_LIMIT_BYTES),
        name="attn_b",
    )(ctl, qkv_nat, qkv_nat, qkv_nat, qkv_g1, qkv_g1, qkv_g1, qkv_g2, qkv_g2, qkv_g2)


def _merge_kernel(x_ref, a_ref, b_ref, cq_ref, mem_ref, gm_ref, wkv_ref, gk_ref, g_ref, wg_ref, bg_ref, wb_ref,
                  wo_ref, o_ref, ckv_ref, *, tiles_per_batch):
    half = C_HEADS * C_HEAD_DIM

    @pl.when(pl.program_id(0) % tiles_per_batch == 0)
    def _():
        mn = _rms(mem_ref[0], gm_ref[...]).astype(BF16)
        kv = _dot(mn, wkv_ref[...])
        ckv_ref[:, :half] = _head_norm(kv[:, :half], gk_ref[...], "full").astype(BF16)
        ckv_ref[:, half:] = kv[:, half:].astype(BF16)

    x = x_ref[...]
    hb = _rms(x, g_ref[...]).astype(BF16)
    d = x.shape[-1]

    def gated(g, branch):
        gate = jax.nn.sigmoid(_dot(hb, wg_ref[:, g * d:(g + 1) * d]) + bg_ref[:, g * d:(g + 1) * d])
        return gate * _dot(branch, wb_ref[g])

    wide = lambda ref: jnp.concatenate([ref[hh] for hh in range(HEADS_PER_CHUNK)], axis=1)
    acc = gated(0, wide(a_ref)) + gated(1, wide(b_ref))
    heads = []
    for hh in range(C_HEADS):
        cols = slice(hh * C_HEAD_DIM, (hh + 1) * C_HEAD_DIM)
        s = _dot_nt(cq_ref[hh], ckv_ref[:, cols])
        e = jnp.exp2(s - jnp.max(s, axis=-1, keepdims=True))
        inv = 1.0 / jnp.sum(e, axis=-1, keepdims=True)
        heads.append(_dot(e.astype(BF16), ckv_ref[:, half + hh * C_HEAD_DIM:half + (hh + 1) * C_HEAD_DIM]) * inv)
    acc = acc + gated(2, jnp.concatenate(heads, axis=1).astype(BF16))
    o_ref[...] = x + _dot(acc.astype(BF16), wo_ref[...])


def _merge(x2, out_a, out_b, qkv_nat, mem, gm, w_kv, gk4, norm_g, w_gate, b_gate, w_branch, w_out, t, tm=512):
    n, d = x2.shape
    n_mem, kv_cols = mem.shape[1], w_kv.shape[1]
    row = lambda w: pl.BlockSpec((tm, w), lambda i: (i, 0))
    heads = lambda chunk: pl.BlockSpec((HEADS_PER_CHUNK, tm, LANES), lambda i: (chunk, i, 0))
    return pl.pallas_call(
        functools.partial(_merge_kernel, tiles_per_batch=t // tm),
        grid=(n // tm,),
        in_specs=[
            row(d), heads(0), heads(0), heads(NAT_HEADS // HEADS_PER_CHUNK - 1),
            pl.BlockSpec((1, n_mem, d), lambda i: (i // (t // tm), 0, 0)),
            _const_spec((1, d)), _const_spec((d, kv_cols)), _const_spec((1, kv_cols // 2)),
            _const_spec((1, d)),
            _const_spec((d, N_BRANCHES * d)),
            _const_spec((1, N_BRANCHES * d)),
            _const_spec((N_BRANCHES, BRANCH_WIDTH, d)),
            _const_spec((d, d)),
        ],
        out_specs=row(d),
        out_shape=jax.ShapeDtypeStruct((n, d), F32),
        scratch_shapes=[pltpu.VMEM((n_mem, kv_cols), BF16)],
        compiler_params=pltpu.CompilerParams(
            dimension_semantics=("arbitrary",), vmem_limit_bytes=VMEM_LIMIT_BYTES),
        name="merge",
    )(x2, out_a, out_b, qkv_nat, mem, gm, w_kv, gk4, norm_g, w_gate, b_gate, w_branch, w_out)


def _ffn_kernel(x_ref, g_ref, wg_ref, wu_ref, wd_ref, o_ref, *, chunks):
    x = x_ref[...]
    hb = _rms(x, g_ref[...]).astype(BF16)
    acc = x
    for c0, c1 in chunks:
        gt = _dot(hb, wg_ref[:, c0:c1])
        up = _dot(hb, wu_ref[:, c0:c1])
        acc = acc + _dot((jax.nn.silu(gt) * up).astype(BF16), wd_ref[c0:c1, :])
    o_ref[...] = acc


def _ffn(x2, norm_g, w_gate, w_up, w_down, tm=512, fc=768):
    n, d = x2.shape
    d_ff = w_gate.shape[1]
    chunks = tuple((c, min(c + fc, d_ff)) for c in range(0, d_ff, fc))
    row = pl.BlockSpec((tm, d), lambda i: (i, 0))
    return pl.pallas_call(
        functools.partial(_ffn_kernel, chunks=chunks),
        grid=(n // tm,),
        in_specs=[row, _const_spec((1, d)), _const_spec((d, d_ff)), _const_spec((d, d_ff)),
                  _const_spec((d_ff, d))],
        out_specs=row,
        out_shape=jax.ShapeDtypeStruct((n, d), F32),
        compiler_params=pltpu.CompilerParams(
            dimension_semantics=("arbitrary",), vmem_limit_bytes=VMEM_LIMIT_BYTES),
        name="ffn",
    )(x2, norm_g, w_gate, w_up, w_down)


def _score_ctl(slopes, head_dim, gq, gk):
    bound = math.sqrt(head_dim) * jnp.max(jnp.abs(gq)) * jnp.max(jnp.abs(gk))
    return jnp.concatenate([slopes.reshape(-1),
                            jnp.stack([bound * LOG2E, (bound <= MAX_SAFE_SCORE_BOUND).astype(F32)])])


def kernel(x, mem, norm_mix, w_in, w_gate, b_gate, a_q_norm, a_k_norm, a_lambda_q1, a_lambda_k1, a_lambda_q2,
           a_lambda_k2, a_subln, b_q_norm, b_k_norm, mem_norm, w_mem_kv, c_q_norm, c_k_norm, w_branch, w_out,
           norm_ffn, w_ffn_gate, w_ffn_up, w_ffn_down):
    b, t, d = x.shape
    n = b * t
    n_groups = len(B_GROUPS)
    slopes_a = jnp.exp2(-ALIBI_MAX_BIAS * jnp.arange(1, A_HEADS + 1, dtype=F32) / A_HEADS)
    nb = n_groups * B_HEADS
    slopes_b = jnp.exp2(-ALIBI_MAX_BIAS * jnp.arange(1, nb + 1, dtype=F32) / nb)

    l = 0
    bw = BRANCH_WIDTH
    row = lambda v: v.reshape(1, -1)
    tiled = lambda v: jnp.tile(v, bw // v.shape[0])
    ones = jnp.ones((bw,), F32)
    a_qs = A_HEAD_DIM ** -0.5 * LOG2E
    b_qs = B_HEAD_DIM ** -0.5 * LOG2E
    c_qs = C_HEAD_DIM ** -0.5 * LOG2E
    gain_nat = jnp.stack([tiled(a_q_norm[l]) * a_qs, tiled(a_k_norm[l]), ones, tiled(b_q_norm[l]) * b_qs,
                          tiled(b_k_norm[l]), ones, tiled(c_q_norm[l]) * c_qs])
    gain_grp = jnp.stack([tiled(b_q_norm[l]) * b_qs, tiled(b_k_norm[l]), ones])

    qkv_nat, qkv_g1, qkv_g2 = _project(x, row(norm_mix[l]), w_in[l], gain_nat, gain_grp)
    qkv_nat = qkv_nat.reshape(NAT_HEADS, n, LANES)
    qkv_g1 = qkv_g1.reshape(GRP_HEADS, n, LANES)
    qkv_g2 = qkv_g2.reshape(GRP_HEADS, n, LANES)

    out_a = _attn_a(qkv_nat, _score_ctl(slopes_a, A_HEAD_DIM, a_q_norm[l], a_k_norm[l]),
                    row(a_lambda_q1[l]), row(a_lambda_k1[l]), row(a_lambda_q2[l]), row(a_lambda_k2[l]),
                    row(a_subln[l]), b, t)
    out_b = _attn_b(qkv_nat, qkv_g1, qkv_g2, _score_ctl(slopes_b, B_HEAD_DIM, b_q_norm[l], b_k_norm[l]), b, t)
    x2 = x.reshape(n, d)
    x2 = _merge(x2, out_a, out_b, qkv_nat, mem, row(mem_norm[l]), w_mem_kv[l], row(tiled(c_k_norm[l])),
                row(norm_mix[l]), w_gate[l], row(b_gate[l]), w_branch[l], w_out[l], t)
    x2 = _ffn(x2, row(norm_ffn[l]), w_ffn_gate[l], w_ffn_up[l], w_ffn_down[l])
    return x2.reshape(b, t, d)
```

```python
import functools
import math

import jax
import jax.numpy as jnp
from jax import lax
from jax.experimental import pallas as pl
from jax.experimental.pallas import tpu as pltpu

F32 = jnp.float32
BF16 = jnp.bfloat16

D_MODEL = 1024
A_HEADS = 4
A_HEAD_DIM = 64
A_V_DIM = 2 * A_HEAD_DIM
B_GROUPS = ((128, 1), (512, 4), (2048, 16))
B_HEADS = 4
B_HEAD_DIM = 128
C_HEADS = 4
C_HEAD_DIM = 128
BRANCH_WIDTH = 512
N_BRANCHES = 3
EPS = 1e-6
ALIBI_MAX_BIAS = 8.0
LAMBDA_INIT = 0.8 - 0.6 * math.exp(-0.3 * 0)

LANES = 128
VMEM_LIMIT_BYTES = 56 * 1024 * 1024

NAT_COLS = 7 * BRANCH_WIDTH
GRP_COLS = 3 * BRANCH_WIDTH
HEADS_PER_CHUNK = BRANCH_WIDTH // LANES
NAT_HEADS = NAT_COLS // LANES
GRP_HEADS = GRP_COLS // LANES
NAT_NORMS = ("half", "half", "none", "full", "full", "none", "full")
GRP_COL_OFFS = (3 * BRANCH_WIDTH, 6 * BRANCH_WIDTH, 9 * BRANCH_WIDTH)
NAT_COL_OFFS = (0, BRANCH_WIDTH, 2 * BRANCH_WIDTH) + GRP_COL_OFFS + (12 * BRANCH_WIDTH,)
GRP_NORMS = ("full", "full", "none")
NEG_BIG = -1e30
LOG2E = 1.4426950408889634
MAX_SAFE_SCORE_BOUND = 40.0


def _rms(x, gain):
    return x * lax.rsqrt(jnp.mean(x * x, axis=-1, keepdims=True) + EPS) * gain


def _dot(a, b):
    return jnp.dot(a, b, preferred_element_type=F32)


def _dot_nt(a, b):
    return lax.dot_general(a, b, (((1,), (1,)), ((), ())), preferred_element_type=F32)


def _const_spec(shape):
    nd = len(shape)
    return pl.BlockSpec(shape, lambda *_: (0,) * nd, pipeline_mode=pl.Buffered(1))


_HBM_SPEC = pl.BlockSpec(memory_space=pl.ANY)
STAGE_ROWS = 128


def _stage_weight(w_hbm, w_vmem, stage_ref, sem_ref):
    rows = stage_ref.shape[1]
    n_chunks = w_hbm.shape[0] // rows

    def chunk_copy(c):
        return pltpu.make_async_copy(w_hbm.at[pl.ds(c * rows, rows), :], stage_ref.at[c % 2], sem_ref.at[c % 2])

    chunk_copy(0).start()
    for c in range(n_chunks):
        if c + 1 < n_chunks:
            chunk_copy(c + 1).start()
        chunk_copy(c).wait()
        w_vmem[c * rows:(c + 1) * rows, :] = stage_ref[c % 2].astype(BF16)


def _head_norm(y, gain, kind):
    if kind == "none":
        return y
    lo_mask = lax.broadcasted_iota(jnp.int32, (1, LANES), 1) < A_HEAD_DIM
    cols = []
    for c in range(0, y.shape[1], LANES):
        z = y[:, c:c + LANES]
        sq = z * z
        s_all = jnp.sum(sq, axis=-1, keepdims=True)
        if kind == "full":
            ms = s_all * (1.0 / LANES)
        else:
            s_lo = jnp.sum(jnp.where(lo_mask, sq, 0.0), axis=-1, keepdims=True)
            ms = jnp.where(lo_mask, s_lo, s_all - s_lo) * (1.0 / A_HEAD_DIM)
        cols.append(z * lax.rsqrt(ms + EPS))
    return jnp.concatenate(cols, axis=1) * gain


def _proj_kernel(x_ref, g_ref, w_hbm, gn_ref, gg_ref, on_ref, o1_ref, o2_ref, slab_ref, hp_ref, w_ref, stage, sem,
                 *, tm):
    @pl.when((pl.program_id(0) == 0) & (pl.program_id(1) == 0))
    def _():
        _stage_weight(w_hbm, w_ref, stage, sem)

    h = _rms(x_ref[0], g_ref[...])
    hb = h.astype(BF16)
    bw = BRANCH_WIDTH
    for ci, kind in enumerate(NAT_NORMS):
        c0 = NAT_COL_OFFS[ci]
        res = _head_norm(_dot(hb, w_ref[:, c0:c0 + bw]), gn_ref[ci:ci + 1, :], kind).astype(BF16)
        for hh in range(HEADS_PER_CHUNK):
            on_ref[ci * HEADS_PER_CHUNK + hh, 0] = res[:, hh * LANES:(hh + 1) * LANES]
    n_slabs = D_MODEL // LANES
    for s in range(n_slabs):
        slab_ref[s] = h[:, s * LANES:(s + 1) * LANES]
    for g, dil, o_ref in ((1, 4, o1_ref), (2, 16, o2_ref)):
        n = tm // dil
        for r in range(dil):
            for s in range(n_slabs):
                hp_ref[r * n:(r + 1) * n, s * LANES:(s + 1) * LANES] = (
                    slab_ref[s, pl.ds(r, n, stride=dil), :].astype(BF16))
        hp = hp_ref[...]
        for ci, kind in enumerate(GRP_NORMS):
            c0 = GRP_COL_OFFS[ci] + g * bw
            res = _head_norm(_dot(hp, w_ref[:, c0:c0 + bw]), gg_ref[ci:ci + 1, :], kind).astype(BF16)
            for hh in range(HEADS_PER_CHUNK):
                for r in range(dil):
                    o_ref[ci * HEADS_PER_CHUNK + hh, 0, r] = res[r * n:(r + 1) * n, hh * LANES:(hh + 1) * LANES]


def _project(x, norm_g, w_in, gain_nat, gain_grp, tm=512):
    b, t, d = x.shape
    grid = (b, t // tm)
    return pl.pallas_call(
        functools.partial(_proj_kernel, tm=tm),
        grid=grid,
        in_specs=[
            pl.BlockSpec((1, tm, d), lambda i, j: (i, j, 0)),
            _const_spec((1, d)),
            _HBM_SPEC,
            _const_spec(gain_nat.shape),
            _const_spec(gain_grp.shape),
        ],
        out_specs=[
            pl.BlockSpec((NAT_HEADS, 1, tm, LANES), lambda i, j: (0, i, j, 0)),
            pl.BlockSpec((GRP_HEADS, 1, 4, tm // 4, LANES), lambda i, j: (0, i, 0, j, 0)),
            pl.BlockSpec((GRP_HEADS, 1, 16, tm // 16, LANES), lambda i, j: (0, i, 0, j, 0)),
        ],
        out_shape=[
            jax.ShapeDtypeStruct((NAT_HEADS, b, t, LANES), BF16),
            jax.ShapeDtypeStruct((GRP_HEADS, b, 4, t // 4, LANES), BF16),
            jax.ShapeDtypeStruct((GRP_HEADS, b, 16, t // 16, LANES), BF16),
        ],
        scratch_shapes=[
            pltpu.VMEM((d // LANES, tm, LANES), F32),
            pltpu.VMEM((tm, d), BF16),
            pltpu.VMEM(w_in.shape, BF16),
            pltpu.VMEM((2, STAGE_ROWS // 2, w_in.shape[1]), F32),
            pltpu.SemaphoreType.DMA((2,)),
        ],
        compiler_params=pltpu.CompilerParams(
            dimension_semantics=("arbitrary", "arbitrary"), vmem_limit_bytes=VMEM_LIMIT_BYTES),
        name="proj",
    )(x, norm_g, w_in, gain_nat, gain_grp)


def _attn_a_kernel(ctl_ref, q_ref, k_ref, v_ref, lq1_ref, lk1_ref, lq2_ref, lk2_ref,
                   sub_ref, o_ref, vt_ref, tab_ref, et_ref, ot_ref, l_ref, *, t, qb, kc, inflight):
    h = pl.program_id(0)
    nblk = t // qb
    slope2 = ctl_ref[h] * LOG2E
    shift = ctl_ref[A_HEADS]
    fast = ctl_ref[A_HEADS + 1] > 0.5
    lo_mask = lax.broadcasted_iota(jnp.int32, (1, LANES), 1) < A_HEAD_DIM

    lam = (jnp.exp(jnp.sum(lq1_ref[...] * lk1_ref[...], keepdims=True))
           - jnp.exp(jnp.sum(lq2_ref[...] * lk2_ref[...], keepdims=True)) + LAMBDA_INIT)

    vt_ref[...] = v_ref[...].T

    @pl.when(pl.program_id(1) == 0)
    def _():
        cc = lax.broadcasted_iota(jnp.int32, (2 * t - qb, qb), 0)
        il = lax.broadcasted_iota(jnp.int32, (2 * t - qb, qb), 1)
        tab_ref[...] = (-slope2 * jnp.abs(cc - (t - qb) - il).astype(F32)
                        - jnp.where(fast, shift, 0.0))

    def fold8(x, op):
        return op(x.reshape(x.shape[0] // 8, 8, x.shape[1]), axis=0)

    def block(j, slot, exact_max):
        q = q_ref[pl.ds(pl.multiple_of(j * qb, qb), qb), :]
        zero = jnp.zeros_like(q)
        q2 = jnp.concatenate([jnp.where(lo_mask, q, zero), jnp.where(lo_mask, zero, q)], axis=0)
        off = pl.multiple_of(t - qb - j * qb, qb)

        def scores(c):
            s = _dot_nt(k_ref[c * kc:(c + 1) * kc, :], q2)
            bias = tab_ref[pl.ds(off + c * kc, kc), :]
            return s + jnp.concatenate([bias, bias], axis=1)

        m = None
        if exact_max:
            for c in range(t // kc):
                cm = fold8(scores(c), jnp.max)
                m = cm if m is None else jnp.maximum(m, cm)
            m = jnp.max(m, axis=0, keepdims=True)
        acc = None
        for c in range(t // kc):
            s = scores(c)
            e = jnp.exp2(s - m if exact_max else s)
            et_ref[slot, c * kc:(c + 1) * kc, :] = e.astype(BF16)
            part = fold8(e, jnp.sum)
            acc = part if acc is None else acc + part
        l_ref[pl.ds(j, 1), :] = jnp.sum(acc, axis=0, keepdims=True)
        ot_ref[j] = _dot(vt_ref[...], et_ref[slot])

    def finish(j):
        ot = ot_ref[j]
        inv = 1.0 / l_ref[pl.ds(j, 1), :]
        o = ot[:, :qb] * inv[:, :qb] - ot[:, qb:] * (lam * inv[:, qb:])
        o = _rms(o.T, sub_ref[...]) * (1.0 - LAMBDA_INIT)
        o_ref[pl.ds(pl.multiple_of(j * qb, qb), qb), :] = o.astype(BF16)

    def run(exact_max):
        for s in range(inflight):
            block(s, s, exact_max)

        def step(i, carry):
            for s in range(inflight):
                finish(inflight * (i - 1) + s)
            for s in range(inflight):
                block(inflight * i + s, s, exact_max)
            return carry
        lax.fori_loop(1, nblk // inflight, step, 0)
        for s in range(inflight):
            finish(nblk - inflight + s)

    @pl.when(fast)
    def _():
        run(False)

    @pl.when(jnp.logical_not(fast))
    def _():
        run(True)


def _attn_a(qkv_nat, ctl, lq1, lk1, lq2, lk2, subln, b, t, qb=128, kc=512, inflight=8):
    n = b * t
    nblk = t // qb
    vec = lambda w: _const_spec((1, w))
    return pl.pallas_call(
        functools.partial(_attn_a_kernel, t=t, qb=qb, kc=kc, inflight=inflight),
        grid=(A_HEADS, b),
        in_specs=[
            pl.BlockSpec(memory_space=pltpu.SMEM),
            pl.BlockSpec((None, t, LANES), lambda h, i: (h, i, 0)),
            pl.BlockSpec((None, t, LANES), lambda h, i: (A_HEADS + h, i, 0)),
            pl.BlockSpec((None, t, LANES), lambda h, i: (2 * A_HEADS + h, i, 0)),
            vec(A_HEAD_DIM), vec(A_HEAD_DIM), vec(A_HEAD_DIM), vec(A_HEAD_DIM),
            vec(A_V_DIM),
        ],
        out_specs=pl.BlockSpec((None, t, LANES), lambda h, i: (h, i, 0)),
        out_shape=jax.ShapeDtypeStruct((A_HEADS, n, LANES), BF16),
        scratch_shapes=[
            pltpu.VMEM((LANES, t), BF16),
            pltpu.VMEM((2 * t - qb, qb), F32),
            pltpu.VMEM((inflight, t, 2 * qb), BF16),
            pltpu.VMEM((nblk, LANES, 2 * qb), F32),
            pltpu.VMEM((nblk, 2 * qb), F32),
        ],
        compiler_params=pltpu.CompilerParams(
            dimension_semantics=("arbitrary", "arbitrary"), vmem_limit_bytes=VMEM_LIMIT_BYTES),
        name="attn_a",
    )(ctl, qkv_nat, qkv_nat, qkv_nat, lq1, lk1, lq2, lk2, subln)


def _attn_b_kernel(ctl_ref, q0_ref, k0_ref, v0_ref, q1_ref, k1_ref, v1_ref, q2_ref, k2_ref, v2_ref,
                   o_ref, acc_ref, den_ref, kt_ref, *, t, qb):
    h = pl.program_id(1)
    n_ctl = len(B_GROUPS) * B_HEADS
    shift = ctl_ref[n_ctl]
    fast = ctl_ref[n_ctl + 1] > 0.5
    refs = ((q0_ref, k0_ref, v0_ref), (q1_ref, k1_ref, v1_ref), (q2_ref, k2_ref, v2_ref))

    def blocks(g):
        window, dil = B_GROUPS[g]
        n_side = window // (2 * dil)
        sub = t // dil
        wk = min(2 * qb, sub)
        for r in range(dil):
            for j in range(sub // qb):
                i0 = j * qb
                ws = min(max(i0 - n_side, 0), sub - wk)
                yield r, i0, r * sub + i0, r * sub + ws, wk, i0 - ws, n_side

    def bias_table(g, wk, offset, n_side, sub_shift):
        slope2 = ctl_ref[g * B_HEADS + h] * (float(B_GROUPS[g][1]) * LOG2E)
        ql = lax.broadcasted_iota(jnp.int32, (qb, wk), 0)
        kl = lax.broadcasted_iota(jnp.int32, (qb, wk), 1)
        dist = jnp.abs(kl - ql - offset)
        return jnp.where(dist <= n_side, -slope2 * dist.astype(F32) - sub_shift, NEG_BIG)

    def store_rows(ref, g, r, i0, val):
        dil = B_GROUPS[g][1]
        if dil == 1:
            ref[g, i0:i0 + qb, :] = val
        else:
            ref[g, pl.ds(r + dil * i0, qb, stride=dil), :] = val

    @pl.when(fast)
    def _():
        half = qb // 2
        for g in range(len(B_GROUPS)):
            q_ref, k_ref, v_ref = refs[g]
            kt_ref[0] = k_ref[...].T
            kt_ref[1] = jnp.concatenate([k_ref[half:, :], k_ref[:half, :]], axis=0).T
            tables = {}
            for r, i0, qrow, krow, wk, offset, n_side in blocks(g):
                if offset not in tables:
                    tables[offset] = bias_table(g, wk, offset, n_side, shift)
                if krow % qb == 0:
                    kwt = kt_ref[0, :, krow:krow + wk]
                else:
                    kwt = kt_ref[1, :, krow - half:krow - half + wk]
                s = _dot(q_ref[qrow:qrow + qb, :], kwt)
                e = jnp.exp2(s + tables[offset])
                den = jnp.sum(e, axis=-1, keepdims=True)
                store_rows(acc_ref, g, r, i0, _dot(e.astype(BF16), v_ref[krow:krow + wk, :]))
                store_rows(den_ref, g, r, i0, jnp.broadcast_to(den, (qb, LANES)))
        o_ref[...] = ((acc_ref[0] + acc_ref[1] + acc_ref[2])
                      / (den_ref[0] + den_ref[1] + den_ref[2])).astype(BF16)

    @pl.when(jnp.logical_not(fast))
    def _():
        for g in range(len(B_GROUPS)):
            q_ref, k_ref, v_ref = refs[g]
            tables = {}
            for r, i0, qrow, krow, wk, offset, n_side in blocks(g):
                if offset not in tables:
                    tables[offset] = bias_table(g, wk, offset, n_side, 0.0)
                s = _dot_nt(q_ref[qrow:qrow + qb, :], k_ref[krow:krow + wk, :]) + tables[offset]
                m = jnp.max(s, axis=-1, keepdims=True)
                e = jnp.exp2(s - m)
                l = jnp.sum(e, axis=-1, keepdims=True)
                o = _dot(e.astype(BF16), v_ref[krow:krow + wk, :]) * (1.0 / l)
                store_rows(acc_ref, g, r, i0, o)
                store_rows(den_ref, g, r, i0, jnp.broadcast_to(m + jnp.log2(l), (qb, LANES)))
        l0, l1, l2 = den_ref[0], den_ref[1], den_ref[2]
        m = jnp.maximum(jnp.maximum(l0, l1), l2)
        e0, e1, e2 = jnp.exp2(l0 - m), jnp.exp2(l1 - m), jnp.exp2(l2 - m)
        o_ref[...] = ((e0 * acc_ref[0] + e1 * acc_ref[1] + e2 * acc_ref[2]) / (e0 + e1 + e2)).astype(BF16)


def _attn_b(qkv_nat, qkv_g1, qkv_g2, ctl, b, t, qb=128):
    n = b * t
    blk = lambda off: pl.BlockSpec((None, t, LANES), lambda i, h: (off + h, i, 0))
    return pl.pallas_call(
        functools.partial(_attn_b_kernel, t=t, qb=qb),
        grid=(b, B_HEADS),
        in_specs=[
            pl.BlockSpec(memory_space=pltpu.SMEM),
            blk(12), blk(16), blk(20),
            blk(0), blk(4), blk(8),
            blk(0), blk(4), blk(8),
        ],
        out_specs=pl.BlockSpec((None, t, LANES), lambda i, h: (h, i, 0)),
        out_shape=jax.ShapeDtypeStruct((B_HEADS, n, LANES), BF16),
        scratch_shapes=[
            pltpu.VMEM((len(B_GROUPS), t, LANES), F32),
            pltpu.VMEM((len(B_GROUPS), t, LANES), F32),
            pltpu.VMEM((2, LANES, t), BF16),
        ],
        compiler_params=pltpu.CompilerParams(
            dimension_semantics=("arbitrary", "arbitrary"), vmem_limit_bytes=VMEM_LIMIT_BYTES),
        name="attn_b",
    )(ctl, qkv_nat, qkv_nat, qkv_nat, qkv_g1, qkv_g1, qkv_g1, qkv_g2, qkv_g2, qkv_g2)


def _merge_kernel(x_ref, a_ref, b_ref, cq_ref, mem_ref, gm_ref, wkv_hbm, gk_ref, g_ref, wg_hbm, bg_ref, wb_hbm,
                  wo_hbm, o_ref, ckv_ref, wkv_ref, wg_ref, wb_ref, wo_ref, stage_g, stage_d, sem,
                  *, tiles_per_batch):
    half = C_HEADS * C_HEAD_DIM

    @pl.when(pl.program_id(0) == 0)
    def _():
        _stage_weight(wkv_hbm, wkv_ref, stage_d, sem)
        _stage_weight(wg_hbm, wg_ref, stage_g, sem)
        _stage_weight(wb_hbm, wb_ref, stage_d, sem)
        _stage_weight(wo_hbm, wo_ref, stage_d, sem)

    @pl.when(pl.program_id(0) % tiles_per_batch == 0)
    def _():
        mn = _rms(mem_ref[0], gm_ref[...]).astype(BF16)
        kv = _dot(mn, wkv_ref[...])
        ckv_ref[:, :half] = _head_norm(kv[:, :half], gk_ref[...], "full").astype(BF16)
        ckv_ref[:, half:] = kv[:, half:].astype(BF16)

    x = x_ref[...]
    hb = _rms(x, g_ref[...]).astype(BF16)
    d = x.shape[-1]

    def gated(g, branch):
        gate = jax.nn.sigmoid(_dot(hb, wg_ref[:, g * d:(g + 1) * d]) + bg_ref[:, g * d:(g + 1) * d])
        return gate * _dot(branch, wb_ref[g * BRANCH_WIDTH:(g + 1) * BRANCH_WIDTH, :])

    wide = lambda ref: jnp.concatenate([ref[hh] for hh in range(HEADS_PER_CHUNK)], axis=1)
    acc = gated(0, wide(a_ref)) + gated(1, wide(b_ref))
    heads = []
    for hh in range(C_HEADS):
        cols = slice(hh * C_HEAD_DIM, (hh + 1) * C_HEAD_DIM)
        s = _dot_nt(cq_ref[hh], ckv_ref[:, cols])
        e = jnp.exp2(s - jnp.max(s, axis=-1, keepdims=True))
        inv = 1.0 / jnp.sum(e, axis=-1, keepdims=True)
        heads.append(_dot(e.astype(BF16), ckv_ref[:, half + hh * C_HEAD_DIM:half + (hh + 1) * C_HEAD_DIM]) * inv)
    acc = acc + gated(2, jnp.concatenate(heads, axis=1).astype(BF16))
    o_ref[...] = x + _dot(acc.astype(BF16), wo_ref[...])


def _merge(x2, out_a, out_b, qkv_nat, mem, gm, w_kv, gk4, norm_g, w_gate, b_gate, w_branch, w_out, t, tm=512):
    n, d = x2.shape
    n_mem, kv_cols = mem.shape[1], w_kv.shape[1]
    row = lambda w: pl.BlockSpec((tm, w), lambda i: (i, 0))
    heads = lambda chunk: pl.BlockSpec((HEADS_PER_CHUNK, tm, LANES), lambda i: (chunk, i, 0))
    return pl.pallas_call(
        functools.partial(_merge_kernel, tiles_per_batch=t // tm),
        grid=(n // tm,),
        in_specs=[
            row(d), heads(0), heads(0), heads(NAT_HEADS // HEADS_PER_CHUNK - 1),
            pl.BlockSpec((1, n_mem, d), lambda i: (i // (t // tm), 0, 0)),
            _const_spec((1, d)), _HBM_SPEC, _const_spec((1, kv_cols // 2)),
            _const_spec((1, d)),
            _HBM_SPEC,
            _const_spec((1, N_BRANCHES * d)),
            _HBM_SPEC,
            _HBM_SPEC,
        ],
        out_specs=row(d),
        out_shape=jax.ShapeDtypeStruct((n, d), F32),
        scratch_shapes=[
            pltpu.VMEM((n_mem, kv_cols), BF16),
            pltpu.VMEM((d, kv_cols), BF16), pltpu.VMEM((d, N_BRANCHES * d), BF16),
            pltpu.VMEM((N_BRANCHES * BRANCH_WIDTH, d), BF16), pltpu.VMEM((d, d), BF16),
            pltpu.VMEM((2, STAGE_ROWS, N_BRANCHES * d), F32), pltpu.VMEM((2, STAGE_ROWS, d), F32),
            pltpu.SemaphoreType.DMA((2,)),
        ],
        compiler_params=pltpu.CompilerParams(
            dimension_semantics=("arbitrary",), vmem_limit_bytes=VMEM_LIMIT_BYTES),
        name="merge",
    )(x2, out_a, out_b, qkv_nat, mem, gm, w_kv, gk4, norm_g, w_gate, b_gate, w_branch, w_out)


def _ffn_kernel(x_ref, g_ref, wg_hbm, wu_hbm, wd_hbm, o_ref, wg_ref, wu_ref, wd_ref, stage_in, stage_out, sem,
                *, chunks):
    @pl.when(pl.program_id(0) == 0)
    def _():
        _stage_weight(wg_hbm, wg_ref, stage_in, sem)
        _stage_weight(wu_hbm, wu_ref, stage_in, sem)
        _stage_weight(wd_hbm, wd_ref, stage_out, sem)

    x = x_ref[...]
    hb = _rms(x, g_ref[...]).astype(BF16)
    acc = x
    for c0, c1 in chunks:
        gt = _dot(hb, wg_ref[:, c0:c1])
        up = _dot(hb, wu_ref[:, c0:c1])
        acc = acc + _dot((jax.nn.silu(gt) * up).astype(BF16), wd_ref[c0:c1, :])
    o_ref[...] = acc


def _ffn(x2, norm_g, w_gate, w_up, w_down, tm=512, fc=768):
    n, d = x2.shape
    d_ff = w_gate.shape[1]
    chunks = tuple((c, min(c + fc, d_ff)) for c in range(0, d_ff, fc))
    row = pl.BlockSpec((tm, d), lambda i: (i, 0))
    return pl.pallas_call(
        functools.partial(_ffn_kernel, chunks=chunks),
        grid=(n // tm,),
        in_specs=[row, _const_spec((1, d)), _HBM_SPEC, _HBM_SPEC, _HBM_SPEC],
        out_specs=row,
        out_shape=jax.ShapeDtypeStruct((n, d), F32),
        scratch_shapes=[
            pltpu.VMEM((d, d_ff), BF16), pltpu.VMEM((d, d_ff), BF16), pltpu.VMEM((d_ff, d), BF16),
            pltpu.VMEM((2, STAGE_ROWS, d_ff), F32), pltpu.VMEM((2, STAGE_ROWS, d), F32),
            pltpu.SemaphoreType.DMA((2,)),
        ],
        compiler_params=pltpu.CompilerParams(
            dimension_semantics=("arbitrary",), vmem_limit_bytes=VMEM_LIMIT_BYTES),
        name="ffn",
    )(x2, norm_g, w_gate, w_up, w_down)


def _score_ctl(slopes, head_dim, gq, gk):
    bound = math.sqrt(head_dim) * jnp.max(jnp.abs(gq)) * jnp.max(jnp.abs(gk))
    return jnp.concatenate([slopes.reshape(-1),
                            jnp.stack([bound * LOG2E, (bound <= MAX_SAFE_SCORE_BOUND).astype(F32)])])


def kernel(x, mem, norm_mix, w_in, w_gate, b_gate, a_q_norm, a_k_norm, a_lambda_q1, a_lambda_k1, a_lambda_q2,
           a_lambda_k2, a_subln, b_q_norm, b_k_norm, mem_norm, w_mem_kv, c_q_norm, c_k_norm, w_branch, w_out,
           norm_ffn, w_ffn_gate, w_ffn_up, w_ffn_down):
    b, t, d = x.shape
    n = b * t
    n_groups = len(B_GROUPS)
    slopes_a = jnp.exp2(-ALIBI_MAX_BIAS * jnp.arange(1, A_HEADS + 1, dtype=F32) / A_HEADS)
    nb = n_groups * B_HEADS
    slopes_b = jnp.exp2(-ALIBI_MAX_BIAS * jnp.arange(1, nb + 1, dtype=F32) / nb)

    l = 0
    bw = BRANCH_WIDTH
    row = lambda v: v.reshape(1, -1)
    tiled = lambda v: jnp.tile(v, bw // v.shape[0])
    ones = jnp.ones((bw,), F32)
    a_qs = A_HEAD_DIM ** -0.5 * LOG2E
    b_qs = B_HEAD_DIM ** -0.5 * LOG2E
    c_qs = C_HEAD_DIM ** -0.5 * LOG2E
    gain_nat = jnp.stack([tiled(a_q_norm[l]) * a_qs, tiled(a_k_norm[l]), ones, tiled(b_q_norm[l]) * b_qs,
                          tiled(b_k_norm[l]), ones, tiled(c_q_norm[l]) * c_qs])
    gain_grp = jnp.stack([tiled(b_q_norm[l]) * b_qs, tiled(b_k_norm[l]), ones])

    qkv_nat, qkv_g1, qkv_g2 = _project(x, row(norm_mix[l]), w_in[l], gain_nat, gain_grp)
    qkv_nat = qkv_nat.reshape(NAT_HEADS, n, LANES)
    qkv_g1 = qkv_g1.reshape(GRP_HEADS, n, LANES)
    qkv_g2 = qkv_g2.reshape(GRP_HEADS, n, LANES)

    out_a = _attn_a(qkv_nat, _score_ctl(slopes_a, A_HEAD_DIM, a_q_norm[l], a_k_norm[l]),
                    row(a_lambda_q1[l]), row(a_lambda_k1[l]), row(a_lambda_q2[l]), row(a_lambda_k2[l]),
                    row(a_subln[l]), b, t)
    out_b = _attn_b(qkv_nat, qkv_g1, qkv_g2, _score_ctl(slopes_b, B_HEAD_DIM, b_q_norm[l], b_k_norm[l]), b, t)
    x2 = x.reshape(n, d)
    x2 = _merge(x2, out_a, out_b, qkv_nat, mem, row(mem_norm[l]), w_mem_kv[l], row(tiled(c_k_norm[l])),
                row(norm_mix[l]), w_gate[l], row(b_gate[l]), w_branch[l].reshape(N_BRANCHES * bw, d), w_out[l], t)
    x2 = _ffn(x2, row(norm_ffn[l]), w_ffn_gate[l], w_ffn_up[l], w_ffn_down[l])
    return x2.reshape(b, t, d)
```

```python
import functools
import math

import jax
import jax.numpy as jnp
from jax import lax
from jax.experimental import pallas as pl
from jax.experimental.pallas import tpu as pltpu

F32 = jnp.float32
BF16 = jnp.bfloat16

D_MODEL = 1024
A_HEADS = 4
A_HEAD_DIM = 64
A_V_DIM = 2 * A_HEAD_DIM
B_GROUPS = ((128, 1), (512, 4), (2048, 16))
B_HEADS = 4
B_HEAD_DIM = 128
C_HEADS = 4
C_HEAD_DIM = 128
BRANCH_WIDTH = 512
N_BRANCHES = 3
EPS = 1e-6
ALIBI_MAX_BIAS = 8.0
LAMBDA_INIT = 0.8 - 0.6 * math.exp(-0.3 * 0)

LANES = 128
VMEM_LIMIT_BYTES = 56 * 1024 * 1024

NAT_COLS = 7 * BRANCH_WIDTH
GRP_COLS = 3 * BRANCH_WIDTH
HEADS_PER_CHUNK = BRANCH_WIDTH // LANES
NAT_HEADS = NAT_COLS // LANES
GRP_HEADS = GRP_COLS // LANES
NAT_NORMS = ("half", "half", "none", "full", "full", "none", "full")
GRP_COL_OFFS = (3 * BRANCH_WIDTH, 6 * BRANCH_WIDTH, 9 * BRANCH_WIDTH)
NAT_COL_OFFS = (0, BRANCH_WIDTH, 2 * BRANCH_WIDTH) + GRP_COL_OFFS + (12 * BRANCH_WIDTH,)
GRP_NORMS = ("full", "full", "none")
NEG_BIG = -1e30
LOG2E = 1.4426950408889634
MAX_SAFE_SCORE_BOUND = 40.0


def _rms(x, gain):
    return x * lax.rsqrt(jnp.mean(x * x, axis=-1, keepdims=True) + EPS) * gain


def _dot(a, b):
    return jnp.dot(a, b, preferred_element_type=F32)


def _dot_nt(a, b):
    return lax.dot_general(a, b, (((1,), (1,)), ((), ())), preferred_element_type=F32)


def _const_spec(shape):
    nd = len(shape)
    return pl.BlockSpec(shape, lambda *_: (0,) * nd, pipeline_mode=pl.Buffered(1))


def _head_norm(y, gain, kind):
    if kind == "none":
        return y
    lo_mask = lax.broadcasted_iota(jnp.int32, (1, LANES), 1) < A_HEAD_DIM
    cols = []
    for c in range(0, y.shape[1], LANES):
        z = y[:, c:c + LANES]
        sq = z * z
        s_all = jnp.sum(sq, axis=-1, keepdims=True)
        if kind == "full":
            ms = s_all * (1.0 / LANES)
        else:
            s_lo = jnp.sum(jnp.where(lo_mask, sq, 0.0), axis=-1, keepdims=True)
            ms = jnp.where(lo_mask, s_lo, s_all - s_lo) * (1.0 / A_HEAD_DIM)
        cols.append(z * lax.rsqrt(ms + EPS))
    return jnp.concatenate(cols, axis=1) * gain


def _proj_kernel(x_ref, g_ref, w_ref, gn_ref, gg_ref, on_ref, o1_ref, o2_ref, slab_ref, hp_ref, *, tm):
    h = _rms(x_ref[0], g_ref[...])
    hb = h.astype(BF16)
    bw = BRANCH_WIDTH
    for ci, kind in enumerate(NAT_NORMS):
        c0 = NAT_COL_OFFS[ci]
        res = _head_norm(_dot(hb, w_ref[:, c0:c0 + bw]), gn_ref[ci:ci + 1, :], kind).astype(BF16)
        for hh in range(HEADS_PER_CHUNK):
            on_ref[ci * HEADS_PER_CHUNK + hh, 0] = res[:, hh * LANES:(hh + 1) * LANES]
    n_slabs = D_MODEL // LANES
    for s in range(n_slabs):
        slab_ref[s] = h[:, s * LANES:(s + 1) * LANES]
    for g, dil, o_ref in ((1, 4, o1_ref), (2, 16, o2_ref)):
        n = tm // dil
        for r in range(dil):
            for s in range(n_slabs):
                hp_ref[r * n:(r + 1) * n, s * LANES:(s + 1) * LANES] = (
                    slab_ref[s, pl.ds(r, n, stride=dil), :].astype(BF16))
        hp = hp_ref[...]
        for ci, kind in enumerate(GRP_NORMS):
            c0 = GRP_COL_OFFS[ci] + g * bw
            res = _head_norm(_dot(hp, w_ref[:, c0:c0 + bw]), gg_ref[ci:ci + 1, :], kind).astype(BF16)
            for hh in range(HEADS_PER_CHUNK):
                for r in range(dil):
                    o_ref[ci * HEADS_PER_CHUNK + hh, 0, r] = res[r * n:(r + 1) * n, hh * LANES:(hh + 1) * LANES]


def _project(x, norm_g, w_in, gain_nat, gain_grp, tm=512):
    b, t, d = x.shape
    grid = (b, t // tm)
    return pl.pallas_call(
        functools.partial(_proj_kernel, tm=tm),
        grid=grid,
        in_specs=[
            pl.BlockSpec((1, tm, d), lambda i, j: (i, j, 0)),
            _const_spec((1, d)),
            _const_spec(w_in.shape),
            _const_spec(gain_nat.shape),
            _const_spec(gain_grp.shape),
        ],
        out_specs=[
            pl.BlockSpec((NAT_HEADS, 1, tm, LANES), lambda i, j: (0, i, j, 0)),
            pl.BlockSpec((GRP_HEADS, 1, 4, tm // 4, LANES), lambda i, j: (0, i, 0, j, 0)),
            pl.BlockSpec((GRP_HEADS, 1, 16, tm // 16, LANES), lambda i, j: (0, i, 0, j, 0)),
        ],
        out_shape=[
            jax.ShapeDtypeStruct((NAT_HEADS, b, t, LANES), BF16),
            jax.ShapeDtypeStruct((GRP_HEADS, b, 4, t // 4, LANES), BF16),
            jax.ShapeDtypeStruct((GRP_HEADS, b, 16, t // 16, LANES), BF16),
        ],
        scratch_shapes=[
            pltpu.VMEM((d // LANES, tm, LANES), F32),
            pltpu.VMEM((tm, d), BF16),
        ],
        compiler_params=pltpu.CompilerParams(
            dimension_semantics=("arbitrary", "arbitrary"), vmem_limit_bytes=VMEM_LIMIT_BYTES),
        name="proj",
    )(x, norm_g, w_in, gain_nat, gain_grp)


def _attn_a_kernel(ctl_ref, q_ref, k_ref, v_ref, lq1_ref, lk1_ref, lq2_ref, lk2_ref,
                   sub_ref, o_ref, vt_ref, tab_ref, et_ref, ot_ref, l_ref, *, t, qb, kc, inflight):
    h = pl.program_id(0)
    nblk = t // qb
    slope2 = ctl_ref[h] * LOG2E
    shift = ctl_ref[A_HEADS]
    fast = ctl_ref[A_HEADS + 1] > 0.5
    lo_mask = lax.broadcasted_iota(jnp.int32, (1, LANES), 1) < A_HEAD_DIM

    lam = (jnp.exp(jnp.sum(lq1_ref[...] * lk1_ref[...], keepdims=True))
           - jnp.exp(jnp.sum(lq2_ref[...] * lk2_ref[...], keepdims=True)) + LAMBDA_INIT)

    vt_ref[...] = v_ref[...].T

    @pl.when(pl.program_id(1) == 0)
    def _():
        cc = lax.broadcasted_iota(jnp.int32, (2 * t - qb, qb), 0)
        il = lax.broadcasted_iota(jnp.int32, (2 * t - qb, qb), 1)
        tab_ref[...] = (-slope2 * jnp.abs(cc - (t - qb) - il).astype(F32)
                        - jnp.where(fast, shift, 0.0))

    def fold8(x, op):
        return op(x.reshape(x.shape[0] // 8, 8, x.shape[1]), axis=0)

    def block(j, slot, exact_max):
        q = q_ref[pl.ds(pl.multiple_of(j * qb, qb), qb), :]
        zero = jnp.zeros_like(q)
        q2 = jnp.concatenate([jnp.where(lo_mask, q, zero), jnp.where(lo_mask, zero, q)], axis=0)
        off = pl.multiple_of(t - qb - j * qb, qb)

        def scores(c):
            s = _dot_nt(k_ref[c * kc:(c + 1) * kc, :], q2)
            bias = tab_ref[pl.ds(off + c * kc, kc), :]
            return s + jnp.concatenate([bias, bias], axis=1)

        m = None
        if exact_max:
            for c in range(t // kc):
                cm = fold8(scores(c), jnp.max)
                m = cm if m is None else jnp.maximum(m, cm)
            m = jnp.max(m, axis=0, keepdims=True)
        acc = None
        for c in range(t // kc):
            s = scores(c)
            e = jnp.exp2(s - m if exact_max else s)
            et_ref[slot, c * kc:(c + 1) * kc, :] = e.astype(BF16)
            part = fold8(e, jnp.sum)
            acc = part if acc is None else acc + part
        l_ref[pl.ds(j, 1), :] = jnp.sum(acc, axis=0, keepdims=True)
        ot_ref[j] = _dot(vt_ref[...], et_ref[slot])

    def finish(j):
        ot = ot_ref[j]
        inv = 1.0 / l_ref[pl.ds(j, 1), :]
        o = ot[:, :qb] * inv[:, :qb] - ot[:, qb:] * (lam * inv[:, qb:])
        o = _rms(o.T, sub_ref[...]) * (1.0 - LAMBDA_INIT)
        o_ref[pl.ds(pl.multiple_of(j * qb, qb), qb), :] = o.astype(BF16)

    def run(exact_max):
        for s in range(inflight):
            block(s, s, exact_max)

        def step(i, carry):
            for s in range(inflight):
                finish(inflight * (i - 1) + s)
            for s in range(inflight):
                block(inflight * i + s, s, exact_max)
            return carry
        lax.fori_loop(1, nblk // inflight, step, 0)
        for s in range(inflight):
            finish(nblk - inflight + s)

    @pl.when(fast)
    def _():
        run(False)

    @pl.when(jnp.logical_not(fast))
    def _():
        run(True)


def _attn_b_kernel(ctl_ref, q0_ref, k0_ref, v0_ref, q1_ref, k1_ref, v1_ref, q2_ref, k2_ref, v2_ref,
                   o_ref, acc_ref, den_ref, kt_ref, *, t, qb):
    h = pl.program_id(0)
    n_ctl = len(B_GROUPS) * B_HEADS
    shift = ctl_ref[n_ctl]
    fast = ctl_ref[n_ctl + 1] > 0.5
    refs = ((q0_ref, k0_ref, v0_ref), (q1_ref, k1_ref, v1_ref), (q2_ref, k2_ref, v2_ref))

    def blocks(g):
        window, dil = B_GROUPS[g]
        n_side = window // (2 * dil)
        sub = t // dil
        wk = min(2 * qb, sub)
        for r in range(dil):
            for j in range(sub // qb):
                i0 = j * qb
                ws = min(max(i0 - n_side, 0), sub - wk)
                yield r, i0, r * sub + i0, r * sub + ws, wk, i0 - ws, n_side

    def bias_table(g, wk, offset, n_side, sub_shift):
        slope2 = ctl_ref[g * B_HEADS + h] * (float(B_GROUPS[g][1]) * LOG2E)
        ql = lax.broadcasted_iota(jnp.int32, (qb, wk), 0)
        kl = lax.broadcasted_iota(jnp.int32, (qb, wk), 1)
        dist = jnp.abs(kl - ql - offset)
        return jnp.where(dist <= n_side, -slope2 * dist.astype(F32) - sub_shift, NEG_BIG)

    def store_rows(ref, g, r, i0, val):
        dil = B_GROUPS[g][1]
        if dil == 1:
            ref[g, i0:i0 + qb, :] = val
        else:
            ref[g, pl.ds(r + dil * i0, qb, stride=dil), :] = val

    @pl.when(fast)
    def _():
        half = qb // 2
        for g in range(len(B_GROUPS)):
            q_ref, k_ref, v_ref = refs[g]
            kt_ref[0] = k_ref[...].T
            kt_ref[1] = jnp.concatenate([k_ref[half:, :], k_ref[:half, :]], axis=0).T
            tables = {}
            for r, i0, qrow, krow, wk, offset, n_side in blocks(g):
                if offset not in tables:
                    tables[offset] = bias_table(g, wk, offset, n_side, shift)
                if krow % qb == 0:
                    kwt = kt_ref[0, :, krow:krow + wk]
                else:
                    kwt = kt_ref[1, :, krow - half:krow - half + wk]
                s = _dot(q_ref[qrow:qrow + qb, :], kwt)
                e = jnp.exp2(s + tables[offset])
                den = jnp.sum(e, axis=-1, keepdims=True)
                store_rows(acc_ref, g, r, i0, _dot(e.astype(BF16), v_ref[krow:krow + wk, :]))
                store_rows(den_ref, g, r, i0, jnp.broadcast_to(den, (qb, LANES)))
        o_ref[...] = ((acc_ref[0] + acc_ref[1] + acc_ref[2])
                      / (den_ref[0] + den_ref[1] + den_ref[2])).astype(BF16)

    @pl.when(jnp.logical_not(fast))
    def _():
        for g in range(len(B_GROUPS)):
            q_ref, k_ref, v_ref = refs[g]
            tables = {}
            for r, i0, qrow, krow, wk, offset, n_side in blocks(g):
                if offset not in tables:
                    tables[offset] = bias_table(g, wk, offset, n_side, 0.0)
                s = _dot_nt(q_ref[qrow:qrow + qb, :], k_ref[krow:krow + wk, :]) + tables[offset]
                m = jnp.max(s, axis=-1, keepdims=True)
                e = jnp.exp2(s - m)
                l = jnp.sum(e, axis=-1, keepdims=True)
                o = _dot(e.astype(BF16), v_ref[krow:krow + wk, :]) * (1.0 / l)
                store_rows(acc_ref, g, r, i0, o)
                store_rows(den_ref, g, r, i0, jnp.broadcast_to(m + jnp.log2(l), (qb, LANES)))
        l0, l1, l2 = den_ref[0], den_ref[1], den_ref[2]
        m = jnp.maximum(jnp.maximum(l0, l1), l2)
        e0, e1, e2 = jnp.exp2(l0 - m), jnp.exp2(l1 - m), jnp.exp2(l2 - m)
        o_ref[...] = ((e0 * acc_ref[0] + e1 * acc_ref[1] + e2 * acc_ref[2]) / (e0 + e1 + e2)).astype(BF16)


def _attn_ab_kernel(ctl_a_ref, ctl_b_ref, qa_ref, ka_ref, va_ref, q0_ref, k0_ref, v0_ref, q1_ref, k1_ref, v1_ref,
                    q2_ref, k2_ref, v2_ref, lq1_ref, lk1_ref, lq2_ref, lk2_ref, sub_ref, oa_ref, ob_ref,
                    vt_ref, tab_ref, et_ref, ot_ref, l_ref, acc_ref, den_ref, kt_ref, *, t, qb, kc, inflight):
    _attn_a_kernel(ctl_a_ref, qa_ref, ka_ref, va_ref, lq1_ref, lk1_ref, lq2_ref, lk2_ref, sub_ref, oa_ref,
                   vt_ref, tab_ref, et_ref, ot_ref, l_ref, t=t, qb=qb, kc=kc, inflight=inflight)
    _attn_b_kernel(ctl_b_ref, q0_ref, k0_ref, v0_ref, q1_ref, k1_ref, v1_ref, q2_ref, k2_ref, v2_ref, ob_ref,
                   acc_ref, den_ref, kt_ref, t=t, qb=qb)


def _attn_ab(qkv_nat, qkv_g1, qkv_g2, ctl_a, ctl_b, lq1, lk1, lq2, lk2, subln, b, t, qb=128, kc=512, inflight=8):
    assert A_HEADS == B_HEADS
    n = b * t
    nblk = t // qb
    vec = lambda w: _const_spec((1, w))
    blk = lambda off: pl.BlockSpec((None, t, LANES), lambda h, i: (off + h, i, 0))
    smem = pl.BlockSpec(memory_space=pltpu.SMEM)
    out = pl.BlockSpec((None, t, LANES), lambda h, i: (h, i, 0))
    return pl.pallas_call(
        functools.partial(_attn_ab_kernel, t=t, qb=qb, kc=kc, inflight=inflight),
        grid=(A_HEADS, b),
        in_specs=[
            smem, smem,
            blk(0), blk(A_HEADS), blk(2 * A_HEADS),
            blk(12), blk(16), blk(20),
            blk(0), blk(4), blk(8),
            blk(0), blk(4), blk(8),
            vec(A_HEAD_DIM), vec(A_HEAD_DIM), vec(A_HEAD_DIM), vec(A_HEAD_DIM),
            vec(A_V_DIM),
        ],
        out_specs=[out, out],
        out_shape=[jax.ShapeDtypeStruct((A_HEADS, n, LANES), BF16), jax.ShapeDtypeStruct((B_HEADS, n, LANES), BF16)],
        scratch_shapes=[
            pltpu.VMEM((LANES, t), BF16),
            pltpu.VMEM((2 * t - qb, qb), F32),
            pltpu.VMEM((inflight, t, 2 * qb), BF16),
            pltpu.VMEM((nblk, LANES, 2 * qb), F32),
            pltpu.VMEM((nblk, 2 * qb), F32),
            pltpu.VMEM((len(B_GROUPS), t, LANES), F32),
            pltpu.VMEM((len(B_GROUPS), t, LANES), F32),
            pltpu.VMEM((2, LANES, t), BF16),
        ],
        compiler_params=pltpu.CompilerParams(
            dimension_semantics=("arbitrary", "arbitrary"), vmem_limit_bytes=VMEM_LIMIT_BYTES),
        name="attn_ab",
    )(ctl_a, ctl_b, qkv_nat, qkv_nat, qkv_nat, qkv_nat, qkv_nat, qkv_nat, qkv_g1, qkv_g1, qkv_g1,
      qkv_g2, qkv_g2, qkv_g2, lq1, lk1, lq2, lk2, subln)


def _merge_kernel(x_ref, a_ref, b_ref, cq_ref, mem_ref, gm_ref, wkv_ref, gk_ref, g_ref, wg_ref, bg_ref, wb_ref,
                  wo_ref, o_ref, ckv_ref, *, tiles_per_batch):
    half = C_HEADS * C_HEAD_DIM

    @pl.when(pl.program_id(0) % tiles_per_batch == 0)
    def _():
        mn = _rms(mem_ref[0], gm_ref[...]).astype(BF16)
        kv = _dot(mn, wkv_ref[...])
        ckv_ref[:, :half] = _head_norm(kv[:, :half], gk_ref[...], "full").astype(BF16)
        ckv_ref[:, half:] = kv[:, half:].astype(BF16)

    x = x_ref[...]
    hb = _rms(x, g_ref[...]).astype(BF16)
    d = x.shape[-1]

    def gated(g, branch):
        gate = jax.nn.sigmoid(_dot(hb, wg_ref[:, g * d:(g + 1) * d]) + bg_ref[:, g * d:(g + 1) * d])
        return gate * _dot(branch, wb_ref[g])

    wide = lambda ref: jnp.concatenate([ref[hh] for hh in range(HEADS_PER_CHUNK)], axis=1)
    acc = gated(0, wide(a_ref)) + gated(1, wide(b_ref))
    heads = []
    for hh in range(C_HEADS):
        cols = slice(hh * C_HEAD_DIM, (hh + 1) * C_HEAD_DIM)
        s = _dot_nt(cq_ref[hh], ckv_ref[:, cols])
        e = jnp.exp2(s - jnp.max(s, axis=-1, keepdims=True))
        inv = 1.0 / jnp.sum(e, axis=-1, keepdims=True)
        heads.append(_dot(e.astype(BF16), ckv_ref[:, half + hh * C_HEAD_DIM:half + (hh + 1) * C_HEAD_DIM]) * inv)
    acc = acc + gated(2, jnp.concatenate(heads, axis=1).astype(BF16))
    o_ref[...] = x + _dot(acc.astype(BF16), wo_ref[...])


def _merge(x2, out_a, out_b, qkv_nat, mem, gm, w_kv, gk4, norm_g, w_gate, b_gate, w_branch, w_out, t, tm=512):
    n, d = x2.shape
    n_mem, kv_cols = mem.shape[1], w_kv.shape[1]
    row = lambda w: pl.BlockSpec((tm, w), lambda i: (i, 0))
    heads = lambda chunk: pl.BlockSpec((HEADS_PER_CHUNK, tm, LANES), lambda i: (chunk, i, 0))
    return pl.pallas_call(
        functools.partial(_merge_kernel, tiles_per_batch=t // tm),
        grid=(n // tm,),
        in_specs=[
            row(d), heads(0), heads(0), heads(NAT_HEADS // HEADS_PER_CHUNK - 1),
            pl.BlockSpec((1, n_mem, d), lambda i: (i // (t // tm), 0, 0)),
            _const_spec((1, d)), _const_spec((d, kv_cols)), _const_spec((1, kv_cols // 2)),
            _const_spec((1, d)),
            _const_spec((d, N_BRANCHES * d)),
            _const_spec((1, N_BRANCHES * d)),
            _const_spec((N_BRANCHES, BRANCH_WIDTH, d)),
            _const_spec((d, d)),
        ],
        out_specs=row(d),
        out_shape=jax.ShapeDtypeStruct((n, d), F32),
        scratch_shapes=[pltpu.VMEM((n_mem, kv_cols), BF16)],
        compiler_params=pltpu.CompilerParams(
            dimension_semantics=("arbitrary",), vmem_limit_bytes=VMEM_LIMIT_BYTES),
        name="merge",
    )(x2, out_a, out_b, qkv_nat, mem, gm, w_kv, gk4, norm_g, w_gate, b_gate, w_branch, w_out)


def _ffn_kernel(x_ref, g_ref, wg_ref, wu_ref, wd_ref, o_ref, *, chunks):
    x = x_ref[...]
    hb = _rms(x, g_ref[...]).astype(BF16)
    acc = x
    for c0, c1 in chunks:
        gt = _dot(hb, wg_ref[:, c0:c1])
        up = _dot(hb, wu_ref[:, c0:c1])
        acc = acc + _dot((jax.nn.silu(gt) * up).astype(BF16), wd_ref[c0:c1, :])
    o_ref[...] = acc


def _ffn(x2, norm_g, w_gate, w_up, w_down, tm=512, fc=768):
    n, d = x2.shape
    d_ff = w_gate.shape[1]
    chunks = tuple((c, min(c + fc, d_ff)) for c in range(0, d_ff, fc))
    row = pl.BlockSpec((tm, d), lambda i: (i, 0))
    return pl.pallas_call(
        functools.partial(_ffn_kernel, chunks=chunks),
        grid=(n // tm,),
        in_specs=[row, _const_spec((1, d)), _const_spec((d, d_ff)), _const_spec((d, d_ff)),
                  _const_spec((d_ff, d))],
        out_specs=row,
        out_shape=jax.ShapeDtypeStruct((n, d), F32),
        compiler_params=pltpu.CompilerParams(
            dimension_semantics=("arbitrary",), vmem_limit_bytes=VMEM_LIMIT_BYTES),
        name="ffn",
    )(x2, norm_g, w_gate, w_up, w_down)


def _score_ctl(slopes, head_dim, gq, gk):
    bound = math.sqrt(head_dim) * jnp.max(jnp.abs(gq)) * jnp.max(jnp.abs(gk))
    return jnp.concatenate([slopes.reshape(-1),
                            jnp.stack([bound * LOG2E, (bound <= MAX_SAFE_SCORE_BOUND).astype(F32)])])


def kernel(x, mem, norm_mix, w_in, w_gate, b_gate, a_q_norm, a_k_norm, a_lambda_q1, a_lambda_k1, a_lambda_q2,
           a_lambda_k2, a_subln, b_q_norm, b_k_norm, mem_norm, w_mem_kv, c_q_norm, c_k_norm, w_branch, w_out,
           norm_ffn, w_ffn_gate, w_ffn_up, w_ffn_down):
    b, t, d = x.shape
    n = b * t
    n_groups = len(B_GROUPS)
    slopes_a = jnp.exp2(-ALIBI_MAX_BIAS * jnp.arange(1, A_HEADS + 1, dtype=F32) / A_HEADS)
    nb = n_groups * B_HEADS
    slopes_b = jnp.exp2(-ALIBI_MAX_BIAS * jnp.arange(1, nb + 1, dtype=F32) / nb)

    l = 0
    bw = BRANCH_WIDTH
    row = lambda v: v.reshape(1, -1)
    tiled = lambda v: jnp.tile(v, bw // v.shape[0])
    ones = jnp.ones((bw,), F32)
    a_qs = A_HEAD_DIM ** -0.5 * LOG2E
    b_qs = B_HEAD_DIM ** -0.5 * LOG2E
    c_qs = C_HEAD_DIM ** -0.5 * LOG2E
    gain_nat = jnp.stack([tiled(a_q_norm[l]) * a_qs, tiled(a_k_norm[l]), ones, tiled(b_q_norm[l]) * b_qs,
                          tiled(b_k_norm[l]), ones, tiled(c_q_norm[l]) * c_qs])
    gain_grp = jnp.stack([tiled(b_q_norm[l]) * b_qs, tiled(b_k_norm[l]), ones])

    qkv_nat, qkv_g1, qkv_g2 = _project(x, row(norm_mix[l]), w_in[l], gain_nat, gain_grp)
    qkv_nat = qkv_nat.reshape(NAT_HEADS, n, LANES)
    qkv_g1 = qkv_g1.reshape(GRP_HEADS, n, LANES)
    qkv_g2 = qkv_g2.reshape(GRP_HEADS, n, LANES)

    out_a, out_b = _attn_ab(qkv_nat, qkv_g1, qkv_g2,
                            _score_ctl(slopes_a, A_HEAD_DIM, a_q_norm[l], a_k_norm[l]),
                            _score_ctl(slopes_b, B_HEAD_DIM, b_q_norm[l], b_k_norm[l]),
                            row(a_lambda_q1[l]), row(a_lambda_k1[l]), row(a_lambda_q2[l]), row(a_lambda_k2[l]),
                            row(a_subln[l]), b, t)
    x2 = x.reshape(n, d)
    x2 = _merge(x2, out_a, out_b, qkv_nat, mem, row(mem_norm[l]), w_mem_kv[l], row(tiled(c_k_norm[l])),
                row(norm_mix[l]), w_gate[l], row(b_gate[l]), w_branch[l], w_out[l], t)
    x2 = _ffn(x2, row(norm_ffn[l]), w_ffn_gate[l], w_ffn_up[l], w_ffn_down[l])
    return x2.reshape(b, t, d)
```

```python
import functools
import math

import jax
import jax.numpy as jnp
from jax import lax
from jax.experimental import pallas as pl
from jax.experimental.pallas import tpu as pltpu

F32 = jnp.float32
BF16 = jnp.bfloat16

D_MODEL = 1024
A_HEADS = 4
A_HEAD_DIM = 64
A_V_DIM = 2 * A_HEAD_DIM
B_GROUPS = ((128, 1), (512, 4), (2048, 16))
B_HEADS = 4
B_HEAD_DIM = 128
C_HEADS = 4
C_HEAD_DIM = 128
BRANCH_WIDTH = 512
N_BRANCHES = 3
EPS = 1e-6
ALIBI_MAX_BIAS = 8.0
LAMBDA_INIT = 0.8 - 0.6 * math.exp(-0.3 * 0)

LANES = 128
VMEM_LIMIT_BYTES = 56 * 1024 * 1024

NAT_COLS = 7 * BRANCH_WIDTH
GRP_COLS = 3 * BRANCH_WIDTH
HEADS_PER_CHUNK = BRANCH_WIDTH // LANES
NAT_HEADS = NAT_COLS // LANES
GRP_HEADS = GRP_COLS // LANES
NAT_NORMS = ("half", "half", "none", "full", "full", "none", "full")
GRP_COL_OFFS = (3 * BRANCH_WIDTH, 6 * BRANCH_WIDTH, 9 * BRANCH_WIDTH)
NAT_COL_OFFS = (0, BRANCH_WIDTH, 2 * BRANCH_WIDTH) + GRP_COL_OFFS + (12 * BRANCH_WIDTH,)
GRP_NORMS = ("full", "full", "none")
NEG_BIG = -1e30
LOG2E = 1.4426950408889634
MAX_SAFE_SCORE_BOUND = 40.0


def _rms(x, gain):
    return x * lax.rsqrt(jnp.mean(x * x, axis=-1, keepdims=True) + EPS) * gain


def _dot(a, b):
    return jnp.dot(a, b, preferred_element_type=F32)


def _dot_nt(a, b):
    return lax.dot_general(a, b, (((1,), (1,)), ((), ())), preferred_element_type=F32)


def _const_spec(shape):
    nd = len(shape)
    return pl.BlockSpec(shape, lambda *_: (0,) * nd, pipeline_mode=pl.Buffered(1))


def _head_norm(y, gain, kind):
    if kind == "none":
        return y
    lo_mask = lax.broadcasted_iota(jnp.int32, (1, LANES), 1) < A_HEAD_DIM
    cols = []
    for c in range(0, y.shape[1], LANES):
        z = y[:, c:c + LANES]
        sq = z * z
        s_all = jnp.sum(sq, axis=-1, keepdims=True)
        if kind == "full":
            ms = s_all * (1.0 / LANES)
        else:
            s_lo = jnp.sum(jnp.where(lo_mask, sq, 0.0), axis=-1, keepdims=True)
            ms = jnp.where(lo_mask, s_lo, s_all - s_lo) * (1.0 / A_HEAD_DIM)
        cols.append(z * lax.rsqrt(ms + EPS))
    return jnp.concatenate(cols, axis=1) * gain


def _proj_kernel(x_ref, g_ref, w_ref, gn_ref, gg_ref, on_ref, o1_ref, o2_ref, slab_ref, p4_ref, hp_ref, *, tm):
    h = _rms(x_ref[0], g_ref[...])
    hb = h.astype(BF16)
    bw = BRANCH_WIDTH
    for ci, kind in enumerate(NAT_NORMS):
        c0 = NAT_COL_OFFS[ci]
        res = _head_norm(_dot(hb, w_ref[:, c0:c0 + bw]), gn_ref[ci:ci + 1, :], kind).astype(BF16)
        for hh in range(HEADS_PER_CHUNK):
            on_ref[ci * HEADS_PER_CHUNK + hh, 0] = res[:, hh * LANES:(hh + 1) * LANES]
    n_slabs = D_MODEL // LANES
    for s in range(n_slabs):
        slab_ref[s] = h[:, s * LANES:(s + 1) * LANES]
    for g, dil, o_ref in ((1, 4, o1_ref), (2, 16, o2_ref)):
        n = tm // dil
        if dil == 4:
            for r in range(dil):
                for s in range(n_slabs):
                    rows = slab_ref[s, pl.ds(r, n, stride=dil), :]
                    p4_ref[s, r * n:(r + 1) * n, :] = rows
                    hp_ref[r * n:(r + 1) * n, s * LANES:(s + 1) * LANES] = rows.astype(BF16)
        else:
            n4 = tm // 4
            for r4 in range(4):
                for q in range(4):
                    r = r4 + 4 * q
                    for s in range(n_slabs):
                        hp_ref[r * n:(r + 1) * n, s * LANES:(s + 1) * LANES] = (
                            p4_ref[s, pl.ds(r4 * n4 + q, n, stride=4), :].astype(BF16))
        hp = hp_ref[...]
        for ci, kind in enumerate(GRP_NORMS):
            c0 = GRP_COL_OFFS[ci] + g * bw
            res = _head_norm(_dot(hp, w_ref[:, c0:c0 + bw]), gg_ref[ci:ci + 1, :], kind).astype(BF16)
            for hh in range(HEADS_PER_CHUNK):
                for r in range(dil):
                    o_ref[ci * HEADS_PER_CHUNK + hh, 0, r] = res[r * n:(r + 1) * n, hh * LANES:(hh + 1) * LANES]


def _project(x, norm_g, w_in, gain_nat, gain_grp, tm=512):
    b, t, d = x.shape
    grid = (b, t // tm)
    return pl.pallas_call(
        functools.partial(_proj_kernel, tm=tm),
        grid=grid,
        in_specs=[
            pl.BlockSpec((1, tm, d), lambda i, j: (i, j, 0)),
            _const_spec((1, d)),
            _const_spec(w_in.shape),
            _const_spec(gain_nat.shape),
            _const_spec(gain_grp.shape),
        ],
        out_specs=[
            pl.BlockSpec((NAT_HEADS, 1, tm, LANES), lambda i, j: (0, i, j, 0)),
            pl.BlockSpec((GRP_HEADS, 1, 4, tm // 4, LANES), lambda i, j: (0, i, 0, j, 0)),
            pl.BlockSpec((GRP_HEADS, 1, 16, tm // 16, LANES), lambda i, j: (0, i, 0, j, 0)),
        ],
        out_shape=[
            jax.ShapeDtypeStruct((NAT_HEADS, b, t, LANES), BF16),
            jax.ShapeDtypeStruct((GRP_HEADS, b, 4, t // 4, LANES), BF16),
            jax.ShapeDtypeStruct((GRP_HEADS, b, 16, t // 16, LANES), BF16),
        ],
        scratch_shapes=[
            pltpu.VMEM((d // LANES, tm, LANES), F32),
            pltpu.VMEM((d // LANES, tm, LANES), F32),
            pltpu.VMEM((tm, d), BF16),
        ],
        compiler_params=pltpu.CompilerParams(
            dimension_semantics=("arbitrary", "arbitrary"), vmem_limit_bytes=VMEM_LIMIT_BYTES),
        name="proj",
    )(x, norm_g, w_in, gain_nat, gain_grp)


def _attn_a_kernel(ctl_ref, q_ref, k_ref, v_ref, lq1_ref, lk1_ref, lq2_ref, lk2_ref,
                   sub_ref, o_ref, vt_ref, tab_ref, et_ref, ot_ref, l_ref, *, t, qb, kc, inflight):
    h = pl.program_id(0)
    nblk = t // qb
    slope2 = ctl_ref[h] * LOG2E
    shift = ctl_ref[A_HEADS]
    fast = ctl_ref[A_HEADS + 1] > 0.5
    lo_mask = lax.broadcasted_iota(jnp.int32, (1, LANES), 1) < A_HEAD_DIM

    lam = (jnp.exp(jnp.sum(lq1_ref[...] * lk1_ref[...], keepdims=True))
           - jnp.exp(jnp.sum(lq2_ref[...] * lk2_ref[...], keepdims=True)) + LAMBDA_INIT)

    vt_ref[...] = v_ref[...].T

    @pl.when(pl.program_id(1) == 0)
    def _():
        cc = lax.broadcasted_iota(jnp.int32, (2 * t - qb, qb), 0)
        il = lax.broadcasted_iota(jnp.int32, (2 * t - qb, qb), 1)
        tab_ref[...] = (-slope2 * jnp.abs(cc - (t - qb) - il).astype(F32)
                        - jnp.where(fast, shift, 0.0))

    def fold8(x, op):
        return op(x.reshape(x.shape[0] // 8, 8, x.shape[1]), axis=0)

    def block(j, slot, exact_max):
        q = q_ref[pl.ds(pl.multiple_of(j * qb, qb), qb), :]
        zero = jnp.zeros_like(q)
        q2 = jnp.concatenate([jnp.where(lo_mask, q, zero), jnp.where(lo_mask, zero, q)], axis=0)
        off = pl.multiple_of(t - qb - j * qb, qb)

        def scores(c):
            s = _dot_nt(k_ref[c * kc:(c + 1) * kc, :], q2)
            bias = tab_ref[pl.ds(off + c * kc, kc), :]
            return s + jnp.concatenate([bias, bias], axis=1)

        m = None
        if exact_max:
            for c in range(t // kc):
                cm = fold8(scores(c), jnp.max)
                m = cm if m is None else jnp.maximum(m, cm)
            m = jnp.max(m, axis=0, keepdims=True)
        acc = None
        for c in range(t // kc):
            s = scores(c)
            e = jnp.exp2(s - m if exact_max else s)
            et_ref[slot, c * kc:(c + 1) * kc, :] = e.astype(BF16)
            part = fold8(e, jnp.sum)
            acc = part if acc is None else acc + part
        l_ref[pl.ds(j, 1), :] = jnp.sum(acc, axis=0, keepdims=True)
        ot_ref[j] = _dot(vt_ref[...], et_ref[slot])

    def finish(j):
        ot = ot_ref[j]
        inv = 1.0 / l_ref[pl.ds(j, 1), :]
        o = ot[:, :qb] * inv[:, :qb] - ot[:, qb:] * (lam * inv[:, qb:])
        o = _rms(o.T, sub_ref[...]) * (1.0 - LAMBDA_INIT)
        o_ref[pl.ds(pl.multiple_of(j * qb, qb), qb), :] = o.astype(BF16)

    def run(exact_max):
        for s in range(inflight):
            block(s, s, exact_max)

        def step(i, carry):
            for s in range(inflight):
                finish(inflight * (i - 1) + s)
            for s in range(inflight):
                block(inflight * i + s, s, exact_max)
            return carry
        lax.fori_loop(1, nblk // inflight, step, 0)
        for s in range(inflight):
            finish(nblk - inflight + s)

    @pl.when(fast)
    def _():
        run(False)

    @pl.when(jnp.logical_not(fast))
    def _():
        run(True)


def _attn_b_kernel(ctl_ref, q0_ref, k0_ref, v0_ref, q1_ref, k1_ref, v1_ref, q2_ref, k2_ref, v2_ref,
                   o_ref, acc_ref, den_ref, kt_ref, *, t, qb):
    h = pl.program_id(0)
    n_ctl = len(B_GROUPS) * B_HEADS
    shift = ctl_ref[n_ctl]
    fast = ctl_ref[n_ctl + 1] > 0.5
    refs = ((q0_ref, k0_ref, v0_ref), (q1_ref, k1_ref, v1_ref), (q2_ref, k2_ref, v2_ref))

    def blocks(g):
        window, dil = B_GROUPS[g]
        n_side = window // (2 * dil)
        sub = t // dil
        wk = min(2 * qb, sub)
        for r in range(dil):
            for j in range(sub // qb):
                i0 = j * qb
                ws = min(max(i0 - n_side, 0), sub - wk)
                yield r, i0, r * sub + i0, r * sub + ws, wk, i0 - ws, n_side

    def bias_table(g, wk, offset, n_side, sub_shift):
        slope2 = ctl_ref[g * B_HEADS + h] * (float(B_GROUPS[g][1]) * LOG2E)
        ql = lax.broadcasted_iota(jnp.int32, (qb, wk), 0)
        kl = lax.broadcasted_iota(jnp.int32, (qb, wk), 1)
        dist = jnp.abs(kl - ql - offset)
        return jnp.where(dist <= n_side, -slope2 * dist.astype(F32) - sub_shift, NEG_BIG)

    def store_rows(ref, g, r, i0, val):
        dil = B_GROUPS[g][1]
        if dil == 1:
            ref[g, i0:i0 + qb, :] = val
        else:
            ref[g, pl.ds(r + dil * i0, qb, stride=dil), :] = val

    @pl.when(fast)
    def _():
        half = qb // 2
        for g in range(len(B_GROUPS)):
            q_ref, k_ref, v_ref = refs[g]
            kt_ref[0] = k_ref[...].T
            kt_ref[1] = jnp.concatenate([k_ref[half:, :], k_ref[:half, :]], axis=0).T
            tables = {}
            for r, i0, qrow, krow, wk, offset, n_side in blocks(g):
                if offset not in tables:
                    tables[offset] = bias_table(g, wk, offset, n_side, shift)
                if krow % qb == 0:
                    kwt = kt_ref[0, :, krow:krow + wk]
                else:
                    kwt = kt_ref[1, :, krow - half:krow - half + wk]
                s = _dot(q_ref[qrow:qrow + qb, :], kwt)
                e = jnp.exp2(s + tables[offset])
                den = jnp.sum(e, axis=-1, keepdims=True)
                store_rows(acc_ref, g, r, i0, _dot(e.astype(BF16), v_ref[krow:krow + wk, :]))
                store_rows(den_ref, g, r, i0, jnp.broadcast_to(den, (qb, LANES)))
        o_ref[...] = ((acc_ref[0] + acc_ref[1] + acc_ref[2])
                      / (den_ref[0] + den_ref[1] + den_ref[2])).astype(BF16)

    @pl.when(jnp.logical_not(fast))
    def _():
        for g in range(len(B_GROUPS)):
            q_ref, k_ref, v_ref = refs[g]
            tables = {}
            for r, i0, qrow, krow, wk, offset, n_side in blocks(g):
                if offset not in tables:
                    tables[offset] = bias_table(g, wk, offset, n_side, 0.0)
                s = _dot_nt(q_ref[qrow:qrow + qb, :], k_ref[krow:krow + wk, :]) + tables[offset]
                m = jnp.max(s, axis=-1, keepdims=True)
                e = jnp.exp2(s - m)
                l = jnp.sum(e, axis=-1, keepdims=True)
                o = _dot(e.astype(BF16), v_ref[krow:krow + wk, :]) * (1.0 / l)
                store_rows(acc_ref, g, r, i0, o)
                store_rows(den_ref, g, r, i0, jnp.broadcast_to(m + jnp.log2(l), (qb, LANES)))
        l0, l1, l2 = den_ref[0], den_ref[1], den_ref[2]
        m = jnp.maximum(jnp.maximum(l0, l1), l2)
        e0, e1, e2 = jnp.exp2(l0 - m), jnp.exp2(l1 - m), jnp.exp2(l2 - m)
        o_ref[...] = ((e0 * acc_ref[0] + e1 * acc_ref[1] + e2 * acc_ref[2]) / (e0 + e1 + e2)).astype(BF16)


def _attn_ab_kernel(ctl_a_ref, ctl_b_ref, qa_ref, ka_ref, va_ref, q0_ref, k0_ref, v0_ref, q1_ref, k1_ref, v1_ref,
                    q2_ref, k2_ref, v2_ref, lq1_ref, lk1_ref, lq2_ref, lk2_ref, sub_ref, oa_ref, ob_ref,
                    vt_ref, tab_ref, et_ref, ot_ref, l_ref, acc_ref, den_ref, kt_ref, *, t, qb, kc, inflight):
    _attn_a_kernel(ctl_a_ref, qa_ref, ka_ref, va_ref, lq1_ref, lk1_ref, lq2_ref, lk2_ref, sub_ref, oa_ref,
                   vt_ref, tab_ref, et_ref, ot_ref, l_ref, t=t, qb=qb, kc=kc, inflight=inflight)
    _attn_b_kernel(ctl_b_ref, q0_ref, k0_ref, v0_ref, q1_ref, k1_ref, v1_ref, q2_ref, k2_ref, v2_ref, ob_ref,
                   acc_ref, den_ref, kt_ref, t=t, qb=qb)


def _attn_ab(qkv_nat, qkv_g1, qkv_g2, ctl_a, ctl_b, lq1, lk1, lq2, lk2, subln, b, t, qb=128, kc=512, inflight=8):
    assert A_HEADS == B_HEADS
    n = b * t
    nblk = t // qb
    vec = lambda w: _const_spec((1, w))
    blk = lambda off: pl.BlockSpec((None, t, LANES), lambda h, i: (off + h, i, 0))
    smem = pl.BlockSpec(memory_space=pltpu.SMEM)
    out = pl.BlockSpec((None, t, LANES), lambda h, i: (h, i, 0))
    return pl.pallas_call(
        functools.partial(_attn_ab_kernel, t=t, qb=qb, kc=kc, inflight=inflight),
        grid=(A_HEADS, b),
        in_specs=[
            smem, smem,
            blk(0), blk(A_HEADS), blk(2 * A_HEADS),
            blk(12), blk(16), blk(20),
            blk(0), blk(4), blk(8),
            blk(0), blk(4), blk(8),
            vec(A_HEAD_DIM), vec(A_HEAD_DIM), vec(A_HEAD_DIM), vec(A_HEAD_DIM),
            vec(A_V_DIM),
        ],
        out_specs=[out, out],
        out_shape=[jax.ShapeDtypeStruct((A_HEADS, n, LANES), BF16), jax.ShapeDtypeStruct((B_HEADS, n, LANES), BF16)],
        scratch_shapes=[
            pltpu.VMEM((LANES, t), BF16),
            pltpu.VMEM((2 * t - qb, qb), F32),
            pltpu.VMEM((inflight, t, 2 * qb), BF16),
            pltpu.VMEM((nblk, LANES, 2 * qb), F32),
            pltpu.VMEM((nblk, 2 * qb), F32),
            pltpu.VMEM((len(B_GROUPS), t, LANES), F32),
            pltpu.VMEM((len(B_GROUPS), t, LANES), F32),
            pltpu.VMEM((2, LANES, t), BF16),
        ],
        compiler_params=pltpu.CompilerParams(
            dimension_semantics=("arbitrary", "arbitrary"), vmem_limit_bytes=VMEM_LIMIT_BYTES),
        name="attn_ab",
    )(ctl_a, ctl_b, qkv_nat, qkv_nat, qkv_nat, qkv_nat, qkv_nat, qkv_nat, qkv_g1, qkv_g1, qkv_g1,
      qkv_g2, qkv_g2, qkv_g2, lq1, lk1, lq2, lk2, subln)


def _merge_kernel(x_ref, a_ref, b_ref, cq_ref, mem_ref, gm_ref, wkv_ref, gk_ref, g_ref, wg_ref, bg_ref, wb_ref,
                  wo_ref, o_ref, ckv_ref, *, tiles_per_batch):
    half = C_HEADS * C_HEAD_DIM

    @pl.when(pl.program_id(0) % tiles_per_batch == 0)
    def _():
        mn = _rms(mem_ref[0], gm_ref[...]).astype(BF16)
        kv = _dot(mn, wkv_ref[...])
        ckv_ref[:, :half] = _head_norm(kv[:, :half], gk_ref[...], "full").astype(BF16)
        ckv_ref[:, half:] = kv[:, half:].astype(BF16)

    x = x_ref[...]
    hb = _rms(x, g_ref[...]).astype(BF16)
    d = x.shape[-1]

    def gated(g, branch):
        gate = jax.nn.sigmoid(_dot(hb, wg_ref[:, g * d:(g + 1) * d]) + bg_ref[:, g * d:(g + 1) * d])
        return gate * _dot(branch, wb_ref[g])

    wide = lambda ref: jnp.concatenate([ref[hh] for hh in range(HEADS_PER_CHUNK)], axis=1)
    acc = gated(0, wide(a_ref)) + gated(1, wide(b_ref))
    heads = []
    for hh in range(C_HEADS):
        cols = slice(hh * C_HEAD_DIM, (hh + 1) * C_HEAD_DIM)
        s = _dot_nt(cq_ref[hh], ckv_ref[:, cols])
        e = jnp.exp2(s - jnp.max(s, axis=-1, keepdims=True))
        inv = 1.0 / jnp.sum(e, axis=-1, keepdims=True)
        heads.append(_dot(e.astype(BF16), ckv_ref[:, half + hh * C_HEAD_DIM:half + (hh + 1) * C_HEAD_DIM]) * inv)
    acc = acc + gated(2, jnp.concatenate(heads, axis=1).astype(BF16))
    o_ref[...] = x + _dot(acc.astype(BF16), wo_ref[...])


def _merge(x2, out_a, out_b, qkv_nat, mem, gm, w_kv, gk4, norm_g, w_gate, b_gate, w_branch, w_out, t, tm=512):
    n, d = x2.shape
    n_mem, kv_cols = mem.shape[1], w_kv.shape[1]
    row = lambda w: pl.BlockSpec((tm, w), lambda i: (i, 0))
    heads = lambda chunk: pl.BlockSpec((HEADS_PER_CHUNK, tm, LANES), lambda i: (chunk, i, 0))
    return pl.pallas_call(
        functools.partial(_merge_kernel, tiles_per_batch=t // tm),
        grid=(n // tm,),
        in_specs=[
            row(d), heads(0), heads(0), heads(NAT_HEADS // HEADS_PER_CHUNK - 1),
            pl.BlockSpec((1, n_mem, d), lambda i: (i // (t // tm), 0, 0)),
            _const_spec((1, d)), _const_spec((d, kv_cols)), _const_spec((1, kv_cols // 2)),
            _const_spec((1, d)),
            _const_spec((d, N_BRANCHES * d)),
            _const_spec((1, N_BRANCHES * d)),
            _const_spec((N_BRANCHES, BRANCH_WIDTH, d)),
            _const_spec((d, d)),
        ],
        out_specs=row(d),
        out_shape=jax.ShapeDtypeStruct((n, d), F32),
        scratch_shapes=[pltpu.VMEM((n_mem, kv_cols), BF16)],
        compiler_params=pltpu.CompilerParams(
            dimension_semantics=("arbitrary",), vmem_limit_bytes=VMEM_LIMIT_BYTES),
        name="merge",
    )(x2, out_a, out_b, qkv_nat, mem, gm, w_kv, gk4, norm_g, w_gate, b_gate, w_branch, w_out)


def _ffn_kernel(x_ref, g_ref, wg_ref, wu_ref, wd_ref, o_ref, *, chunks):
    x = x_ref[...]
    hb = _rms(x, g_ref[...]).astype(BF16)
    acc = x
    for c0, c1 in chunks:
        gt = _dot(hb, wg_ref[:, c0:c1])
        up = _dot(hb, wu_ref[:, c0:c1])
        acc = acc + _dot((jax.nn.silu(gt) * up).astype(BF16), wd_ref[c0:c1, :])
    o_ref[...] = acc


def _ffn(x2, norm_g, w_gate, w_up, w_down, tm=512, fc=768):
    n, d = x2.shape
    d_ff = w_gate.shape[1]
    chunks = tuple((c, min(c + fc, d_ff)) for c in range(0, d_ff, fc))
    row = pl.BlockSpec((tm, d), lambda i: (i, 0))
    return pl.pallas_call(
        functools.partial(_ffn_kernel, chunks=chunks),
        grid=(n // tm,),
        in_specs=[row, _const_spec((1, d)), _const_spec((d, d_ff)), _const_spec((d, d_ff)),
                  _const_spec((d_ff, d))],
        out_specs=row,
        out_shape=jax.ShapeDtypeStruct((n, d), F32),
        compiler_params=pltpu.CompilerParams(
            dimension_semantics=("arbitrary",), vmem_limit_bytes=VMEM_LIMIT_BYTES),
        name="ffn",
    )(x2, norm_g, w_gate, w_up, w_down)


def _score_ctl(slopes, head_dim, gq, gk):
    bound = math.sqrt(head_dim) * jnp.max(jnp.abs(gq)) * jnp.max(jnp.abs(gk))
    return jnp.concatenate([slopes.reshape(-1),
                            jnp.stack([bound * LOG2E, (bound <= MAX_SAFE_SCORE_BOUND).astype(F32)])])


def kernel(x, mem, norm_mix, w_in, w_gate, b_gate, a_q_norm, a_k_norm, a_lambda_q1, a_lambda_k1, a_lambda_q2,
           a_lambda_k2, a_subln, b_q_norm, b_k_norm, mem_norm, w_mem_kv, c_q_norm, c_k_norm, w_branch, w_out,
           norm_ffn, w_ffn_gate, w_ffn_up, w_ffn_down):
    b, t, d = x.shape
    n = b * t
    n_groups = len(B_GROUPS)
    slopes_a = jnp.exp2(-ALIBI_MAX_BIAS * jnp.arange(1, A_HEADS + 1, dtype=F32) / A_HEADS)
    nb = n_groups * B_HEADS
    slopes_b = jnp.exp2(-ALIBI_MAX_BIAS * jnp.arange(1, nb + 1, dtype=F32) / nb)

    l = 0
    bw = BRANCH_WIDTH
    row = lambda v: v.reshape(1, -1)
    tiled = lambda v: jnp.tile(v, bw // v.shape[0])
    ones = jnp.ones((bw,), F32)
    a_qs = A_HEAD_DIM ** -0.5 * LOG2E
    b_qs = B_HEAD_DIM ** -0.5 * LOG2E
    c_qs = C_HEAD_DIM ** -0.5 * LOG2E
    gain_nat = jnp.stack([tiled(a_q_norm[l]) * a_qs, tiled(a_k_norm[l]), ones, tiled(b_q_norm[l]) * b_qs,
                          tiled(b_k_norm[l]), ones, tiled(c_q_norm[l]) * c_qs])
    gain_grp = jnp.stack([tiled(b_q_norm[l]) * b_qs, tiled(b_k_norm[l]), ones])

    qkv_nat, qkv_g1, qkv_g2 = _project(x, row(norm_mix[l]), w_in[l], gain_nat, gain_grp)
    qkv_nat = qkv_nat.reshape(NAT_HEADS, n, LANES)
    qkv_g1 = qkv_g1.reshape(GRP_HEADS, n, LANES)
    qkv_g2 = qkv_g2.reshape(GRP_HEADS, n, LANES)

    out_a, out_b = _attn_ab(qkv_nat, qkv_g1, qkv_g2,
                            _score_ctl(slopes_a, A_HEAD_DIM, a_q_norm[l], a_k_norm[l]),
                            _score_ctl(slopes_b, B_HEAD_DIM, b_q_norm[l], b_k_norm[l]),
                            row(a_lambda_q1[l]), row(a_lambda_k1[l]), row(a_lambda_q2[l]), row(a_lambda_k2[l]),
                            row(a_subln[l]), b, t)
    x2 = x.reshape(n, d)
    x2 = _merge(x2, out_a, out_b, qkv_nat, mem, row(mem_norm[l]), w_mem_kv[l], row(tiled(c_k_norm[l])),
                row(norm_mix[l]), w_gate[l], row(b_gate[l]), w_branch[l], w_out[l], t)
    x2 = _ffn(x2, row(norm_ffn[l]), w_ffn_gate[l], w_ffn_up[l], w_ffn_down[l])
    return x2.reshape(b, t, d)
```

```python
import functools
import math

import jax
import jax.numpy as jnp
from jax import lax
from jax.experimental import pallas as pl
from jax.experimental.pallas import tpu as pltpu

F32 = jnp.float32
BF16 = jnp.bfloat16

D_MODEL = 1024
A_HEADS = 4
A_HEAD_DIM = 64
A_V_DIM = 2 * A_HEAD_DIM
B_GROUPS = ((128, 1), (512, 4), (2048, 16))
B_HEADS = 4
B_HEAD_DIM = 128
C_HEADS = 4
C_HEAD_DIM = 128
BRANCH_WIDTH = 512
N_BRANCHES = 3
EPS = 1e-6
ALIBI_MAX_BIAS = 8.0
LAMBDA_INIT = 0.8 - 0.6 * math.exp(-0.3 * 0)

LANES = 128
VMEM_LIMIT_BYTES = 56 * 1024 * 1024

NAT_COLS = 7 * BRANCH_WIDTH
GRP_COLS = 3 * BRANCH_WIDTH
HEADS_PER_CHUNK = BRANCH_WIDTH // LANES
NAT_HEADS = NAT_COLS // LANES
GRP_HEADS = GRP_COLS // LANES
NAT_NORMS = ("half", "half", "none", "full", "full", "none", "full")
GRP_COL_OFFS = (3 * BRANCH_WIDTH, 6 * BRANCH_WIDTH, 9 * BRANCH_WIDTH)
NAT_COL_OFFS = (0, BRANCH_WIDTH, 2 * BRANCH_WIDTH) + GRP_COL_OFFS + (12 * BRANCH_WIDTH,)
GRP_NORMS = ("full", "full", "none")
NEG_BIG = -1e30
LOG2E = 1.4426950408889634
MAX_SAFE_SCORE_BOUND = 40.0


def _rms(x, gain):
    return x * lax.rsqrt(jnp.mean(x * x, axis=-1, keepdims=True) + EPS) * gain


def _dot(a, b):
    return jnp.dot(a, b, preferred_element_type=F32)


def _dot_nt(a, b):
    return lax.dot_general(a, b, (((1,), (1,)), ((), ())), preferred_element_type=F32)


def _const_spec(shape):
    nd = len(shape)
    return pl.BlockSpec(shape, lambda *_: (0,) * nd, pipeline_mode=pl.Buffered(1))


def _head_norm(y, gain, kind):
    if kind == "none":
        return y
    lo_mask = lax.broadcasted_iota(jnp.int32, (1, LANES), 1) < A_HEAD_DIM
    cols = []
    for c in range(0, y.shape[1], LANES):
        z = y[:, c:c + LANES]
        sq = z * z
        s_all = jnp.sum(sq, axis=-1, keepdims=True)
        if kind == "full":
            ms = s_all * (1.0 / LANES)
        else:
            s_lo = jnp.sum(jnp.where(lo_mask, sq, 0.0), axis=-1, keepdims=True)
            ms = jnp.where(lo_mask, s_lo, s_all - s_lo) * (1.0 / A_HEAD_DIM)
        cols.append(z * lax.rsqrt(ms + EPS))
    return jnp.concatenate(cols, axis=1) * gain


def _proj_kernel(x_ref, g_ref, w_ref, gn_ref, gg_ref, on_ref, o1_ref, o2_ref, slab_ref, p4_ref, hp_ref, *, tm):
    h = _rms(x_ref[0], g_ref[...])
    hb = h.astype(BF16)
    bw = BRANCH_WIDTH
    for ci, kind in enumerate(NAT_NORMS):
        c0 = NAT_COL_OFFS[ci]
        res = _head_norm(_dot(hb, w_ref[:, c0:c0 + bw]), gn_ref[ci:ci + 1, :], kind).astype(BF16)
        for hh in range(HEADS_PER_CHUNK):
            on_ref[ci * HEADS_PER_CHUNK + hh, 0] = res[:, hh * LANES:(hh + 1) * LANES]
    n_slabs = D_MODEL // LANES
    for s in range(n_slabs):
        slab_ref[s] = h[:, s * LANES:(s + 1) * LANES]
    for g, dil, o_ref in ((1, 4, o1_ref), (2, 16, o2_ref)):
        n = tm // dil
        if dil == 4:
            for r in range(dil):
                for s in range(n_slabs):
                    rows = slab_ref[s, pl.ds(r, n, stride=dil), :]
                    p4_ref[s, r * n:(r + 1) * n, :] = rows
                    hp_ref[r * n:(r + 1) * n, s * LANES:(s + 1) * LANES] = rows.astype(BF16)
        else:
            n4 = tm // 4
            for r4 in range(4):
                for q in range(4):
                    r = r4 + 4 * q
                    for s in range(n_slabs):
                        hp_ref[r * n:(r + 1) * n, s * LANES:(s + 1) * LANES] = (
                            p4_ref[s, pl.ds(r4 * n4 + q, n, stride=4), :].astype(BF16))
        hp = hp_ref[...]
        for ci, kind in enumerate(GRP_NORMS):
            c0 = GRP_COL_OFFS[ci] + g * bw
            res = _head_norm(_dot(hp, w_ref[:, c0:c0 + bw]), gg_ref[ci:ci + 1, :], kind).astype(BF16)
            for hh in range(HEADS_PER_CHUNK):
                for r in range(dil):
                    o_ref[ci * HEADS_PER_CHUNK + hh, 0, r] = res[r * n:(r + 1) * n, hh * LANES:(hh + 1) * LANES]


def _project(x, norm_g, w_in, gain_nat, gain_grp, tm=512):
    b, t, d = x.shape
    grid = (b, t // tm)
    return pl.pallas_call(
        functools.partial(_proj_kernel, tm=tm),
        grid=grid,
        in_specs=[
            pl.BlockSpec((1, tm, d), lambda i, j: (i, j, 0)),
            _const_spec((1, d)),
            _const_spec(w_in.shape),
            _const_spec(gain_nat.shape),
            _const_spec(gain_grp.shape),
        ],
        out_specs=[
            pl.BlockSpec((NAT_HEADS, 1, tm, LANES), lambda i, j: (0, i, j, 0)),
            pl.BlockSpec((GRP_HEADS, 1, 4, tm // 4, LANES), lambda i, j: (0, i, 0, j, 0)),
            pl.BlockSpec((GRP_HEADS, 1, 16, tm // 16, LANES), lambda i, j: (0, i, 0, j, 0)),
        ],
        out_shape=[
            jax.ShapeDtypeStruct((NAT_HEADS, b, t, LANES), BF16),
            jax.ShapeDtypeStruct((GRP_HEADS, b, 4, t // 4, LANES), BF16),
            jax.ShapeDtypeStruct((GRP_HEADS, b, 16, t // 16, LANES), BF16),
        ],
        scratch_shapes=[
            pltpu.VMEM((d // LANES, tm, LANES), F32),
            pltpu.VMEM((d // LANES, tm, LANES), F32),
            pltpu.VMEM((tm, d), BF16),
        ],
        compiler_params=pltpu.CompilerParams(
            dimension_semantics=("arbitrary", "arbitrary"), vmem_limit_bytes=VMEM_LIMIT_BYTES),
        name="proj",
    )(x, norm_g, w_in, gain_nat, gain_grp)


def _attn_a_kernel(ctl_ref, q_ref, k_ref, v_ref, lq1_ref, lk1_ref, lq2_ref, lk2_ref,
                   sub_ref, o_ref, vt_ref, tab_ref, et_ref, ot_ref, l_ref, *, t, qb, kc, inflight):
    h = pl.program_id(0)
    nblk = t // qb
    slope2 = ctl_ref[h] * LOG2E
    shift = ctl_ref[A_HEADS]
    fast = ctl_ref[A_HEADS + 1] > 0.5
    lo_mask = lax.broadcasted_iota(jnp.int32, (1, LANES), 1) < A_HEAD_DIM

    lam = (jnp.exp(jnp.sum(lq1_ref[...] * lk1_ref[...], keepdims=True))
           - jnp.exp(jnp.sum(lq2_ref[...] * lk2_ref[...], keepdims=True)) + LAMBDA_INIT)

    vt_ref[...] = v_ref[...].T

    @pl.when(pl.program_id(1) == 0)
    def _():
        cc = lax.broadcasted_iota(jnp.int32, (2 * t - qb, qb), 0)
        il = lax.broadcasted_iota(jnp.int32, (2 * t - qb, qb), 1)
        tab_ref[...] = (-slope2 * jnp.abs(cc - (t - qb) - il).astype(F32)
                        - jnp.where(fast, shift, 0.0))

    def fold8(x, op):
        return op(x.reshape(x.shape[0] // 8, 8, x.shape[1]), axis=0)

    def block(j, slot, exact_max):
        q = q_ref[pl.ds(pl.multiple_of(j * qb, qb), qb), :]
        zero = jnp.zeros_like(q)
        q2 = jnp.concatenate([jnp.where(lo_mask, q, zero), jnp.where(lo_mask, zero, q)], axis=0)
        off = pl.multiple_of(t - qb - j * qb, qb)

        def scores(c):
            s = _dot_nt(k_ref[c * kc:(c + 1) * kc, :], q2)
            bias = tab_ref[pl.ds(off + c * kc, kc), :]
            return s + jnp.concatenate([bias, bias], axis=1)

        m = None
        if exact_max:
            for c in range(t // kc):
                cm = fold8(scores(c), jnp.max)
                m = cm if m is None else jnp.maximum(m, cm)
            m = jnp.max(m, axis=0, keepdims=True)
        acc = None
        for c in range(t // kc):
            s = scores(c)
            e = jnp.exp2(s - m if exact_max else s)
            et_ref[slot, c * kc:(c + 1) * kc, :] = e.astype(BF16)
            part = fold8(e, jnp.sum)
            acc = part if acc is None else acc + part
        l_ref[pl.ds(j, 1), :] = jnp.sum(acc, axis=0, keepdims=True)
        ot_ref[j] = _dot(vt_ref[...], et_ref[slot])

    def finish(j):
        ot = ot_ref[j]
        inv = 1.0 / l_ref[pl.ds(j, 1), :]
        o = ot[:, :qb] * inv[:, :qb] - ot[:, qb:] * (lam * inv[:, qb:])
        o = _rms(o.T, sub_ref[...]) * (1.0 - LAMBDA_INIT)
        o_ref[pl.ds(pl.multiple_of(j * qb, qb), qb), :] = o.astype(BF16)

    def run(exact_max):
        for s in range(inflight):
            block(s, s, exact_max)

        def step(i, carry):
            for s in range(inflight):
                finish(inflight * (i - 1) + s)
            for s in range(inflight):
                block(inflight * i + s, s, exact_max)
            return carry
        lax.fori_loop(1, nblk // inflight, step, 0)
        for s in range(inflight):
            finish(nblk - inflight + s)

    @pl.when(fast)
    def _():
        run(False)

    @pl.when(jnp.logical_not(fast))
    def _():
        run(True)


def _attn_a(qkv_nat, ctl, lq1, lk1, lq2, lk2, subln, b, t, qb=128, kc=512, inflight=8):
    n = b * t
    nblk = t // qb
    vec = lambda w: _const_spec((1, w))
    return pl.pallas_call(
        functools.partial(_attn_a_kernel, t=t, qb=qb, kc=kc, inflight=inflight),
        grid=(A_HEADS, b),
        in_specs=[
            pl.BlockSpec(memory_space=pltpu.SMEM),
            pl.BlockSpec((None, t, LANES), lambda h, i: (h, i, 0)),
            pl.BlockSpec((None, t, LANES), lambda h, i: (A_HEADS + h, i, 0)),
            pl.BlockSpec((None, t, LANES), lambda h, i: (2 * A_HEADS + h, i, 0)),
            vec(A_HEAD_DIM), vec(A_HEAD_DIM), vec(A_HEAD_DIM), vec(A_HEAD_DIM),
            vec(A_V_DIM),
        ],
        out_specs=pl.BlockSpec((None, t, LANES), lambda h, i: (h, i, 0)),
        out_shape=jax.ShapeDtypeStruct((A_HEADS, n, LANES), BF16),
        scratch_shapes=[
            pltpu.VMEM((LANES, t), BF16),
            pltpu.VMEM((2 * t - qb, qb), F32),
            pltpu.VMEM((inflight, t, 2 * qb), BF16),
            pltpu.VMEM((nblk, LANES, 2 * qb), F32),
            pltpu.VMEM((nblk, 2 * qb), F32),
        ],
        compiler_params=pltpu.CompilerParams(
            dimension_semantics=("arbitrary", "arbitrary"), vmem_limit_bytes=VMEM_LIMIT_BYTES),
        name="attn_a",
    )(ctl, qkv_nat, qkv_nat, qkv_nat, lq1, lk1, lq2, lk2, subln)


def _attn_b_kernel(ctl_ref, q0_ref, k0_ref, v0_ref, q1_ref, k1_ref, v1_ref, q2_ref, k2_ref, v2_ref,
                   o_ref, acc_ref, den_ref, kt_ref, *, t, qb):
    h = pl.program_id(1)
    n_ctl = len(B_GROUPS) * B_HEADS
    shift = ctl_ref[n_ctl]
    fast = ctl_ref[n_ctl + 1] > 0.5
    refs = ((q0_ref, k0_ref, v0_ref), (q1_ref, k1_ref, v1_ref), (q2_ref, k2_ref, v2_ref))

    def blocks(g):
        window, dil = B_GROUPS[g]
        n_side = window // (2 * dil)
        sub = t // dil
        wk = min(2 * qb, sub)
        for r in range(dil):
            for j in range(sub // qb):
                i0 = j * qb
                ws = min(max(i0 - n_side, 0), sub - wk)
                yield r, i0, r * sub + i0, r * sub + ws, wk, i0 - ws, n_side

    def bias_table(g, wk, offset, n_side, sub_shift):
        slope2 = ctl_ref[g * B_HEADS + h] * (float(B_GROUPS[g][1]) * LOG2E)
        ql = lax.broadcasted_iota(jnp.int32, (qb, wk), 0)
        kl = lax.broadcasted_iota(jnp.int32, (qb, wk), 1)
        dist = jnp.abs(kl - ql - offset)
        return jnp.where(dist <= n_side, -slope2 * dist.astype(F32) - sub_shift, NEG_BIG)

    def store_rows(ref, g, r, i0, val):
        dil = B_GROUPS[g][1]
        if dil == 1:
            ref[g, i0:i0 + qb, :] = val
        else:
            ref[g, pl.ds(r + dil * i0, qb, stride=dil), :] = val

    @pl.when(fast)
    def _():
        half = qb // 2
        for g in range(len(B_GROUPS)):
            q_ref, k_ref, v_ref = refs[g]
            kt_ref[0] = k_ref[...].T
            kt_ref[1] = jnp.concatenate([k_ref[half:, :], k_ref[:half, :]], axis=0).T
            tables = {}
            for r, i0, qrow, krow, wk, offset, n_side in blocks(g):
                if offset not in tables:
                    tables[offset] = bias_table(g, wk, offset, n_side, shift)
                if krow % qb == 0:
                    kwt = kt_ref[0, :, krow:krow + wk]
                else:
                    kwt = kt_ref[1, :, krow - half:krow - half + wk]
                s = _dot(q_ref[qrow:qrow + qb, :], kwt)
                e = jnp.exp2(s + tables[offset])
                den = jnp.sum(e, axis=-1, keepdims=True)
                store_rows(acc_ref, g, r, i0, _dot(e.astype(BF16), v_ref[krow:krow + wk, :]))
                store_rows(den_ref, g, r, i0, jnp.broadcast_to(den, (qb, LANES)))
        o_ref[...] = ((acc_ref[0] + acc_ref[1] + acc_ref[2])
                      / (den_ref[0] + den_ref[1] + den_ref[2])).astype(BF16)

    @pl.when(jnp.logical_not(fast))
    def _():
        for g in range(len(B_GROUPS)):
            q_ref, k_ref, v_ref = refs[g]
            tables = {}
            for r, i0, qrow, krow, wk, offset, n_side in blocks(g):
                if offset not in tables:
                    tables[offset] = bias_table(g, wk, offset, n_side, 0.0)
                s = _dot_nt(q_ref[qrow:qrow + qb, :], k_ref[krow:krow + wk, :]) + tables[offset]
                m = jnp.max(s, axis=-1, keepdims=True)
                e = jnp.exp2(s - m)
                l = jnp.sum(e, axis=-1, keepdims=True)
                o = _dot(e.astype(BF16), v_ref[krow:krow + wk, :]) * (1.0 / l)
                store_rows(acc_ref, g, r, i0, o)
                store_rows(den_ref, g, r, i0, jnp.broadcast_to(m + jnp.log2(l), (qb, LANES)))
        l0, l1, l2 = den_ref[0], den_ref[1], den_ref[2]
        m = jnp.maximum(jnp.maximum(l0, l1), l2)
        e0, e1, e2 = jnp.exp2(l0 - m), jnp.exp2(l1 - m), jnp.exp2(l2 - m)
        o_ref[...] = ((e0 * acc_ref[0] + e1 * acc_ref[1] + e2 * acc_ref[2]) / (e0 + e1 + e2)).astype(BF16)


def _attn_b(qkv_nat, qkv_g1, qkv_g2, ctl, b, t, qb=128):
    n = b * t
    blk = lambda off: pl.BlockSpec((None, t, LANES), lambda i, h: (off + h, i, 0))
    return pl.pallas_call(
        functools.partial(_attn_b_kernel, t=t, qb=qb),
        grid=(b, B_HEADS),
        in_specs=[
            pl.BlockSpec(memory_space=pltpu.SMEM),
            blk(12), blk(16), blk(20),
            blk(0), blk(4), blk(8),
            blk(0), blk(4), blk(8),
        ],
        out_specs=pl.BlockSpec((None, t, LANES), lambda i, h: (h, i, 0)),
        out_shape=jax.ShapeDtypeStruct((B_HEADS, n, LANES), BF16),
        scratch_shapes=[
            pltpu.VMEM((len(B_GROUPS), t, LANES), F32),
            pltpu.VMEM((len(B_GROUPS), t, LANES), F32),
            pltpu.VMEM((2, LANES, t), BF16),
        ],
        compiler_params=pltpu.CompilerParams(
            dimension_semantics=("arbitrary", "arbitrary"), vmem_limit_bytes=VMEM_LIMIT_BYTES),
        name="attn_b",
    )(ctl, qkv_nat, qkv_nat, qkv_nat, qkv_g1, qkv_g1, qkv_g1, qkv_g2, qkv_g2, qkv_g2)


def _merge_kernel(x_ref, a_ref, b_ref, cq_ref, mem_ref, gm_ref, wkv_ref, gk_ref, g_ref, wg_ref, bg_ref, wb_ref,
                  wo_ref, o_ref, ckv_ref, *, tiles_per_batch):
    half = C_HEADS * C_HEAD_DIM

    @pl.when(pl.program_id(0) % tiles_per_batch == 0)
    def _():
        mn = _rms(mem_ref[0], gm_ref[...]).astype(BF16)
        kv = _dot(mn, wkv_ref[...])
        ckv_ref[:, :half] = _head_norm(kv[:, :half], gk_ref[...], "full").astype(BF16)
        ckv_ref[:, half:] = kv[:, half:].astype(BF16)

    x = x_ref[...]
    hb = _rms(x, g_ref[...]).astype(BF16)
    d = x.shape[-1]

    def gated(g, branch):
        gate = jax.nn.sigmoid(_dot(hb, wg_ref[:, g * d:(g + 1) * d]) + bg_ref[:, g * d:(g + 1) * d])
        return gate * _dot(branch, wb_ref[g])

    wide = lambda ref: jnp.concatenate([ref[hh] for hh in range(HEADS_PER_CHUNK)], axis=1)
    acc = gated(0, wide(a_ref)) + gated(1, wide(b_ref))
    heads = []
    for hh in range(C_HEADS):
        cols = slice(hh * C_HEAD_DIM, (hh + 1) * C_HEAD_DIM)
        s = _dot_nt(cq_ref[hh], ckv_ref[:, cols])
        e = jnp.exp2(s - jnp.max(s, axis=-1, keepdims=True))
        inv = 1.0 / jnp.sum(e, axis=-1, keepdims=True)
        heads.append(_dot(e.astype(BF16), ckv_ref[:, half + hh * C_HEAD_DIM:half + (hh + 1) * C_HEAD_DIM]) * inv)
    acc = acc + gated(2, jnp.concatenate(heads, axis=1).astype(BF16))
    o_ref[...] = x + _dot(acc.astype(BF16), wo_ref[...])


def _merge(x2, out_a, out_b, qkv_nat, mem, gm, w_kv, gk4, norm_g, w_gate, b_gate, w_branch, w_out, t, tm=512):
    n, d = x2.shape
    n_mem, kv_cols = mem.shape[1], w_kv.shape[1]
    row = lambda w: pl.BlockSpec((tm, w), lambda i: (i, 0))
    heads = lambda chunk: pl.BlockSpec((HEADS_PER_CHUNK, tm, LANES), lambda i: (chunk, i, 0))
    return pl.pallas_call(
        functools.partial(_merge_kernel, tiles_per_batch=t // tm),
        grid=(n // tm,),
        in_specs=[
            row(d), heads(0), heads(0), heads(NAT_HEADS // HEADS_PER_CHUNK - 1),
            pl.BlockSpec((1, n_mem, d), lambda i: (i // (t // tm), 0, 0)),
            _const_spec((1, d)), _const_spec((d, kv_cols)), _const_spec((1, kv_cols // 2)),
            _const_spec((1, d)),
            _const_spec((d, N_BRANCHES * d)),
            _const_spec((1, N_BRANCHES * d)),
            _const_spec((N_BRANCHES, BRANCH_WIDTH, d)),
            _const_spec((d, d)),
        ],
        out_specs=row(d),
        out_shape=jax.ShapeDtypeStruct((n, d), F32),
        scratch_shapes=[pltpu.VMEM((n_mem, kv_cols), BF16)],
        compiler_params=pltpu.CompilerParams(
            dimension_semantics=("arbitrary",), vmem_limit_bytes=VMEM_LIMIT_BYTES),
        name="merge",
    )(x2, out_a, out_b, qkv_nat, mem, gm, w_kv, gk4, norm_g, w_gate, b_gate, w_branch, w_out)


def _ffn_kernel(x_ref, g_ref, wg_ref, wu_ref, wd_ref, o_ref, *, chunks):
    x = x_ref[...]
    hb = _rms(x, g_ref[...]).astype(BF16)
    acc = x
    for c0, c1 in chunks:
        gt = _dot(hb, wg_ref[:, c0:c1])
        up = _dot(hb, wu_ref[:, c0:c1])
        acc = acc + _dot((jax.nn.silu(gt) * up).astype(BF16), wd_ref[c0:c1, :])
    o_ref[...] = acc


def _ffn(x2, norm_g, w_gate, w_up, w_down, tm=512, fc=768):
    n, d = x2.shape
    d_ff = w_gate.shape[1]
    chunks = tuple((c, min(c + fc, d_ff)) for c in range(0, d_ff, fc))
    row = pl.BlockSpec((tm, d), lambda i: (i, 0))
    return pl.pallas_call(
        functools.partial(_ffn_kernel, chunks=chunks),
        grid=(n // tm,),
        in_specs=[row, _const_spec((1, d)), _const_spec((d, d_ff)), _const_spec((d, d_ff)),
                  _const_spec((d_ff, d))],
        out_specs=row,
        out_shape=jax.ShapeDtypeStruct((n, d), F32),
        compiler_params=pltpu.CompilerParams(
            dimension_semantics=("arbitrary",), vmem_limit_bytes=VMEM_LIMIT_BYTES),
        name="ffn",
    )(x2, norm_g, w_gate, w_up, w_down)


def _score_ctl(slopes, head_dim, gq, gk):
    bound = math.sqrt(head_dim) * jnp.max(jnp.abs(gq)) * jnp.max(jnp.abs(gk))
    return jnp.concatenate([slopes.reshape(-1),
                            jnp.stack([bound * LOG2E, (bound <= MAX_SAFE_SCORE_BOUND).astype(F32)])])


def kernel(x, mem, norm_mix, w_in, w_gate, b_gate, a_q_norm, a_k_norm, a_lambda_q1, a_lambda_k1, a_lambda_q2,
           a_lambda_k2, a_subln, b_q_norm, b_k_norm, mem_norm, w_mem_kv, c_q_norm, c_k_norm, w_branch, w_out,
           norm_ffn, w_ffn_gate, w_ffn_up, w_ffn_down):
    b, t, d = x.shape
    n = b * t
    n_groups = len(B_GROUPS)
    slopes_a = jnp.exp2(-ALIBI_MAX_BIAS * jnp.arange(1, A_HEADS + 1, dtype=F32) / A_HEADS)
    nb = n_groups * B_HEADS
    slopes_b = jnp.exp2(-ALIBI_MAX_BIAS * jnp.arange(1, nb + 1, dtype=F32) / nb)

    l = 0
    bw = BRANCH_WIDTH
    row = lambda v: v.reshape(1, -1)
    tiled = lambda v: jnp.tile(v, bw // v.shape[0])
    ones = jnp.ones((bw,), F32)
    a_qs = A_HEAD_DIM ** -0.5 * LOG2E
    b_qs = B_HEAD_DIM ** -0.5 * LOG2E
    c_qs = C_HEAD_DIM ** -0.5 * LOG2E
    gain_nat = jnp.stack([tiled(a_q_norm[l]) * a_qs, tiled(a_k_norm[l]), ones, tiled(b_q_norm[l]) * b_qs,
                          tiled(b_k_norm[l]), ones, tiled(c_q_norm[l]) * c_qs])
    gain_grp = jnp.stack([tiled(b_q_norm[l]) * b_qs, tiled(b_k_norm[l]), ones])

    qkv_nat, qkv_g1, qkv_g2 = _project(x, row(norm_mix[l]), w_in[l], gain_nat, gain_grp)
    qkv_nat = qkv_nat.reshape(NAT_HEADS, n, LANES)
    qkv_g1 = qkv_g1.reshape(GRP_HEADS, n, LANES)
    qkv_g2 = qkv_g2.reshape(GRP_HEADS, n, LANES)

    out_a = _attn_a(qkv_nat, _score_ctl(slopes_a, A_HEAD_DIM, a_q_norm[l], a_k_norm[l]),
                    row(a_lambda_q1[l]), row(a_lambda_k1[l]), row(a_lambda_q2[l]), row(a_lambda_k2[l]),
                    row(a_subln[l]), b, t)
    out_b = _attn_b(qkv_nat, qkv_g1, qkv_g2, _score_ctl(slopes_b, B_HEAD_DIM, b_q_norm[l], b_k_norm[l]), b, t)
    x2 = x.reshape(n, d)
    x2 = _merge(x2, out_a, out_b, qkv_nat, mem, row(mem_norm[l]), w_mem_kv[l], row(tiled(c_k_norm[l])),
                row(norm_mix[l]), w_gate[l], row(b_gate[l]), w_branch[l], w_out[l], t)
    x2 = _ffn(x2, row(norm_ffn[l]), w_ffn_gate[l], w_ffn_up[l], w_ffn_down[l])
    return x2.reshape(b, t, d)
```

```python
import functools
import math

import jax
import jax.numpy as jnp
from jax import lax
from jax.experimental import pallas as pl
from jax.experimental.pallas import tpu as pltpu

F32 = jnp.float32
BF16 = jnp.bfloat16

D_MODEL = 1024
A_HEADS = 4
A_HEAD_DIM = 64
A_V_DIM = 2 * A_HEAD_DIM
B_GROUPS = ((128, 1), (512, 4), (2048, 16))
B_HEADS = 4
B_HEAD_DIM = 128
C_HEADS = 4
C_HEAD_DIM = 128
BRANCH_WIDTH = 512
N_BRANCHES = 3
EPS = 1e-6
ALIBI_MAX_BIAS = 8.0
LAMBDA_INIT = 0.8 - 0.6 * math.exp(-0.3 * 0)

LANES = 128
VMEM_LIMIT_BYTES = 56 * 1024 * 1024

NAT_COLS = 7 * BRANCH_WIDTH
GRP_COLS = 3 * BRANCH_WIDTH
HEADS_PER_CHUNK = BRANCH_WIDTH // LANES
NAT_HEADS = NAT_COLS // LANES
GRP_HEADS = GRP_COLS // LANES
NAT_NORMS = ("half", "half", "none", "full", "full", "none", "full")
GRP_COL_OFFS = (3 * BRANCH_WIDTH, 6 * BRANCH_WIDTH, 9 * BRANCH_WIDTH)
NAT_COL_OFFS = (0, BRANCH_WIDTH, 2 * BRANCH_WIDTH) + GRP_COL_OFFS + (12 * BRANCH_WIDTH,)
GRP_NORMS = ("full", "full", "none")
NEG_BIG = -1e30
LOG2E = 1.4426950408889634
MAX_SAFE_SCORE_BOUND = 40.0


def _rms(x, gain):
    return x * lax.rsqrt(jnp.mean(x * x, axis=-1, keepdims=True) + EPS) * gain


def _dot(a, b):
    return jnp.dot(a, b, preferred_element_type=F32)


def _dot_nt(a, b):
    return lax.dot_general(a, b, (((1,), (1,)), ((), ())), preferred_element_type=F32)


def _const_spec(shape):
    nd = len(shape)
    return pl.BlockSpec(shape, lambda *_: (0,) * nd, pipeline_mode=pl.Buffered(1))


def _head_norm(y, gain, kind):
    if kind == "none":
        return y
    lo_mask = lax.broadcasted_iota(jnp.int32, (1, LANES), 1) < A_HEAD_DIM
    cols = []
    for c in range(0, y.shape[1], LANES):
        z = y[:, c:c + LANES]
        sq = z * z
        s_all = jnp.sum(sq, axis=-1, keepdims=True)
        if kind == "full":
            ms = s_all * (1.0 / LANES)
        else:
            s_lo = jnp.sum(jnp.where(lo_mask, sq, 0.0), axis=-1, keepdims=True)
            ms = jnp.where(lo_mask, s_lo, s_all - s_lo) * (1.0 / A_HEAD_DIM)
        cols.append(z * lax.rsqrt(ms + EPS))
    return jnp.concatenate(cols, axis=1) * gain


def _proj_kernel(x_ref, g_ref, w_ref, gn_ref, gg_ref, on_ref, o1_ref, o2_ref, slab_ref, p4_ref, hp_ref, *, tm):
    h = _rms(x_ref[0], g_ref[...])
    hb = h.astype(BF16)
    bw = BRANCH_WIDTH
    for ci, kind in enumerate(NAT_NORMS):
        c0 = NAT_COL_OFFS[ci]
        res = _head_norm(_dot(hb, w_ref[:, c0:c0 + bw]), gn_ref[ci:ci + 1, :], kind).astype(BF16)
        for hh in range(HEADS_PER_CHUNK):
            on_ref[ci * HEADS_PER_CHUNK + hh, 0] = res[:, hh * LANES:(hh + 1) * LANES]
    n_slabs = D_MODEL // LANES
    for s in range(n_slabs):
        slab_ref[s] = h[:, s * LANES:(s + 1) * LANES]
    for g, dil, o_ref in ((1, 4, o1_ref), (2, 16, o2_ref)):
        n = tm // dil
        if dil == 4:
            for r in range(dil):
                for s in range(n_slabs):
                    rows = slab_ref[s, pl.ds(r, n, stride=dil), :]
                    p4_ref[s, r * n:(r + 1) * n, :] = rows
                    hp_ref[r * n:(r + 1) * n, s * LANES:(s + 1) * LANES] = rows.astype(BF16)
        else:
            n4 = tm // 4
            for r4 in range(4):
                for q in range(4):
                    r = r4 + 4 * q
                    for s in range(n_slabs):
                        hp_ref[r * n:(r + 1) * n, s * LANES:(s + 1) * LANES] = (
                            p4_ref[s, pl.ds(r4 * n4 + q, n, stride=4), :].astype(BF16))
        hp = hp_ref[...]
        for ci, kind in enumerate(GRP_NORMS):
            c0 = GRP_COL_OFFS[ci] + g * bw
            res = _head_norm(_dot(hp, w_ref[:, c0:c0 + bw]), gg_ref[ci:ci + 1, :], kind).astype(BF16)
            for hh in range(HEADS_PER_CHUNK):
                for r in range(dil):
                    o_ref[ci * HEADS_PER_CHUNK + hh, 0, r] = res[r * n:(r + 1) * n, hh * LANES:(hh + 1) * LANES]


def _project(x, norm_g, w_in, gain_nat, gain_grp, tm=512):
    b, t, d = x.shape
    grid = (b, t // tm)
    return pl.pallas_call(
        functools.partial(_proj_kernel, tm=tm),
        grid=grid,
        in_specs=[
            pl.BlockSpec((1, tm, d), lambda i, j: (i, j, 0)),
            _const_spec((1, d)),
            _const_spec(w_in.shape),
            _const_spec(gain_nat.shape),
            _const_spec(gain_grp.shape),
        ],
        out_specs=[
            pl.BlockSpec((NAT_HEADS, 1, tm, LANES), lambda i, j: (0, i, j, 0)),
            pl.BlockSpec((GRP_HEADS, 1, 4, tm // 4, LANES), lambda i, j: (0, i, 0, j, 0)),
            pl.BlockSpec((GRP_HEADS, 1, 16, tm // 16, LANES), lambda i, j: (0, i, 0, j, 0)),
        ],
        out_shape=[
            jax.ShapeDtypeStruct((NAT_HEADS, b, t, LANES), BF16),
            jax.ShapeDtypeStruct((GRP_HEADS, b, 4, t // 4, LANES), BF16),
            jax.ShapeDtypeStruct((GRP_HEADS, b, 16, t // 16, LANES), BF16),
        ],
        scratch_shapes=[
            pltpu.VMEM((d // LANES, tm, LANES), F32),
            pltpu.VMEM((d // LANES, tm, LANES), F32),
            pltpu.VMEM((tm, d), BF16),
        ],
        compiler_params=pltpu.CompilerParams(
            dimension_semantics=("arbitrary", "arbitrary"), vmem_limit_bytes=VMEM_LIMIT_BYTES),
        name="proj",
    )(x, norm_g, w_in, gain_nat, gain_grp)


def _attn_a_kernel(ctl_ref, q_ref, k_ref, v_ref, lq1_ref, lk1_ref, lq2_ref, lk2_ref,
                   sub_ref, o_ref, vt_ref, tab_ref, et_ref, ot_ref, l_ref, *, t, qb, kc, inflight):
    h = pl.program_id(0)
    nblk = t // qb
    slope2 = ctl_ref[h] * LOG2E
    shift = ctl_ref[A_HEADS]
    fast = ctl_ref[A_HEADS + 1] > 0.5
    lo_mask = lax.broadcasted_iota(jnp.int32, (1, LANES), 1) < A_HEAD_DIM

    lam = (jnp.exp(jnp.sum(lq1_ref[...] * lk1_ref[...], keepdims=True))
           - jnp.exp(jnp.sum(lq2_ref[...] * lk2_ref[...], keepdims=True)) + LAMBDA_INIT)

    vt_ref[...] = v_ref[...].T

    @pl.when(pl.program_id(1) == 0)
    def _():
        cc = lax.broadcasted_iota(jnp.int32, (2 * t - qb, qb), 0)
        il = lax.broadcasted_iota(jnp.int32, (2 * t - qb, qb), 1)
        tab_ref[...] = (-slope2 * jnp.abs(cc - (t - qb) - il).astype(F32)
                        - jnp.where(fast, shift, 0.0))

    def fold8(x, op):
        return op(x.reshape(x.shape[0] // 8, 8, x.shape[1]), axis=0)

    def block(j, slot, exact_max):
        q = q_ref[pl.ds(pl.multiple_of(j * qb, qb), qb), :]
        zero = jnp.zeros_like(q)
        q2 = jnp.concatenate([jnp.where(lo_mask, q, zero), jnp.where(lo_mask, zero, q)], axis=0)
        off = pl.multiple_of(t - qb - j * qb, qb)

        def scores(c):
            s = _dot_nt(k_ref[c * kc:(c + 1) * kc, :], q2)
            bias = tab_ref[pl.ds(off + c * kc, kc), :]
            return s + jnp.concatenate([bias, bias], axis=1)

        m = None
        if exact_max:
            for c in range(t // kc):
                cm = fold8(scores(c), jnp.max)
                m = cm if m is None else jnp.maximum(m, cm)
            m = jnp.max(m, axis=0, keepdims=True)
        acc = None
        for c in range(t // kc):
            s = scores(c)
            e = jnp.exp2(s - m if exact_max else s)
            et_ref[slot, c * kc:(c + 1) * kc, :] = e.astype(BF16)
            part = fold8(e, jnp.sum)
            acc = part if acc is None else acc + part
        l_ref[pl.ds(j, 1), :] = jnp.sum(acc, axis=0, keepdims=True)
        ot_ref[j] = _dot(vt_ref[...], et_ref[slot])

    def finish(j):
        ot = ot_ref[j]
        inv = 1.0 / l_ref[pl.ds(j, 1), :]
        o = ot[:, :qb] * inv[:, :qb] - ot[:, qb:] * (lam * inv[:, qb:])
        o = _rms(o.T, sub_ref[...]) * (1.0 - LAMBDA_INIT)
        o_ref[pl.ds(pl.multiple_of(j * qb, qb), qb), :] = o.astype(BF16)

    def run(exact_max):
        for s in range(inflight):
            block(s, s, exact_max)

        def step(i, carry):
            for s in range(inflight):
                finish(inflight * (i - 1) + s)
            for s in range(inflight):
                block(inflight * i + s, s, exact_max)
            return carry
        lax.fori_loop(1, nblk // inflight, step, 0)
        for s in range(inflight):
            finish(nblk - inflight + s)

    @pl.when(fast)
    def _():
        run(False)

    @pl.when(jnp.logical_not(fast))
    def _():
        run(True)


def _attn_a(qkv_nat, ctl, lq1, lk1, lq2, lk2, subln, b, t, qb=128, kc=512, inflight=8):
    n = b * t
    nblk = t // qb
    vec = lambda w: _const_spec((1, w))
    return pl.pallas_call(
        functools.partial(_attn_a_kernel, t=t, qb=qb, kc=kc, inflight=inflight),
        grid=(A_HEADS, b),
        in_specs=[
            pl.BlockSpec(memory_space=pltpu.SMEM),
            pl.BlockSpec((None, t, LANES), lambda h, i: (h, i, 0)),
            pl.BlockSpec((None, t, LANES), lambda h, i: (A_HEADS + h, i, 0)),
            pl.BlockSpec((None, t, LANES), lambda h, i: (2 * A_HEADS + h, i, 0)),
            vec(A_HEAD_DIM), vec(A_HEAD_DIM), vec(A_HEAD_DIM), vec(A_HEAD_DIM),
            vec(A_V_DIM),
        ],
        out_specs=pl.BlockSpec((None, t, LANES), lambda h, i: (h, i, 0)),
        out_shape=jax.ShapeDtypeStruct((A_HEADS, n, LANES), BF16),
        scratch_shapes=[
            pltpu.VMEM((LANES, t), BF16),
            pltpu.VMEM((2 * t - qb, qb), F32),
            pltpu.VMEM((inflight, t, 2 * qb), BF16),
            pltpu.VMEM((nblk, LANES, 2 * qb), F32),
            pltpu.VMEM((nblk, 2 * qb), F32),
        ],
        compiler_params=pltpu.CompilerParams(
            dimension_semantics=("arbitrary", "arbitrary"), vmem_limit_bytes=VMEM_LIMIT_BYTES),
        name="attn_a",
    )(ctl, qkv_nat, qkv_nat, qkv_nat, lq1, lk1, lq2, lk2, subln)


def _attn_b_kernel(ctl_ref, q0_ref, k0_ref, v0_ref, q1_ref, k1_ref, v1_ref, q2_ref, k2_ref, v2_ref,
                   o_ref, acc_ref, den_ref, kt_ref, *, t, qb):
    h = pl.program_id(1)
    n_ctl = len(B_GROUPS) * B_HEADS
    shift = ctl_ref[n_ctl]
    fast = ctl_ref[n_ctl + 1] > 0.5
    refs = ((q0_ref, k0_ref, v0_ref), (q1_ref, k1_ref, v1_ref), (q2_ref, k2_ref, v2_ref))

    def blocks(g):
        window, dil = B_GROUPS[g]
        n_side = window // (2 * dil)
        sub = t // dil
        wk = min(2 * qb, sub)
        for r in range(dil):
            for j in range(sub // qb):
                i0 = j * qb
                ws = min(max(i0 - n_side, 0), sub - wk)
                yield r, i0, r * sub + i0, r * sub + ws, wk, i0 - ws, n_side

    def bias_table(g, wk, offset, n_side, sub_shift):
        slope2 = ctl_ref[g * B_HEADS + h] * (float(B_GROUPS[g][1]) * LOG2E)
        ql = lax.broadcasted_iota(jnp.int32, (qb, wk), 0)
        kl = lax.broadcasted_iota(jnp.int32, (qb, wk), 1)
        dist = jnp.abs(kl - ql - offset)
        return jnp.where(dist <= n_side, -slope2 * dist.astype(F32) - sub_shift, NEG_BIG)

    def store_rows(ref, g, r, i0, val):
        dil = B_GROUPS[g][1]
        if dil == 1:
            ref[g, i0:i0 + qb, :] = val
        else:
            ref[g, pl.ds(r + dil * i0, qb, stride=dil), :] = val

    @pl.when(fast)
    def _():
        half = qb // 2
        for g in range(len(B_GROUPS)):
            q_ref, k_ref, v_ref = refs[g]
            kt_ref[0] = k_ref[...].T
            kt_ref[1] = jnp.concatenate([k_ref[half:, :], k_ref[:half, :]], axis=0).T
            tables = {}
            for r, i0, qrow, krow, wk, offset, n_side in blocks(g):
                if offset not in tables:
                    tables[offset] = bias_table(g, wk, offset, n_side, shift)
                if krow % qb == 0:
                    kwt = kt_ref[0, :, krow:krow + wk]
                else:
                    kwt = kt_ref[1, :, krow - half:krow - half + wk]
                s = _dot(q_ref[qrow:qrow + qb, :], kwt)
                e = jnp.exp2(s + tables[offset])
                den = jnp.sum(e, axis=-1, keepdims=True)
                store_rows(acc_ref, g, r, i0, _dot(e.astype(BF16), v_ref[krow:krow + wk, :]))
                store_rows(den_ref, g, r, i0, jnp.broadcast_to(den, (qb, LANES)))
        o_ref[...] = ((acc_ref[0] + acc_ref[1] + acc_ref[2])
                      / (den_ref[0] + den_ref[1] + den_ref[2])).astype(BF16)

    @pl.when(jnp.logical_not(fast))
    def _():
        for g in range(len(B_GROUPS)):
            q_ref, k_ref, v_ref = refs[g]
            tables = {}
            for r, i0, qrow, krow, wk, offset, n_side in blocks(g):
                if offset not in tables:
                    tables[offset] = bias_table(g, wk, offset, n_side, 0.0)
                s = _dot_nt(q_ref[qrow:qrow + qb, :], k_ref[krow:krow + wk, :]) + tables[offset]
                m = jnp.max(s, axis=-1, keepdims=True)
                e = jnp.exp2(s - m)
                l = jnp.sum(e, axis=-1, keepdims=True)
                o = _dot(e.astype(BF16), v_ref[krow:krow + wk, :]) * (1.0 / l)
                store_rows(acc_ref, g, r, i0, o)
                store_rows(den_ref, g, r, i0, jnp.broadcast_to(m + jnp.log2(l), (qb, LANES)))
        l0, l1, l2 = den_ref[0], den_ref[1], den_ref[2]
        m = jnp.maximum(jnp.maximum(l0, l1), l2)
        e0, e1, e2 = jnp.exp2(l0 - m), jnp.exp2(l1 - m), jnp.exp2(l2 - m)
        o_ref[...] = ((e0 * acc_ref[0] + e1 * acc_ref[1] + e2 * acc_ref[2]) / (e0 + e1 + e2)).astype(BF16)


def _attn_b(qkv_nat, qkv_g1, qkv_g2, ctl, b, t, qb=128):
    n = b * t
    blk = lambda off: pl.BlockSpec((None, t, LANES), lambda i, h: (off + h, i, 0))
    return pl.pallas_call(
        functools.partial(_attn_b_kernel, t=t, qb=qb),
        grid=(b, B_HEADS),
        in_specs=[
            pl.BlockSpec(memory_space=pltpu.SMEM),
            blk(3 * HEADS_PER_CHUNK), blk(4 * HEADS_PER_CHUNK), blk(5 * HEADS_PER_CHUNK),
            blk(0), blk(HEADS_PER_CHUNK), blk(2 * HEADS_PER_CHUNK),
            blk(0), blk(HEADS_PER_CHUNK), blk(2 * HEADS_PER_CHUNK),
        ],
        out_specs=pl.BlockSpec((None, t, LANES), lambda i, h: (h, i, 0)),
        out_shape=jax.ShapeDtypeStruct((B_HEADS, n, LANES), BF16),
        scratch_shapes=[
            pltpu.VMEM((len(B_GROUPS), t, LANES), F32),
            pltpu.VMEM((len(B_GROUPS), t, LANES), F32),
            pltpu.VMEM((2, LANES, t), BF16),
        ],
        compiler_params=pltpu.CompilerParams(
            dimension_semantics=("arbitrary", "arbitrary"), vmem_limit_bytes=VMEM_LIMIT_BYTES),
        name="attn_b",
    )(ctl, qkv_nat, qkv_nat, qkv_nat, qkv_g1, qkv_g1, qkv_g1, qkv_g2, qkv_g2, qkv_g2)


def _merge_kernel(x_ref, a_ref, b_ref, cq_ref, mem_ref, gm_ref, wkv_ref, gk_ref, g_ref, wg_ref, bg_ref, wb_ref,
                  wo_ref, o_ref, ckv_ref, *, tiles_per_batch):
    half = C_HEADS * C_HEAD_DIM

    @pl.when(pl.program_id(0) % tiles_per_batch == 0)
    def _():
        mn = _rms(mem_ref[0], gm_ref[...]).astype(BF16)
        kv = _dot(mn, wkv_ref[...])
        ckv_ref[:, :half] = _head_norm(kv[:, :half], gk_ref[...], "full").astype(BF16)
        ckv_ref[:, half:] = kv[:, half:].astype(BF16)

    x = x_ref[...]
    hb = _rms(x, g_ref[...]).astype(BF16)
    d = x.shape[-1]

    def gated(g, branch):
        gate = jax.nn.sigmoid(_dot(hb, wg_ref[:, g * d:(g + 1) * d]) + bg_ref[:, g * d:(g + 1) * d])
        return gate * _dot(branch, wb_ref[g])

    wide = lambda ref: jnp.concatenate([ref[hh] for hh in range(HEADS_PER_CHUNK)], axis=1)
    acc = gated(0, wide(a_ref)) + gated(1, wide(b_ref))
    heads = []
    for hh in range(C_HEADS):
        cols = slice(hh * C_HEAD_DIM, (hh + 1) * C_HEAD_DIM)
        s = _dot_nt(cq_ref[hh], ckv_ref[:, cols])
        e = jnp.exp2(s - jnp.max(s, axis=-1, keepdims=True))
        inv = 1.0 / jnp.sum(e, axis=-1, keepdims=True)
        heads.append(_dot(e.astype(BF16), ckv_ref[:, half + hh * C_HEAD_DIM:half + (hh + 1) * C_HEAD_DIM]) * inv)
    acc = acc + gated(2, jnp.concatenate(heads, axis=1).astype(BF16))
    o_ref[...] = x + _dot(acc.astype(BF16), wo_ref[...])


def _merge(x2, out_a, out_b, qkv_nat, mem, gm, w_kv, gk4, norm_g, w_gate, b_gate, w_branch, w_out, t, tm=512):
    n, d = x2.shape
    n_mem, kv_cols = mem.shape[1], w_kv.shape[1]
    row = lambda w: pl.BlockSpec((tm, w), lambda i: (i, 0))
    heads = lambda chunk: pl.BlockSpec((HEADS_PER_CHUNK, tm, LANES), lambda i: (chunk, i, 0))
    return pl.pallas_call(
        functools.partial(_merge_kernel, tiles_per_batch=t // tm),
        grid=(n // tm,),
        in_specs=[
            row(d), heads(0), heads(0), heads(NAT_HEADS // HEADS_PER_CHUNK - 1),
            pl.BlockSpec((1, n_mem, d), lambda i: (i // (t // tm), 0, 0)),
            _const_spec((1, d)), _const_spec((d, kv_cols)), _const_spec((1, kv_cols // 2)),
            _const_spec((1, d)),
            _const_spec((d, N_BRANCHES * d)),
            _const_spec((1, N_BRANCHES * d)),
            _const_spec((N_BRANCHES, BRANCH_WIDTH, d)),
            _const_spec((d, d)),
        ],
        out_specs=row(d),
        out_shape=jax.ShapeDtypeStruct((n, d), F32),
        scratch_shapes=[pltpu.VMEM((n_mem, kv_cols), BF16)],
        compiler_params=pltpu.CompilerParams(
            dimension_semantics=("arbitrary",), vmem_limit_bytes=VMEM_LIMIT_BYTES),
        name="merge",
    )(x2, out_a, out_b, qkv_nat, mem, gm, w_kv, gk4, norm_g, w_gate, b_gate, w_branch, w_out)


def _ffn_kernel(x_ref, g_ref, wg_ref, wu_ref, wd_ref, o_ref, *, chunks):
    x = x_ref[...]
    hb = _rms(x, g_ref[...]).astype(BF16)
    acc = x
    for c0, c1 in chunks:
        gt = _dot(hb, wg_ref[:, c0:c1])
        up = _dot(hb, wu_ref[:, c0:c1])
        acc = acc + _dot((jax.nn.silu(gt) * up).astype(BF16), wd_ref[c0:c1, :])
    o_ref[...] = acc


def _ffn(x2, norm_g, w_gate, w_up, w_down, tm=512, fc=768):
    n, d = x2.shape
    d_ff = w_gate.shape[1]
    chunks = tuple((c, min(c + fc, d_ff)) for c in range(0, d_ff, fc))
    row = pl.BlockSpec((tm, d), lambda i: (i, 0))
    return pl.pallas_call(
        functools.partial(_ffn_kernel, chunks=chunks),
        grid=(n // tm,),
        in_specs=[row, _const_spec((1, d)), _const_spec((d, d_ff)), _const_spec((d, d_ff)),
                  _const_spec((d_ff, d))],
        out_specs=row,
        out_shape=jax.ShapeDtypeStruct((n, d), F32),
        compiler_params=pltpu.CompilerParams(
            dimension_semantics=("arbitrary",), vmem_limit_bytes=VMEM_LIMIT_BYTES),
        name="ffn",
    )(x2, norm_g, w_gate, w_up, w_down)


def _score_ctl(slopes, head_dim, gq, gk):
    bound = math.sqrt(head_dim) * jnp.max(jnp.abs(gq)) * jnp.max(jnp.abs(gk))
    return jnp.concatenate([slopes.reshape(-1),
                            jnp.stack([bound * LOG2E, (bound <= MAX_SAFE_SCORE_BOUND).astype(F32)])])


def kernel(x, mem, norm_mix, w_in, w_gate, b_gate, a_q_norm, a_k_norm, a_lambda_q1, a_lambda_k1, a_lambda_q2,
           a_lambda_k2, a_subln, b_q_norm, b_k_norm, mem_norm, w_mem_kv, c_q_norm, c_k_norm, w_branch, w_out,
           norm_ffn, w_ffn_gate, w_ffn_up, w_ffn_down):
    b, t, d = x.shape
    n = b * t
    n_groups = len(B_GROUPS)
    slopes_a = jnp.exp2(-ALIBI_MAX_BIAS * jnp.arange(1, A_HEADS + 1, dtype=F32) / A_HEADS)
    nb = n_groups * B_HEADS
    slopes_b = jnp.exp2(-ALIBI_MAX_BIAS * jnp.arange(1, nb + 1, dtype=F32) / nb)

    l = 0
    bw = BRANCH_WIDTH
    row = lambda v: v.reshape(1, -1)
    tiled = lambda v: jnp.tile(v, bw // v.shape[0])
    ones = jnp.ones((bw,), F32)
    a_qs = A_HEAD_DIM ** -0.5 * LOG2E
    b_qs = B_HEAD_DIM ** -0.5 * LOG2E
    c_qs = C_HEAD_DIM ** -0.5 * LOG2E
    gain_nat = jnp.stack([tiled(a_q_norm[l]) * a_qs, tiled(a_k_norm[l]), ones, tiled(b_q_norm[l]) * b_qs,
                          tiled(b_k_norm[l]), ones, tiled(c_q_norm[l]) * c_qs])
    gain_grp = jnp.stack([tiled(b_q_norm[l]) * b_qs, tiled(b_k_norm[l]), ones])

    qkv_nat, qkv_g1, qkv_g2 = _project(x, row(norm_mix[l]), w_in[l], gain_nat, gain_grp)
    qkv_nat = qkv_nat.reshape(NAT_HEADS, n, LANES)
    qkv_g1 = qkv_g1.reshape(GRP_HEADS, n, LANES)
    qkv_g2 = qkv_g2.reshape(GRP_HEADS, n, LANES)

    out_a = _attn_a(qkv_nat, _score_ctl(slopes_a, A_HEAD_DIM, a_q_norm[l], a_k_norm[l]),
                    row(a_lambda_q1[l]), row(a_lambda_k1[l]), row(a_lambda_q2[l]), row(a_lambda_k2[l]),
                    row(a_subln[l]), b, t)
    out_b = _attn_b(qkv_nat, qkv_g1, qkv_g2, _score_ctl(slopes_b, B_HEAD_DIM, b_q_norm[l], b_k_norm[l]), b, t)
    x2 = x.reshape(n, d)
    x2 = _merge(x2, out_a, out_b, qkv_nat, mem, row(mem_norm[l]), w_mem_kv[l], row(tiled(c_k_norm[l])),
                row(norm_mix[l]), w_gate[l], row(b_gate[l]), w_branch[l], w_out[l], t)
    x2 = _ffn(x2, row(norm_ffn[l]), w_ffn_gate[l], w_ffn_up[l], w_ffn_down[l])
    return x2.reshape(b, t, d)
```

```python
import functools
import math

import jax
import jax.numpy as jnp
from jax import lax
from jax.experimental import pallas as pl
from jax.experimental.pallas import tpu as pltpu

F32 = jnp.float32
BF16 = jnp.bfloat16

D_MODEL = 1024
A_HEADS = 4
A_HEAD_DIM = 64
A_V_DIM = 2 * A_HEAD_DIM
B_GROUPS = ((128, 1), (512, 4), (2048, 16))
B_HEADS = 4
B_HEAD_DIM = 128
C_HEADS = 4
C_HEAD_DIM = 128
BRANCH_WIDTH = 512
N_BRANCHES = 3
EPS = 1e-6
ALIBI_MAX_BIAS = 8.0
LAMBDA_INIT = 0.8 - 0.6 * math.exp(-0.3 * 0)

LANES = 128
VMEM_LIMIT_BYTES = 56 * 1024 * 1024

NAT_COLS = 7 * BRANCH_WIDTH
GRP_COLS = 3 * BRANCH_WIDTH
HEADS_PER_CHUNK = BRANCH_WIDTH // LANES
NAT_HEADS = NAT_COLS // LANES
GRP_HEADS = GRP_COLS // LANES
NAT_NORMS = ("half", "half", "none", "full", "full", "none", "full")
GRP_COL_OFFS = (3 * BRANCH_WIDTH, 6 * BRANCH_WIDTH, 9 * BRANCH_WIDTH)
NAT_COL_OFFS = (0, BRANCH_WIDTH, 2 * BRANCH_WIDTH) + GRP_COL_OFFS + (12 * BRANCH_WIDTH,)
GRP_NORMS = ("full", "full", "none")
NEG_BIG = -1e30
LOG2E = 1.4426950408889634
MAX_SAFE_SCORE_BOUND = 40.0


def _rms(x, gain):
    return x * lax.rsqrt(jnp.mean(x * x, axis=-1, keepdims=True) + EPS) * gain


def _dot(a, b):
    return jnp.dot(a, b, preferred_element_type=F32)


def _dot_nt(a, b):
    return lax.dot_general(a, b, (((1,), (1,)), ((), ())), preferred_element_type=F32)


def _const_spec(shape):
    nd = len(shape)
    return pl.BlockSpec(shape, lambda *_: (0,) * nd, pipeline_mode=pl.Buffered(1))


def _head_norm(y, gain, kind):
    if kind == "none":
        return y
    lo_mask = lax.broadcasted_iota(jnp.int32, (1, LANES), 1) < A_HEAD_DIM
    cols = []
    for c in range(0, y.shape[1], LANES):
        z = y[:, c:c + LANES]
        sq = z * z
        s_all = jnp.sum(sq, axis=-1, keepdims=True)
        if kind == "full":
            ms = s_all * (1.0 / LANES)
        else:
            s_lo = jnp.sum(jnp.where(lo_mask, sq, 0.0), axis=-1, keepdims=True)
            ms = jnp.where(lo_mask, s_lo, s_all - s_lo) * (1.0 / A_HEAD_DIM)
        cols.append(z * lax.rsqrt(ms + EPS))
    return jnp.concatenate(cols, axis=1) * gain


def _proj_kernel(x_ref, g_ref, w_ref, gn_ref, gg_ref, on_ref, o1_ref, o2_ref, slab_ref, p4_ref, hp_ref, *, tm):
    h = _rms(x_ref[0], g_ref[...])
    hb = h.astype(BF16)
    bw = BRANCH_WIDTH
    for ci, kind in enumerate(NAT_NORMS):
        c0 = NAT_COL_OFFS[ci]
        res = _head_norm(_dot(hb, w_ref[:, c0:c0 + bw]), gn_ref[ci:ci + 1, :], kind).astype(BF16)
        for hh in range(HEADS_PER_CHUNK):
            on_ref[ci * HEADS_PER_CHUNK + hh, 0] = res[:, hh * LANES:(hh + 1) * LANES]
    n_slabs = D_MODEL // LANES
    for s in range(n_slabs):
        slab_ref[s] = h[:, s * LANES:(s + 1) * LANES]
    for g, dil, o_ref in ((1, 4, o1_ref), (2, 16, o2_ref)):
        n = tm // dil
        if dil == 4:
            for r in range(dil):
                for s in range(n_slabs):
                    rows = slab_ref[s, pl.ds(r, n, stride=dil), :]
                    p4_ref[s, r * n:(r + 1) * n, :] = rows
                    hp_ref[r * n:(r + 1) * n, s * LANES:(s + 1) * LANES] = rows.astype(BF16)
        else:
            n4 = tm // 4
            for r4 in range(4):
                for q in range(4):
                    r = r4 + 4 * q
                    for s in range(n_slabs):
                        hp_ref[r * n:(r + 1) * n, s * LANES:(s + 1) * LANES] = (
                            p4_ref[s, pl.ds(r4 * n4 + q, n, stride=4), :].astype(BF16))
        hp = hp_ref[...]
        for ci, kind in enumerate(GRP_NORMS):
            c0 = GRP_COL_OFFS[ci] + g * bw
            res = _head_norm(_dot(hp, w_ref[:, c0:c0 + bw]), gg_ref[ci:ci + 1, :], kind).astype(BF16)
            for hh in range(HEADS_PER_CHUNK):
                for r in range(dil):
                    o_ref[ci * HEADS_PER_CHUNK + hh, 0, r] = res[r * n:(r + 1) * n, hh * LANES:(hh + 1) * LANES]


def _project(x, norm_g, w_in, gain_nat, gain_grp, tm=512):
    b, t, d = x.shape
    grid = (b, t // tm)
    return pl.pallas_call(
        functools.partial(_proj_kernel, tm=tm),
        grid=grid,
        in_specs=[
            pl.BlockSpec((1, tm, d), lambda i, j: (i, j, 0)),
            _const_spec((1, d)),
            _const_spec(w_in.shape),
            _const_spec(gain_nat.shape),
            _const_spec(gain_grp.shape),
        ],
        out_specs=[
            pl.BlockSpec((NAT_HEADS, 1, tm, LANES), lambda i, j: (0, i, j, 0)),
            pl.BlockSpec((GRP_HEADS, 1, 4, tm // 4, LANES), lambda i, j: (0, i, 0, j, 0)),
            pl.BlockSpec((GRP_HEADS, 1, 16, tm // 16, LANES), lambda i, j: (0, i, 0, j, 0)),
        ],
        out_shape=[
            jax.ShapeDtypeStruct((NAT_HEADS, b, t, LANES), BF16),
            jax.ShapeDtypeStruct((GRP_HEADS, b, 4, t // 4, LANES), BF16),
            jax.ShapeDtypeStruct((GRP_HEADS, b, 16, t // 16, LANES), BF16),
        ],
        scratch_shapes=[
            pltpu.VMEM((d // LANES, tm, LANES), F32),
            pltpu.VMEM((d // LANES, tm, LANES), F32),
            pltpu.VMEM((tm, d), BF16),
        ],
        compiler_params=pltpu.CompilerParams(
            dimension_semantics=("arbitrary", "arbitrary"), vmem_limit_bytes=VMEM_LIMIT_BYTES),
        name="proj",
    )(x, norm_g, w_in, gain_nat, gain_grp)


def _attn_a_kernel(ctl_ref, q_ref, k_ref, v_ref, lq1_ref, lk1_ref, lq2_ref, lk2_ref,
                   sub_ref, o_ref, vt_ref, tab_ref, et_ref, ot_ref, l_ref, *, t, qb, kc, inflight):
    h = pl.program_id(0)
    nblk = t // qb
    slope2 = ctl_ref[h] * LOG2E
    shift = ctl_ref[A_HEADS]
    fast = ctl_ref[A_HEADS + 1] > 0.5
    lo_mask = lax.broadcasted_iota(jnp.int32, (1, LANES), 1) < A_HEAD_DIM

    lam = (jnp.exp(jnp.sum(lq1_ref[...] * lk1_ref[...], keepdims=True))
           - jnp.exp(jnp.sum(lq2_ref[...] * lk2_ref[...], keepdims=True)) + LAMBDA_INIT)

    vt_ref[...] = v_ref[...].T

    @pl.when(pl.program_id(1) == 0)
    def _():
        cc = lax.broadcasted_iota(jnp.int32, (2 * t - qb, qb), 0)
        il = lax.broadcasted_iota(jnp.int32, (2 * t - qb, qb), 1)
        tab_ref[...] = (-slope2 * jnp.abs(cc - (t - qb) - il).astype(F32)
                        - jnp.where(fast, shift, 0.0))

    def fold8(x, op):
        return op(x.reshape(x.shape[0] // 8, 8, x.shape[1]), axis=0)

    def block(j, slot, exact_max):
        q = q_ref[pl.ds(pl.multiple_of(j * qb, qb), qb), :]
        zero = jnp.zeros_like(q)
        q2 = jnp.concatenate([jnp.where(lo_mask, q, zero), jnp.where(lo_mask, zero, q)], axis=0)
        off = pl.multiple_of(t - qb - j * qb, qb)

        def scores(c):
            s = _dot_nt(k_ref[c * kc:(c + 1) * kc, :], q2)
            bias = tab_ref[pl.ds(off + c * kc, kc), :]
            return s + jnp.concatenate([bias, bias], axis=1)

        m = None
        if exact_max:
            for c in range(t // kc):
                cm = fold8(scores(c), jnp.max)
                m = cm if m is None else jnp.maximum(m, cm)
            m = jnp.max(m, axis=0, keepdims=True)
        acc = None
        for c in range(t // kc):
            s = scores(c)
            e = jnp.exp2(s - m if exact_max else s)
            et_ref[slot, c * kc:(c + 1) * kc, :] = e.astype(BF16)
            part = fold8(e, jnp.sum)
            acc = part if acc is None else acc + part
        l_ref[pl.ds(j, 1), :] = jnp.sum(acc, axis=0, keepdims=True)
        ot_ref[j] = _dot(vt_ref[...], et_ref[slot])

    def finish(j):
        ot = ot_ref[j]
        inv = 1.0 / l_ref[pl.ds(j, 1), :]
        o = ot[:, :qb] * inv[:, :qb] - ot[:, qb:] * (lam * inv[:, qb:])
        o = _rms(o.T, sub_ref[...]) * (1.0 - LAMBDA_INIT)
        o_ref[pl.ds(pl.multiple_of(j * qb, qb), qb), :] = o.astype(BF16)

    def run(exact_max):
        for s in range(inflight):
            block(s, s, exact_max)

        def step(i, carry):
            for s in range(inflight):
                finish(inflight * (i - 1) + s)
            for s in range(inflight):
                block(inflight * i + s, s, exact_max)
            return carry
        lax.fori_loop(1, nblk // inflight, step, 0)
        for s in range(inflight):
            finish(nblk - inflight + s)

    @pl.when(fast)
    def _():
        run(False)

    @pl.when(jnp.logical_not(fast))
    def _():
        run(True)


def _attn_a(qkv_nat, ctl, lq1, lk1, lq2, lk2, subln, b, t, qb=128, kc=512, inflight=8):
    n = b * t
    nblk = t // qb
    vec = lambda w: _const_spec((1, w))
    return pl.pallas_call(
        functools.partial(_attn_a_kernel, t=t, qb=qb, kc=kc, inflight=inflight),
        grid=(A_HEADS, b),
        in_specs=[
            pl.BlockSpec(memory_space=pltpu.SMEM),
            pl.BlockSpec((None, t, LANES), lambda h, i: (h, i, 0)),
            pl.BlockSpec((None, t, LANES), lambda h, i: (A_HEADS + h, i, 0)),
            pl.BlockSpec((None, t, LANES), lambda h, i: (2 * A_HEADS + h, i, 0)),
            vec(A_HEAD_DIM), vec(A_HEAD_DIM), vec(A_HEAD_DIM), vec(A_HEAD_DIM),
            vec(A_V_DIM),
        ],
        out_specs=pl.BlockSpec((None, t, LANES), lambda h, i: (h, i, 0)),
        out_shape=jax.ShapeDtypeStruct((A_HEADS, n, LANES), BF16),
        scratch_shapes=[
            pltpu.VMEM((LANES, t), BF16),
            pltpu.VMEM((2 * t - qb, qb), F32),
            pltpu.VMEM((inflight, t, 2 * qb), BF16),
            pltpu.VMEM((nblk, LANES, 2 * qb), F32),
            pltpu.VMEM((nblk, 2 * qb), F32),
        ],
        compiler_params=pltpu.CompilerParams(
            dimension_semantics=("arbitrary", "arbitrary"), vmem_limit_bytes=VMEM_LIMIT_BYTES),
        name="attn_a",
    )(ctl, qkv_nat, qkv_nat, qkv_nat, lq1, lk1, lq2, lk2, subln)


def _attn_b_kernel(ctl_ref, q0_ref, k0_ref, v0_ref, q1_ref, k1_ref, v1_ref, q2_ref, k2_ref, v2_ref,
                   o_ref, acc_ref, den_ref, kt_ref, *, t, qb):
    h = pl.program_id(1)
    n_ctl = len(B_GROUPS) * B_HEADS
    shift = ctl_ref[n_ctl]
    fast = ctl_ref[n_ctl + 1] > 0.5
    refs = ((q0_ref, k0_ref, v0_ref), (q1_ref, k1_ref, v1_ref), (q2_ref, k2_ref, v2_ref))

    def blocks(g):
        window, dil = B_GROUPS[g]
        n_side = window // (2 * dil)
        sub = t // dil
        wk = min(2 * qb, sub)
        for r in range(dil):
            for j in range(sub // qb):
                i0 = j * qb
                ws = min(max(i0 - n_side, 0), sub - wk)
                yield r, i0, r * sub + i0, r * sub + ws, wk, i0 - ws, n_side

    def bias_table(g, wk, offset, n_side, sub_shift):
        slope2 = ctl_ref[g * B_HEADS + h] * (float(B_GROUPS[g][1]) * LOG2E)
        ql = lax.broadcasted_iota(jnp.int32, (qb, wk), 0)
        kl = lax.broadcasted_iota(jnp.int32, (qb, wk), 1)
        dist = jnp.abs(kl - ql - offset)
        return jnp.where(dist <= n_side, -slope2 * dist.astype(F32) - sub_shift, NEG_BIG)

    def store_rows(ref, g, r, i0, val):
        dil = B_GROUPS[g][1]
        if dil == 1:
            ref[g, i0:i0 + qb, :] = val
        else:
            ref[g, pl.ds(r + dil * i0, qb, stride=dil), :] = val

    @pl.when(fast)
    def _():
        half = qb // 2
        for g in range(len(B_GROUPS)):
            q_ref, k_ref, v_ref = refs[g]
            kt_ref[0] = k_ref[...].T
            kt_ref[1] = jnp.concatenate([k_ref[half:, :], k_ref[:half, :]], axis=0).T
            tables = {}
            for r, i0, qrow, krow, wk, offset, n_side in blocks(g):
                if offset not in tables:
                    tables[offset] = bias_table(g, wk, offset, n_side, shift)
                if krow % qb == 0:
                    kwt = kt_ref[0, :, krow:krow + wk]
                else:
                    kwt = kt_ref[1, :, krow - half:krow - half + wk]
                s = _dot(q_ref[qrow:qrow + qb, :], kwt)
                e = jnp.exp2(s + tables[offset])
                den = jnp.sum(e, axis=-1, keepdims=True)
                store_rows(acc_ref, g, r, i0, _dot(e.astype(BF16), v_ref[krow:krow + wk, :]))
                store_rows(den_ref, g, r, i0, jnp.broadcast_to(den, (qb, LANES)))
        o_ref[...] = ((acc_ref[0] + acc_ref[1] + acc_ref[2])
                      / (den_ref[0] + den_ref[1] + den_ref[2])).astype(BF16)

    @pl.when(jnp.logical_not(fast))
    def _():
        for g in range(len(B_GROUPS)):
            q_ref, k_ref, v_ref = refs[g]
            tables = {}
            for r, i0, qrow, krow, wk, offset, n_side in blocks(g):
                if offset not in tables:
                    tables[offset] = bias_table(g, wk, offset, n_side, 0.0)
                s = _dot_nt(q_ref[qrow:qrow + qb, :], k_ref[krow:krow + wk, :]) + tables[offset]
                m = jnp.max(s, axis=-1, keepdims=True)
                e = jnp.exp2(s - m)
                l = jnp.sum(e, axis=-1, keepdims=True)
                o = _dot(e.astype(BF16), v_ref[krow:krow + wk, :]) * (1.0 / l)
                store_rows(acc_ref, g, r, i0, o)
                store_rows(den_ref, g, r, i0, jnp.broadcast_to(m + jnp.log2(l), (qb, LANES)))
        l0, l1, l2 = den_ref[0], den_ref[1], den_ref[2]
        m = jnp.maximum(jnp.maximum(l0, l1), l2)
        e0, e1, e2 = jnp.exp2(l0 - m), jnp.exp2(l1 - m), jnp.exp2(l2 - m)
        o_ref[...] = ((e0 * acc_ref[0] + e1 * acc_ref[1] + e2 * acc_ref[2]) / (e0 + e1 + e2)).astype(BF16)


def _attn_b(qkv_nat, qkv_g1, qkv_g2, ctl, b, t, qb=128):
    n = b * t
    blk = lambda off: pl.BlockSpec((None, t, LANES), lambda i, h: (off + h, i, 0))
    return pl.pallas_call(
        functools.partial(_attn_b_kernel, t=t, qb=qb),
        grid=(b, B_HEADS),
        in_specs=[
            pl.BlockSpec(memory_space=pltpu.SMEM),
            blk(3 * HEADS_PER_CHUNK), blk(4 * HEADS_PER_CHUNK), blk(5 * HEADS_PER_CHUNK),
            blk(0), blk(HEADS_PER_CHUNK), blk(2 * HEADS_PER_CHUNK),
            blk(0), blk(HEADS_PER_CHUNK), blk(2 * HEADS_PER_CHUNK),
        ],
        out_specs=pl.BlockSpec((None, t, LANES), lambda i, h: (h, i, 0)),
        out_shape=jax.ShapeDtypeStruct((B_HEADS, n, LANES), BF16),
        scratch_shapes=[
            pltpu.VMEM((len(B_GROUPS), t, LANES), F32),
            pltpu.VMEM((len(B_GROUPS), t, LANES), F32),
            pltpu.VMEM((2, LANES, t), BF16),
        ],
        compiler_params=pltpu.CompilerParams(
            dimension_semantics=("arbitrary", "arbitrary"), vmem_limit_bytes=VMEM_LIMIT_BYTES),
        name="attn_b",
    )(ctl, qkv_nat, qkv_nat, qkv_nat, qkv_g1, qkv_g1, qkv_g1, qkv_g2, qkv_g2, qkv_g2)


def _merge_kernel(x_ref, a_ref, b_ref, cq_ref, mem_ref, gm_ref, wkv_ref, gk_ref, g_ref, wg_ref, bg_ref, wb_ref,
                  wo_ref, o_ref, ckt_ref, cv_ref, *, tiles_per_batch):
    half = C_HEADS * C_HEAD_DIM

    @pl.when(pl.program_id(0) % tiles_per_batch == 0)
    def _():
        mn = _rms(mem_ref[0], gm_ref[...]).astype(BF16)
        kv = _dot(mn, wkv_ref[...])
        ckt_ref[...] = _head_norm(kv[:, :half], gk_ref[...], "full").astype(BF16).T
        cv_ref[...] = kv[:, half:].astype(BF16)

    x = x_ref[...]
    hb = _rms(x, g_ref[...]).astype(BF16)
    d = x.shape[-1]

    def gated(g, branch):
        gate = jax.nn.sigmoid(_dot(hb, wg_ref[:, g * d:(g + 1) * d]) + bg_ref[:, g * d:(g + 1) * d])
        return gate * _dot(branch, wb_ref[g])

    wide = lambda ref: jnp.concatenate([ref[hh] for hh in range(HEADS_PER_CHUNK)], axis=1)
    acc = gated(0, wide(a_ref)) + gated(1, wide(b_ref))
    heads = []
    for hh in range(C_HEADS):
        cols = slice(hh * C_HEAD_DIM, (hh + 1) * C_HEAD_DIM)
        s = _dot(cq_ref[hh], ckt_ref[cols, :])
        e = jnp.exp2(s - jnp.max(s, axis=-1, keepdims=True))
        inv = 1.0 / jnp.sum(e, axis=-1, keepdims=True)
        heads.append(_dot(e.astype(BF16), cv_ref[:, cols]) * inv)
    acc = acc + gated(2, jnp.concatenate(heads, axis=1).astype(BF16))
    o_ref[...] = x + _dot(acc.astype(BF16), wo_ref[...])


def _merge(x2, out_a, out_b, qkv_nat, mem, gm, w_kv, gk4, norm_g, w_gate, b_gate, w_branch, w_out, t, tm=512):
    n, d = x2.shape
    n_mem, kv_cols = mem.shape[1], w_kv.shape[1]
    row = lambda w: pl.BlockSpec((tm, w), lambda i: (i, 0))
    heads = lambda chunk: pl.BlockSpec((HEADS_PER_CHUNK, tm, LANES), lambda i: (chunk, i, 0))
    return pl.pallas_call(
        functools.partial(_merge_kernel, tiles_per_batch=t // tm),
        grid=(n // tm,),
        in_specs=[
            row(d), heads(0), heads(0), heads(NAT_HEADS // HEADS_PER_CHUNK - 1),
            pl.BlockSpec((1, n_mem, d), lambda i: (i // (t // tm), 0, 0)),
            _const_spec((1, d)), _const_spec((d, kv_cols)), _const_spec((1, kv_cols // 2)),
            _const_spec((1, d)),
            _const_spec((d, N_BRANCHES * d)),
            _const_spec((1, N_BRANCHES * d)),
            _const_spec((N_BRANCHES, BRANCH_WIDTH, d)),
            _const_spec((d, d)),
        ],
        out_specs=row(d),
        out_shape=jax.ShapeDtypeStruct((n, d), F32),
        scratch_shapes=[pltpu.VMEM((kv_cols // 2, n_mem), BF16), pltpu.VMEM((n_mem, kv_cols // 2), BF16)],
        compiler_params=pltpu.CompilerParams(
            dimension_semantics=("arbitrary",), vmem_limit_bytes=VMEM_LIMIT_BYTES),
        name="merge",
    )(x2, out_a, out_b, qkv_nat, mem, gm, w_kv, gk4, norm_g, w_gate, b_gate, w_branch, w_out)


def _ffn_kernel(x_ref, g_ref, wg_ref, wu_ref, wd_ref, o_ref, *, chunks):
    x = x_ref[...]
    hb = _rms(x, g_ref[...]).astype(BF16)
    acc = x
    for c0, c1 in chunks:
        gt = _dot(hb, wg_ref[:, c0:c1])
        up = _dot(hb, wu_ref[:, c0:c1])
        acc = acc + _dot((jax.nn.silu(gt) * up).astype(BF16), wd_ref[c0:c1, :])
    o_ref[...] = acc


def _ffn(x2, norm_g, w_gate, w_up, w_down, tm=512, fc=768):
    n, d = x2.shape
    d_ff = w_gate.shape[1]
    chunks = tuple((c, min(c + fc, d_ff)) for c in range(0, d_ff, fc))
    row = pl.BlockSpec((tm, d), lambda i: (i, 0))
    return pl.pallas_call(
        functools.partial(_ffn_kernel, chunks=chunks),
        grid=(n // tm,),
        in_specs=[row, _const_spec((1, d)), _const_spec((d, d_ff)), _const_spec((d, d_ff)),
                  _const_spec((d_ff, d))],
        out_specs=row,
        out_shape=jax.ShapeDtypeStruct((n, d), F32),
        compiler_params=pltpu.CompilerParams(
            dimension_semantics=("arbitrary",), vmem_limit_bytes=VMEM_LIMIT_BYTES),
        name="ffn",
    )(x2, norm_g, w_gate, w_up, w_down)


def _score_ctl(slopes, head_dim, gq, gk):
    bound = math.sqrt(head_dim) * jnp.max(jnp.abs(gq)) * jnp.max(jnp.abs(gk))
    return jnp.concatenate([slopes.reshape(-1),
                            jnp.stack([bound * LOG2E, (bound <= MAX_SAFE_SCORE_BOUND).astype(F32)])])


def kernel(x, mem, norm_mix, w_in, w_gate, b_gate, a_q_norm, a_k_norm, a_lambda_q1, a_lambda_k1, a_lambda_q2,
           a_lambda_k2, a_subln, b_q_norm, b_k_norm, mem_norm, w_mem_kv, c_q_norm, c_k_norm, w_branch, w_out,
           norm_ffn, w_ffn_gate, w_ffn_up, w_ffn_down):
    b, t, d = x.shape
    n = b * t
    n_groups = len(B_GROUPS)
    slopes_a = jnp.exp2(-ALIBI_MAX_BIAS * jnp.arange(1, A_HEADS + 1, dtype=F32) / A_HEADS)
    nb = n_groups * B_HEADS
    slopes_b = jnp.exp2(-ALIBI_MAX_BIAS * jnp.arange(1, nb + 1, dtype=F32) / nb)

    l = 0
    bw = BRANCH_WIDTH
    row = lambda v: v.reshape(1, -1)
    tiled = lambda v: jnp.tile(v, bw // v.shape[0])
    ones = jnp.ones((bw,), F32)
    a_qs = A_HEAD_DIM ** -0.5 * LOG2E
    b_qs = B_HEAD_DIM ** -0.5 * LOG2E
    c_qs = C_HEAD_DIM ** -0.5 * LOG2E
    gain_nat = jnp.stack([tiled(a_q_norm[l]) * a_qs, tiled(a_k_norm[l]), ones, tiled(b_q_norm[l]) * b_qs,
                          tiled(b_k_norm[l]), ones, tiled(c_q_norm[l]) * c_qs])
    gain_grp = jnp.stack([tiled(b_q_norm[l]) * b_qs, tiled(b_k_norm[l]), ones])

    qkv_nat, qkv_g1, qkv_g2 = _project(x, row(norm_mix[l]), w_in[l], gain_nat, gain_grp)
    qkv_nat = qkv_nat.reshape(NAT_HEADS, n, LANES)
    qkv_g1 = qkv_g1.reshape(GRP_HEADS, n, LANES)
    qkv_g2 = qkv_g2.reshape(GRP_HEADS, n, LANES)

    out_a = _attn_a(qkv_nat, _score_ctl(slopes_a, A_HEAD_DIM, a_q_norm[l], a_k_norm[l]),
                    row(a_lambda_q1[l]), row(a_lambda_k1[l]), row(a_lambda_q2[l]), row(a_lambda_k2[l]),
                    row(a_subln[l]), b, t)
    out_b = _attn_b(qkv_nat, qkv_g1, qkv_g2, _score_ctl(slopes_b, B_HEAD_DIM, b_q_norm[l], b_k_norm[l]), b, t)
    x2 = x.reshape(n, d)
    x2 = _merge(x2, out_a, out_b, qkv_nat, mem, row(mem_norm[l]), w_mem_kv[l], row(tiled(c_k_norm[l])),
                row(norm_mix[l]), w_gate[l], row(b_gate[l]), w_branch[l], w_out[l], t)
    x2 = _ffn(x2, row(norm_ffn[l]), w_ffn_gate[l], w_ffn_up[l], w_ffn_down[l])
    return x2.reshape(b, t, d)
```

```python
import functools
import math

import jax
import jax.numpy as jnp
from jax import lax
from jax.experimental import pallas as pl
from jax.experimental.pallas import tpu as pltpu

F32 = jnp.float32
BF16 = jnp.bfloat16

D_MODEL = 1024
A_HEADS = 4
A_HEAD_DIM = 64
A_V_DIM = 2 * A_HEAD_DIM
B_GROUPS = ((128, 1), (512, 4), (2048, 16))
B_HEADS = 4
B_HEAD_DIM = 128
C_HEADS = 4
C_HEAD_DIM = 128
BRANCH_WIDTH = 512
N_BRANCHES = 3
EPS = 1e-6
ALIBI_MAX_BIAS = 8.0
LAMBDA_INIT = 0.8 - 0.6 * math.exp(-0.3 * 0)

LANES = 128
VMEM_LIMIT_BYTES = 56 * 1024 * 1024

NAT_COLS = 7 * BRANCH_WIDTH
GRP_COLS = 3 * BRANCH_WIDTH
HEADS_PER_CHUNK = BRANCH_WIDTH // LANES
NAT_HEADS = NAT_COLS // LANES
GRP_HEADS = GRP_COLS // LANES
NAT_NORMS = ("half", "half", "none", "full", "full", "none", "full")
GRP_COL_OFFS = (3 * BRANCH_WIDTH, 6 * BRANCH_WIDTH, 9 * BRANCH_WIDTH)
NAT_COL_OFFS = (0, BRANCH_WIDTH, 2 * BRANCH_WIDTH) + GRP_COL_OFFS + (12 * BRANCH_WIDTH,)
GRP_NORMS = ("full", "full", "none")
NEG_BIG = -1e30
LOG2E = 1.4426950408889634
MAX_SAFE_SCORE_BOUND = 40.0


def _rms(x, gain):
    return x * lax.rsqrt(jnp.mean(x * x, axis=-1, keepdims=True) + EPS) * gain


def _dot(a, b):
    return jnp.dot(a, b, preferred_element_type=F32)


def _dot_nt(a, b):
    return lax.dot_general(a, b, (((1,), (1,)), ((), ())), preferred_element_type=F32)


def _const_spec(shape):
    nd = len(shape)
    return pl.BlockSpec(shape, lambda *_: (0,) * nd, pipeline_mode=pl.Buffered(1))


def _head_norm(y, gain, kind):
    if kind == "none":
        return y
    lo_mask = lax.broadcasted_iota(jnp.int32, (1, LANES), 1) < A_HEAD_DIM
    cols = []
    for c in range(0, y.shape[1], LANES):
        z = y[:, c:c + LANES]
        sq = z * z
        s_all = jnp.sum(sq, axis=-1, keepdims=True)
        if kind == "full":
            ms = s_all * (1.0 / LANES)
        else:
            s_lo = jnp.sum(jnp.where(lo_mask, sq, 0.0), axis=-1, keepdims=True)
            ms = jnp.where(lo_mask, s_lo, s_all - s_lo) * (1.0 / A_HEAD_DIM)
        cols.append(z * lax.rsqrt(ms + EPS))
    return jnp.concatenate(cols, axis=1) * gain


def _proj_kernel(x_ref, g_ref, w_ref, gn_ref, gg_ref, on_ref, o1_ref, o2_ref, slab_ref, p4_ref, hp_ref, *, tm):
    h = _rms(x_ref[0], g_ref[...])
    hb = h.astype(BF16)
    bw = BRANCH_WIDTH
    for ci, kind in enumerate(NAT_NORMS):
        c0 = NAT_COL_OFFS[ci]
        res = _head_norm(_dot(hb, w_ref[:, c0:c0 + bw]), gn_ref[ci:ci + 1, :], kind).astype(BF16)
        for hh in range(HEADS_PER_CHUNK):
            on_ref[ci * HEADS_PER_CHUNK + hh, 0] = res[:, hh * LANES:(hh + 1) * LANES]
    n_slabs = D_MODEL // LANES
    for s in range(n_slabs):
        slab_ref[s] = h[:, s * LANES:(s + 1) * LANES]
    for g, dil, o_ref in ((1, 4, o1_ref), (2, 16, o2_ref)):
        n = tm // dil
        if dil == 4:
            for r in range(dil):
                for s in range(n_slabs):
                    rows = slab_ref[s, pl.ds(r, n, stride=dil), :]
                    p4_ref[s, r * n:(r + 1) * n, :] = rows
                    hp_ref[r * n:(r + 1) * n, s * LANES:(s + 1) * LANES] = rows.astype(BF16)
        else:
            n4 = tm // 4
            for r4 in range(4):
                for q in range(4):
                    r = r4 + 4 * q
                    for s in range(n_slabs):
                        hp_ref[r * n:(r + 1) * n, s * LANES:(s + 1) * LANES] = (
                            p4_ref[s, pl.ds(r4 * n4 + q, n, stride=4), :].astype(BF16))
        hp = hp_ref[...]
        for ci, kind in enumerate(GRP_NORMS):
            c0 = GRP_COL_OFFS[ci] + g * bw
            res = _head_norm(_dot(hp, w_ref[:, c0:c0 + bw]), gg_ref[ci:ci + 1, :], kind).astype(BF16)
            for hh in range(HEADS_PER_CHUNK):
                for r in range(dil):
                    o_ref[ci * HEADS_PER_CHUNK + hh, 0, r] = res[r * n:(r + 1) * n, hh * LANES:(hh + 1) * LANES]


def _project(x, norm_g, w_in, gain_nat, gain_grp, tm=512):
    b, t, d = x.shape
    grid = (b, t // tm)
    return pl.pallas_call(
        functools.partial(_proj_kernel, tm=tm),
        grid=grid,
        in_specs=[
            pl.BlockSpec((1, tm, d), lambda i, j: (i, j, 0)),
            _const_spec((1, d)),
            _const_spec(w_in.shape),
            _const_spec(gain_nat.shape),
            _const_spec(gain_grp.shape),
        ],
        out_specs=[
            pl.BlockSpec((NAT_HEADS, 1, tm, LANES), lambda i, j: (0, i, j, 0)),
            pl.BlockSpec((GRP_HEADS, 1, 4, tm // 4, LANES), lambda i, j: (0, i, 0, j, 0)),
            pl.BlockSpec((GRP_HEADS, 1, 16, tm // 16, LANES), lambda i, j: (0, i, 0, j, 0)),
        ],
        out_shape=[
            jax.ShapeDtypeStruct((NAT_HEADS, b, t, LANES), BF16),
            jax.ShapeDtypeStruct((GRP_HEADS, b, 4, t // 4, LANES), BF16),
            jax.ShapeDtypeStruct((GRP_HEADS, b, 16, t // 16, LANES), BF16),
        ],
        scratch_shapes=[
            pltpu.VMEM((d // LANES, tm, LANES), F32),
            pltpu.VMEM((d // LANES, tm, LANES), F32),
            pltpu.VMEM((tm, d), BF16),
        ],
        compiler_params=pltpu.CompilerParams(
            dimension_semantics=("arbitrary", "arbitrary"), vmem_limit_bytes=VMEM_LIMIT_BYTES),
        name="proj",
    )(x, norm_g, w_in, gain_nat, gain_grp)


def _attn_a_kernel(ctl_ref, q_ref, k_ref, v_ref, lq1_ref, lk1_ref, lq2_ref, lk2_ref,
                   sub_ref, o_ref, vt_ref, tab_ref, et_ref, ot_ref, l_ref, *, t, qb, kc, inflight):
    h = pl.program_id(0)
    nblk = t // qb
    slope2 = ctl_ref[h] * LOG2E
    shift = ctl_ref[A_HEADS]
    fast = ctl_ref[A_HEADS + 1] > 0.5
    lo_mask = lax.broadcasted_iota(jnp.int32, (1, LANES), 1) < A_HEAD_DIM

    lam = (jnp.exp(jnp.sum(lq1_ref[...] * lk1_ref[...], keepdims=True))
           - jnp.exp(jnp.sum(lq2_ref[...] * lk2_ref[...], keepdims=True)) + LAMBDA_INIT)

    vt_ref[...] = v_ref[...].T

    @pl.when(pl.program_id(1) == 0)
    def _():
        cc = lax.broadcasted_iota(jnp.int32, (2 * t - qb, qb), 0)
        il = lax.broadcasted_iota(jnp.int32, (2 * t - qb, qb), 1)
        tab_ref[...] = (-slope2 * jnp.abs(cc - (t - qb) - il).astype(F32)
                        - jnp.where(fast, shift, 0.0))

    def fold8(x, op):
        return op(x.reshape(x.shape[0] // 8, 8, x.shape[1]), axis=0)

    def block(j, slot, exact_max):
        q = q_ref[pl.ds(pl.multiple_of(j * qb, qb), qb), :]
        zero = jnp.zeros_like(q)
        q2 = jnp.concatenate([jnp.where(lo_mask, q, zero), jnp.where(lo_mask, zero, q)], axis=0)
        off = pl.multiple_of(t - qb - j * qb, qb)

        def scores(c):
            s = _dot_nt(k_ref[c * kc:(c + 1) * kc, :], q2)
            bias = tab_ref[pl.ds(off + c * kc, kc), :]
            return s + jnp.concatenate([bias, bias], axis=1)

        m = None
        if exact_max:
            for c in range(t // kc):
                cm = fold8(scores(c), jnp.max)
                m = cm if m is None else jnp.maximum(m, cm)
            m = jnp.max(m, axis=0, keepdims=True)
        acc = None
        for c in range(t // kc):
            s = scores(c)
            e = jnp.exp2(s - m if exact_max else s)
            et_ref[slot, c * kc:(c + 1) * kc, :] = e.astype(BF16)
            part = fold8(e, jnp.sum)
            acc = part if acc is None else acc + part
        l_ref[pl.ds(j, 1), :] = jnp.sum(acc, axis=0, keepdims=True)
        ot_ref[j] = _dot(vt_ref[...], et_ref[slot])

    def finish(j):
        ot = ot_ref[j]
        inv = 1.0 / l_ref[pl.ds(j, 1), :]
        o = ot[:, :qb] * inv[:, :qb] - ot[:, qb:] * (lam * inv[:, qb:])
        o = _rms(o.T, sub_ref[...]) * (1.0 - LAMBDA_INIT)
        o_ref[pl.ds(pl.multiple_of(j * qb, qb), qb), :] = o.astype(BF16)

    def run(exact_max):
        for s in range(inflight):
            block(s, s, exact_max)

        def step(i, carry):
            for s in range(inflight):
                finish(inflight * (i - 1) + s)
            for s in range(inflight):
                block(inflight * i + s, s, exact_max)
            return carry
        lax.fori_loop(1, nblk // inflight, step, 0)
        for s in range(inflight):
            finish(nblk - inflight + s)

    @pl.when(fast)
    def _():
        run(False)

    @pl.when(jnp.logical_not(fast))
    def _():
        run(True)


def _attn_a(qkv_nat, ctl, lq1, lk1, lq2, lk2, subln, b, t, qb=128, kc=512, inflight=8):
    n = b * t
    nblk = t // qb
    vec = lambda w: _const_spec((1, w))
    return pl.pallas_call(
        functools.partial(_attn_a_kernel, t=t, qb=qb, kc=kc, inflight=inflight),
        grid=(A_HEADS, b),
        in_specs=[
            pl.BlockSpec(memory_space=pltpu.SMEM),
            pl.BlockSpec((None, t, LANES), lambda h, i: (h, i, 0)),
            pl.BlockSpec((None, t, LANES), lambda h, i: (A_HEADS + h, i, 0)),
            pl.BlockSpec((None, t, LANES), lambda h, i: (2 * A_HEADS + h, i, 0)),
            vec(A_HEAD_DIM), vec(A_HEAD_DIM), vec(A_HEAD_DIM), vec(A_HEAD_DIM),
            vec(A_V_DIM),
        ],
        out_specs=pl.BlockSpec((None, t, LANES), lambda h, i: (h, i, 0)),
        out_shape=jax.ShapeDtypeStruct((A_HEADS, n, LANES), BF16),
        scratch_shapes=[
            pltpu.VMEM((LANES, t), BF16),
            pltpu.VMEM((2 * t - qb, qb), F32),
            pltpu.VMEM((inflight, t, 2 * qb), BF16),
            pltpu.VMEM((nblk, LANES, 2 * qb), F32),
            pltpu.VMEM((nblk, 2 * qb), F32),
        ],
        compiler_params=pltpu.CompilerParams(
            dimension_semantics=("arbitrary", "arbitrary"), vmem_limit_bytes=VMEM_LIMIT_BYTES),
        name="attn_a",
    )(ctl, qkv_nat, qkv_nat, qkv_nat, lq1, lk1, lq2, lk2, subln)


def _attn_b_kernel(ctl_ref, q0_ref, k0_ref, v0_ref, q1_ref, k1_ref, v1_ref, q2_ref, k2_ref, v2_ref,
                   o_ref, acc_ref, den_ref, kt_ref, *, t, qb):
    h = pl.program_id(1)
    n_ctl = len(B_GROUPS) * B_HEADS
    shift = ctl_ref[n_ctl]
    fast = ctl_ref[n_ctl + 1] > 0.5
    refs = ((q0_ref, k0_ref, v0_ref), (q1_ref, k1_ref, v1_ref), (q2_ref, k2_ref, v2_ref))

    def blocks(g):
        window, dil = B_GROUPS[g]
        n_side = window // (2 * dil)
        sub = t // dil
        wk = min(2 * qb, sub)
        for r in range(dil):
            for j in range(sub // qb):
                i0 = j * qb
                ws = min(max(i0 - n_side, 0), sub - wk)
                yield r, i0, r * sub + i0, r * sub + ws, wk, i0 - ws, n_side

    def bias_table(g, wk, offset, n_side, sub_shift):
        slope2 = ctl_ref[g * B_HEADS + h] * (float(B_GROUPS[g][1]) * LOG2E)
        ql = lax.broadcasted_iota(jnp.int32, (qb, wk), 0)
        kl = lax.broadcasted_iota(jnp.int32, (qb, wk), 1)
        dist = jnp.abs(kl - ql - offset)
        return jnp.where(dist <= n_side, -slope2 * dist.astype(F32) - sub_shift, NEG_BIG)

    def store_rows(ref, g, r, i0, val):
        dil = B_GROUPS[g][1]
        if dil == 1:
            ref[g, i0:i0 + qb, :] = val
        else:
            ref[g, pl.ds(r + dil * i0, qb, stride=dil), :] = val

    @pl.when(fast)
    def _():
        half = qb // 2
        for g in (1, 2, 0):
            q_ref, k_ref, v_ref = refs[g]
            kt_ref[0] = k_ref[...].T
            kt_ref[1] = jnp.concatenate([k_ref[half:, :], k_ref[:half, :]], axis=0).T
            tables = {}
            for r, i0, qrow, krow, wk, offset, n_side in blocks(g):
                if offset not in tables:
                    tables[offset] = bias_table(g, wk, offset, n_side, shift)
                if krow % qb == 0:
                    kwt = kt_ref[0, :, krow:krow + wk]
                else:
                    kwt = kt_ref[1, :, krow - half:krow - half + wk]
                s = _dot(q_ref[qrow:qrow + qb, :], kwt)
                e = jnp.exp2(s + tables[offset])
                den = jnp.sum(e, axis=-1, keepdims=True)
                acc = _dot(e.astype(BF16), v_ref[krow:krow + wk, :])
                if B_GROUPS[g][1] == 1:
                    rows = slice(i0, i0 + qb)
                    acc = acc + acc_ref[1, rows, :] + acc_ref[2, rows, :]
                    den = den + den_ref[1, rows, :] + den_ref[2, rows, :]
                    o_ref[rows, :] = (acc / den).astype(BF16)
                else:
                    store_rows(acc_ref, g, r, i0, acc)
                    store_rows(den_ref, g, r, i0, jnp.broadcast_to(den, (qb, LANES)))

    @pl.when(jnp.logical_not(fast))
    def _():
        for g in range(len(B_GROUPS)):
            q_ref, k_ref, v_ref = refs[g]
            tables = {}
            for r, i0, qrow, krow, wk, offset, n_side in blocks(g):
                if offset not in tables:
                    tables[offset] = bias_table(g, wk, offset, n_side, 0.0)
                s = _dot_nt(q_ref[qrow:qrow + qb, :], k_ref[krow:krow + wk, :]) + tables[offset]
                m = jnp.max(s, axis=-1, keepdims=True)
                e = jnp.exp2(s - m)
                l = jnp.sum(e, axis=-1, keepdims=True)
                o = _dot(e.astype(BF16), v_ref[krow:krow + wk, :]) * (1.0 / l)
                store_rows(acc_ref, g, r, i0, o)
                store_rows(den_ref, g, r, i0, jnp.broadcast_to(m + jnp.log2(l), (qb, LANES)))
        l0, l1, l2 = den_ref[0], den_ref[1], den_ref[2]
        m = jnp.maximum(jnp.maximum(l0, l1), l2)
        e0, e1, e2 = jnp.exp2(l0 - m), jnp.exp2(l1 - m), jnp.exp2(l2 - m)
        o_ref[...] = ((e0 * acc_ref[0] + e1 * acc_ref[1] + e2 * acc_ref[2]) / (e0 + e1 + e2)).astype(BF16)


def _attn_b(qkv_nat, qkv_g1, qkv_g2, ctl, b, t, qb=128):
    n = b * t
    blk = lambda off: pl.BlockSpec((None, t, LANES), lambda i, h: (off + h, i, 0))
    return pl.pallas_call(
        functools.partial(_attn_b_kernel, t=t, qb=qb),
        grid=(b, B_HEADS),
        in_specs=[
            pl.BlockSpec(memory_space=pltpu.SMEM),
            blk(3 * HEADS_PER_CHUNK), blk(4 * HEADS_PER_CHUNK), blk(5 * HEADS_PER_CHUNK),
            blk(0), blk(HEADS_PER_CHUNK), blk(2 * HEADS_PER_CHUNK),
            blk(0), blk(HEADS_PER_CHUNK), blk(2 * HEADS_PER_CHUNK),
        ],
        out_specs=pl.BlockSpec((None, t, LANES), lambda i, h: (h, i, 0)),
        out_shape=jax.ShapeDtypeStruct((B_HEADS, n, LANES), BF16),
        scratch_shapes=[
            pltpu.VMEM((len(B_GROUPS), t, LANES), F32),
            pltpu.VMEM((len(B_GROUPS), t, LANES), F32),
            pltpu.VMEM((2, LANES, t), BF16),
        ],
        compiler_params=pltpu.CompilerParams(
            dimension_semantics=("arbitrary", "arbitrary"), vmem_limit_bytes=VMEM_LIMIT_BYTES),
        name="attn_b",
    )(ctl, qkv_nat, qkv_nat, qkv_nat, qkv_g1, qkv_g1, qkv_g1, qkv_g2, qkv_g2, qkv_g2)


def _merge_kernel(x_ref, a_ref, b_ref, cq_ref, mem_ref, gm_ref, wkv_ref, gk_ref, g_ref, wg_ref, bg_ref, wb_ref,
                  wo_ref, o_ref, ckv_ref, *, tiles_per_batch):
    half = C_HEADS * C_HEAD_DIM

    @pl.when(pl.program_id(0) % tiles_per_batch == 0)
    def _():
        mn = _rms(mem_ref[0], gm_ref[...]).astype(BF16)
        kv = _dot(mn, wkv_ref[...])
        ckv_ref[:, :half] = _head_norm(kv[:, :half], gk_ref[...], "full").astype(BF16)
        ckv_ref[:, half:] = kv[:, half:].astype(BF16)

    x = x_ref[...]
    hb = _rms(x, g_ref[...]).astype(BF16)
    d = x.shape[-1]

    def gated(g, branch):
        gate = jax.nn.sigmoid(_dot(hb, wg_ref[:, g * d:(g + 1) * d]) + bg_ref[:, g * d:(g + 1) * d])
        return gate * _dot(branch, wb_ref[g])

    wide = lambda ref: jnp.concatenate([ref[hh] for hh in range(HEADS_PER_CHUNK)], axis=1)
    acc = gated(0, wide(a_ref)) + gated(1, wide(b_ref))
    heads = []
    for hh in range(C_HEADS):
        cols = slice(hh * C_HEAD_DIM, (hh + 1) * C_HEAD_DIM)
        s = _dot_nt(cq_ref[hh], ckv_ref[:, cols])
        e = jnp.exp2(s - jnp.max(s, axis=-1, keepdims=True))
        inv = 1.0 / jnp.sum(e, axis=-1, keepdims=True)
        heads.append(_dot(e.astype(BF16), ckv_ref[:, half + hh * C_HEAD_DIM:half + (hh + 1) * C_HEAD_DIM]) * inv)
    acc = acc + gated(2, jnp.concatenate(heads, axis=1).astype(BF16))
    o_ref[...] = x + _dot(acc.astype(BF16), wo_ref[...])


def _merge(x2, out_a, out_b, qkv_nat, mem, gm, w_kv, gk4, norm_g, w_gate, b_gate, w_branch, w_out, t, tm=512):
    n, d = x2.shape
    n_mem, kv_cols = mem.shape[1], w_kv.shape[1]
    row = lambda w: pl.BlockSpec((tm, w), lambda i: (i, 0))
    heads = lambda chunk: pl.BlockSpec((HEADS_PER_CHUNK, tm, LANES), lambda i: (chunk, i, 0))
    return pl.pallas_call(
        functools.partial(_merge_kernel, tiles_per_batch=t // tm),
        grid=(n // tm,),
        in_specs=[
            row(d), heads(0), heads(0), heads(NAT_HEADS // HEADS_PER_CHUNK - 1),
            pl.BlockSpec((1, n_mem, d), lambda i: (i // (t // tm), 0, 0)),
            _const_spec((1, d)), _const_spec((d, kv_cols)), _const_spec((1, kv_cols // 2)),
            _const_spec((1, d)),
            _const_spec((d, N_BRANCHES * d)),
            _const_spec((1, N_BRANCHES * d)),
            _const_spec((N_BRANCHES, BRANCH_WIDTH, d)),
            _const_spec((d, d)),
        ],
        out_specs=row(d),
        out_shape=jax.ShapeDtypeStruct((n, d), F32),
        scratch_shapes=[pltpu.VMEM((n_mem, kv_cols), BF16)],
        compiler_params=pltpu.CompilerParams(
            dimension_semantics=("arbitrary",), vmem_limit_bytes=VMEM_LIMIT_BYTES),
        name="merge",
    )(x2, out_a, out_b, qkv_nat, mem, gm, w_kv, gk4, norm_g, w_gate, b_gate, w_branch, w_out)


def _ffn_kernel(x_ref, g_ref, wg_ref, wu_ref, wd_ref, o_ref, *, chunks):
    x = x_ref[...]
    hb = _rms(x, g_ref[...]).astype(BF16)
    acc = x
    for c0, c1 in chunks:
        gt = _dot(hb, wg_ref[:, c0:c1])
        up = _dot(hb, wu_ref[:, c0:c1])
        acc = acc + _dot((jax.nn.silu(gt) * up).astype(BF16), wd_ref[c0:c1, :])
    o_ref[...] = acc


def _ffn(x2, norm_g, w_gate, w_up, w_down, tm=512, fc=768):
    n, d = x2.shape
    d_ff = w_gate.shape[1]
    chunks = tuple((c, min(c + fc, d_ff)) for c in range(0, d_ff, fc))
    row = pl.BlockSpec((tm, d), lambda i: (i, 0))
    return pl.pallas_call(
        functools.partial(_ffn_kernel, chunks=chunks),
        grid=(n // tm,),
        in_specs=[row, _const_spec((1, d)), _const_spec((d, d_ff)), _const_spec((d, d_ff)),
                  _const_spec((d_ff, d))],
        out_specs=row,
        out_shape=jax.ShapeDtypeStruct((n, d), F32),
        compiler_params=pltpu.CompilerParams(
            dimension_semantics=("arbitrary",), vmem_limit_bytes=VMEM_LIMIT_BYTES),
        name="ffn",
    )(x2, norm_g, w_gate, w_up, w_down)


def _score_ctl(slopes, head_dim, gq, gk):
    bound = math.sqrt(head_dim) * jnp.max(jnp.abs(gq)) * jnp.max(jnp.abs(gk))
    return jnp.concatenate([slopes.reshape(-1),
                            jnp.stack([bound * LOG2E, (bound <= MAX_SAFE_SCORE_BOUND).astype(F32)])])


def kernel(x, mem, norm_mix, w_in, w_gate, b_gate, a_q_norm, a_k_norm, a_lambda_q1, a_lambda_k1, a_lambda_q2,
           a_lambda_k2, a_subln, b_q_norm, b_k_norm, mem_norm, w_mem_kv, c_q_norm, c_k_norm, w_branch, w_out,
           norm_ffn, w_ffn_gate, w_ffn_up, w_ffn_down):
    b, t, d = x.shape
    n = b * t
    n_groups = len(B_GROUPS)
    slopes_a = jnp.exp2(-ALIBI_MAX_BIAS * jnp.arange(1, A_HEADS + 1, dtype=F32) / A_HEADS)
    nb = n_groups * B_HEADS
    slopes_b = jnp.exp2(-ALIBI_MAX_BIAS * jnp.arange(1, nb + 1, dtype=F32) / nb)

    l = 0
    bw = BRANCH_WIDTH
    row = lambda v: v.reshape(1, -1)
    tiled = lambda v: jnp.tile(v, bw // v.shape[0])
    ones = jnp.ones((bw,), F32)
    a_qs = A_HEAD_DIM ** -0.5 * LOG2E
    b_qs = B_HEAD_DIM ** -0.5 * LOG2E
    c_qs = C_HEAD_DIM ** -0.5 * LOG2E
    gain_nat = jnp.stack([tiled(a_q_norm[l]) * a_qs, tiled(a_k_norm[l]), ones, tiled(b_q_norm[l]) * b_qs,
                          tiled(b_k_norm[l]), ones, tiled(c_q_norm[l]) * c_qs])
    gain_grp = jnp.stack([tiled(b_q_norm[l]) * b_qs, tiled(b_k_norm[l]), ones])

    qkv_nat, qkv_g1, qkv_g2 = _project(x, row(norm_mix[l]), w_in[l], gain_nat, gain_grp)
    qkv_nat = qkv_nat.reshape(NAT_HEADS, n, LANES)
    qkv_g1 = qkv_g1.reshape(GRP_HEADS, n, LANES)
    qkv_g2 = qkv_g2.reshape(GRP_HEADS, n, LANES)

    out_a = _attn_a(qkv_nat, _score_ctl(slopes_a, A_HEAD_DIM, a_q_norm[l], a_k_norm[l]),
                    row(a_lambda_q1[l]), row(a_lambda_k1[l]), row(a_lambda_q2[l]), row(a_lambda_k2[l]),
                    row(a_subln[l]), b, t)
    out_b = _attn_b(qkv_nat, qkv_g1, qkv_g2, _score_ctl(slopes_b, B_HEAD_DIM, b_q_norm[l], b_k_norm[l]), b, t)
    x2 = x.reshape(n, d)
    x2 = _merge(x2, out_a, out_b, qkv_nat, mem, row(mem_norm[l]), w_mem_kv[l], row(tiled(c_k_norm[l])),
                row(norm_mix[l]), w_gate[l], row(b_gate[l]), w_branch[l], w_out[l], t)
    x2 = _ffn(x2, row(norm_ffn[l]), w_ffn_gate[l], w_ffn_up[l], w_ffn_down[l])
    return x2.reshape(b, t, d)
```

```python
import functools
import math

import jax
import jax.numpy as jnp
from jax import lax
from jax.experimental import pallas as pl
from jax.experimental.pallas import tpu as pltpu

F32 = jnp.float32
BF16 = jnp.bfloat16

D_MODEL = 1024
A_HEADS = 4
A_HEAD_DIM = 64
A_V_DIM = 2 * A_HEAD_DIM
B_GROUPS = ((128, 1), (512, 4), (2048, 16))
B_HEADS = 4
B_HEAD_DIM = 128
C_HEADS = 4
C_HEAD_DIM = 128
BRANCH_WIDTH = 512
N_BRANCHES = 3
EPS = 1e-6
ALIBI_MAX_BIAS = 8.0
LAMBDA_INIT = 0.8 - 0.6 * math.exp(-0.3 * 0)

LANES = 128
VMEM_LIMIT_BYTES = 56 * 1024 * 1024

NAT_COLS = 7 * BRANCH_WIDTH
GRP_COLS = 3 * BRANCH_WIDTH
HEADS_PER_CHUNK = BRANCH_WIDTH // LANES
NAT_HEADS = NAT_COLS // LANES
GRP_HEADS = GRP_COLS // LANES
NAT_NORMS = ("half", "half", "none", "full", "full", "none", "full")
GRP_COL_OFFS = (3 * BRANCH_WIDTH, 6 * BRANCH_WIDTH, 9 * BRANCH_WIDTH)
NAT_COL_OFFS = (0, BRANCH_WIDTH, 2 * BRANCH_WIDTH) + GRP_COL_OFFS + (12 * BRANCH_WIDTH,)
GRP_NORMS = ("full", "full", "none")
NAT_GAIN_ROWS = (0, 1, 2, 3, 4, 2, 5)
GRP_GAIN_ROWS = (3, 4, 2)
MEM_K_GAIN_ROW = 6
NEG_BIG = -1e30
LOG2E = 1.4426950408889634
MAX_SAFE_SCORE_BOUND = 40.0


def _rms(x, gain):
    return x * lax.rsqrt(jnp.mean(x * x, axis=-1, keepdims=True) + EPS) * gain


def _dot(a, b):
    return jnp.dot(a, b, preferred_element_type=F32)


def _dot_nt(a, b):
    return lax.dot_general(a, b, (((1,), (1,)), ((), ())), preferred_element_type=F32)


def _const_spec(shape):
    nd = len(shape)
    return pl.BlockSpec(shape, lambda *_: (0,) * nd, pipeline_mode=pl.Buffered(1))


def _head_norm(y, gain, kind):
    if kind == "none":
        return y
    lo_mask = lax.broadcasted_iota(jnp.int32, (1, LANES), 1) < A_HEAD_DIM
    cols = []
    for c in range(0, y.shape[1], LANES):
        z = y[:, c:c + LANES]
        sq = z * z
        s_all = jnp.sum(sq, axis=-1, keepdims=True)
        if kind == "full":
            ms = s_all * (1.0 / LANES)
        else:
            s_lo = jnp.sum(jnp.where(lo_mask, sq, 0.0), axis=-1, keepdims=True)
            ms = jnp.where(lo_mask, s_lo, s_all - s_lo) * (1.0 / A_HEAD_DIM)
        cols.append(z * lax.rsqrt(ms + EPS))
    return jnp.concatenate(cols, axis=1) * gain


def _proj_kernel(x_ref, g_ref, w_ref, gains_ref, on_ref, o1_ref, o2_ref, slab_ref, p4_ref, hp_ref, *, tm):
    h = _rms(x_ref[0], g_ref[...])
    hb = h.astype(BF16)
    bw = BRANCH_WIDTH
    for ci, kind in enumerate(NAT_NORMS):
        c0 = NAT_COL_OFFS[ci]
        gain = gains_ref[NAT_GAIN_ROWS[ci]:NAT_GAIN_ROWS[ci] + 1, :]
        res = _head_norm(_dot(hb, w_ref[:, c0:c0 + bw]), gain, kind).astype(BF16)
        for hh in range(HEADS_PER_CHUNK):
            on_ref[ci * HEADS_PER_CHUNK + hh, 0] = res[:, hh * LANES:(hh + 1) * LANES]
    n_slabs = D_MODEL // LANES
    for s in range(n_slabs):
        slab_ref[s] = h[:, s * LANES:(s + 1) * LANES]
    for g, dil, o_ref in ((1, 4, o1_ref), (2, 16, o2_ref)):
        n = tm // dil
        if dil == 4:
            for r in range(dil):
                for s in range(n_slabs):
                    rows = slab_ref[s, pl.ds(r, n, stride=dil), :]
                    p4_ref[s, r * n:(r + 1) * n, :] = rows
                    hp_ref[0, r * n:(r + 1) * n, s * LANES:(s + 1) * LANES] = rows.astype(BF16)
        else:
            n4 = tm // 4
            for r4 in range(4):
                for q in range(4):
                    r = r4 + 4 * q
                    for s in range(n_slabs):
                        hp_ref[1, r * n:(r + 1) * n, s * LANES:(s + 1) * LANES] = (
                            p4_ref[s, pl.ds(r4 * n4 + q, n, stride=4), :].astype(BF16))
        hp = hp_ref[g - 1]
        for ci, kind in enumerate(GRP_NORMS):
            c0 = GRP_COL_OFFS[ci] + g * bw
            gain = gains_ref[GRP_GAIN_ROWS[ci]:GRP_GAIN_ROWS[ci] + 1, :]
            res = _head_norm(_dot(hp, w_ref[:, c0:c0 + bw]), gain, kind).astype(BF16)
            for hh in range(HEADS_PER_CHUNK):
                for r in range(dil):
                    o_ref[ci * HEADS_PER_CHUNK + hh, 0, r] = res[r * n:(r + 1) * n, hh * LANES:(hh + 1) * LANES]


def _project(x, norm_g, w_in, gains, tm=512):
    b, t, d = x.shape
    grid = (b, t // tm)
    return pl.pallas_call(
        functools.partial(_proj_kernel, tm=tm),
        grid=grid,
        in_specs=[
            pl.BlockSpec((1, tm, d), lambda i, j: (i, j, 0)),
            _const_spec((1, d)),
            _const_spec(w_in.shape),
            _const_spec(gains.shape),
        ],
        out_specs=[
            pl.BlockSpec((NAT_HEADS, 1, tm, LANES), lambda i, j: (0, i, j, 0)),
            pl.BlockSpec((GRP_HEADS, 1, 4, tm // 4, LANES), lambda i, j: (0, i, 0, j, 0)),
            pl.BlockSpec((GRP_HEADS, 1, 16, tm // 16, LANES), lambda i, j: (0, i, 0, j, 0)),
        ],
        out_shape=[
            jax.ShapeDtypeStruct((NAT_HEADS, b, t, LANES), BF16),
            jax.ShapeDtypeStruct((GRP_HEADS, b, 4, t // 4, LANES), BF16),
            jax.ShapeDtypeStruct((GRP_HEADS, b, 16, t // 16, LANES), BF16),
        ],
        scratch_shapes=[
            pltpu.VMEM((d // LANES, tm, LANES), F32),
            pltpu.VMEM((d // LANES, tm, LANES), F32),
            pltpu.VMEM((2, tm, d), BF16),
        ],
        compiler_params=pltpu.CompilerParams(
            dimension_semantics=("arbitrary", "arbitrary"), vmem_limit_bytes=VMEM_LIMIT_BYTES),
        name="proj",
    )(x, norm_g, w_in, gains)


def _attn_a_kernel(ctl_ref, q_ref, k_ref, v_ref, lq1_ref, lk1_ref, lq2_ref, lk2_ref,
                   sub_ref, o_ref, vt_ref, tab_ref, et_ref, ot_ref, l_ref, *, t, qb, kc, inflight):
    h = pl.program_id(0)
    nblk = t // qb
    slope2 = ctl_ref[h] * LOG2E
    shift = ctl_ref[A_HEADS]
    fast = ctl_ref[A_HEADS + 1] > 0.5
    lo_mask = lax.broadcasted_iota(jnp.int32, (1, LANES), 1) < A_HEAD_DIM

    lam = (jnp.exp(jnp.sum(lq1_ref[...] * lk1_ref[...], keepdims=True))
           - jnp.exp(jnp.sum(lq2_ref[...] * lk2_ref[...], keepdims=True)) + LAMBDA_INIT)

    vt_ref[...] = v_ref[...].T

    @pl.when(pl.program_id(1) == 0)
    def _():
        cc = lax.broadcasted_iota(jnp.int32, (2 * t - qb, qb), 0)
        il = lax.broadcasted_iota(jnp.int32, (2 * t - qb, qb), 1)
        tab_ref[...] = (-slope2 * jnp.abs(cc - (t - qb) - il).astype(F32)
                        - jnp.where(fast, shift, 0.0))

    def fold8(x, op):
        return op(x.reshape(x.shape[0] // 8, 8, x.shape[1]), axis=0)

    def block(j, slot, exact_max):
        q = q_ref[pl.ds(pl.multiple_of(j * qb, qb), qb), :]
        zero = jnp.zeros_like(q)
        q2 = jnp.concatenate([jnp.where(lo_mask, q, zero), jnp.where(lo_mask, zero, q)], axis=0)
        off = pl.multiple_of(t - qb - j * qb, qb)

        def scores(c):
            s = _dot_nt(k_ref[c * kc:(c + 1) * kc, :], q2)
            bias = tab_ref[pl.ds(off + c * kc, kc), :]
            return s + jnp.concatenate([bias, bias], axis=1)

        m = None
        if exact_max:
            for c in range(t // kc):
                cm = fold8(scores(c), jnp.max)
                m = cm if m is None else jnp.maximum(m, cm)
            m = jnp.max(m, axis=0, keepdims=True)
        acc = None
        for c in range(t // kc):
            s = scores(c)
            e = jnp.exp2(s - m if exact_max else s)
            et_ref[slot, c * kc:(c + 1) * kc, :] = e.astype(BF16)
            part = fold8(e, jnp.sum)
            acc = part if acc is None else acc + part
        l_ref[pl.ds(j, 1), :] = jnp.sum(acc, axis=0, keepdims=True)
        ot_ref[j] = _dot(vt_ref[...], et_ref[slot])

    def finish(j):
        ot = ot_ref[j]
        inv = 1.0 / l_ref[pl.ds(j, 1), :]
        o = ot[:, :qb] * inv[:, :qb] - ot[:, qb:] * (lam * inv[:, qb:])
        o = _rms(o.T, sub_ref[...]) * (1.0 - LAMBDA_INIT)
        o_ref[pl.ds(pl.multiple_of(j * qb, qb), qb), :] = o.astype(BF16)

    def run(exact_max):
        for s in range(inflight):
            block(s, s, exact_max)

        def step(i, carry):
            for s in range(inflight):
                finish(inflight * (i - 1) + s)
            for s in range(inflight):
                block(inflight * i + s, s, exact_max)
            return carry
        lax.fori_loop(1, nblk // inflight, step, 0)
        for s in range(inflight):
            finish(nblk - inflight + s)

    @pl.when(fast)
    def _():
        run(False)

    @pl.when(jnp.logical_not(fast))
    def _():
        run(True)


def _attn_a(qkv_nat, ctl, lq1, lk1, lq2, lk2, subln, b, t, qb=256, kc=512, inflight=4):
    n = b * t
    nblk = t // qb
    vec = lambda w: _const_spec((1, w))
    return pl.pallas_call(
        functools.partial(_attn_a_kernel, t=t, qb=qb, kc=kc, inflight=inflight),
        grid=(A_HEADS, b),
        in_specs=[
            pl.BlockSpec(memory_space=pltpu.SMEM),
            pl.BlockSpec((None, t, LANES), lambda h, i: (h, i, 0)),
            pl.BlockSpec((None, t, LANES), lambda h, i: (A_HEADS + h, i, 0)),
            pl.BlockSpec((None, t, LANES), lambda h, i: (2 * A_HEADS + h, i, 0)),
            vec(A_HEAD_DIM), vec(A_HEAD_DIM), vec(A_HEAD_DIM), vec(A_HEAD_DIM),
            vec(A_V_DIM),
        ],
        out_specs=pl.BlockSpec((None, t, LANES), lambda h, i: (h, i, 0)),
        out_shape=jax.ShapeDtypeStruct((A_HEADS, n, LANES), BF16),
        scratch_shapes=[
            pltpu.VMEM((LANES, t), BF16),
            pltpu.VMEM((2 * t - qb, qb), F32),
            pltpu.VMEM((inflight, t, 2 * qb), BF16),
            pltpu.VMEM((nblk, LANES, 2 * qb), F32),
            pltpu.VMEM((nblk, 2 * qb), F32),
        ],
        compiler_params=pltpu.CompilerParams(
            dimension_semantics=("arbitrary", "arbitrary"), vmem_limit_bytes=VMEM_LIMIT_BYTES),
        name="attn_a",
    )(ctl, qkv_nat, qkv_nat, qkv_nat, lq1, lk1, lq2, lk2, subln)


def _attn_b_kernel(ctl_ref, q0_ref, k0_ref, v0_ref, q1_ref, k1_ref, v1_ref, q2_ref, k2_ref, v2_ref,
                   o_ref, acc_ref, den_ref, kt_ref, *, t, qb):
    h = pl.program_id(1)
    n_ctl = len(B_GROUPS) * B_HEADS
    shift = ctl_ref[n_ctl]
    fast = ctl_ref[n_ctl + 1] > 0.5
    refs = ((q0_ref, k0_ref, v0_ref), (q1_ref, k1_ref, v1_ref), (q2_ref, k2_ref, v2_ref))

    def blocks(g):
        window, dil = B_GROUPS[g]
        n_side = window // (2 * dil)
        sub = t // dil
        wk = min(2 * qb, sub)
        for r in range(dil):
            for j in range(sub // qb):
                i0 = j * qb
                ws = min(max(i0 - n_side, 0), sub - wk)
                yield r, i0, r * sub + i0, r * sub + ws, wk, i0 - ws, n_side

    def bias_table(g, wk, offset, n_side, sub_shift):
        slope2 = ctl_ref[g * B_HEADS + h] * (float(B_GROUPS[g][1]) * LOG2E)
        ql = lax.broadcasted_iota(jnp.int32, (qb, wk), 0)
        kl = lax.broadcasted_iota(jnp.int32, (qb, wk), 1)
        dist = jnp.abs(kl - ql - offset)
        return jnp.where(dist <= n_side, -slope2 * dist.astype(F32) - sub_shift, NEG_BIG)

    def store_rows(ref, g, r, i0, val):
        dil = B_GROUPS[g][1]
        if dil == 1:
            ref[g, i0:i0 + qb, :] = val
        else:
            ref[g, pl.ds(r + dil * i0, qb, stride=dil), :] = val

    @pl.when(fast)
    def _():
        half = qb // 2
        for g in (1, 2, 0):
            q_ref, k_ref, v_ref = refs[g]
            kt_ref[0] = k_ref[...].T
            kt_ref[1] = jnp.concatenate([k_ref[half:, :], k_ref[:half, :]], axis=0).T
            tables = {}
            for r, i0, qrow, krow, wk, offset, n_side in blocks(g):
                if offset not in tables:
                    tables[offset] = bias_table(g, wk, offset, n_side, shift)
                if krow % qb == 0:
                    kwt = kt_ref[0, :, krow:krow + wk]
                else:
                    kwt = kt_ref[1, :, krow - half:krow - half + wk]
                s = _dot(q_ref[qrow:qrow + qb, :], kwt)
                e = jnp.exp2(s + tables[offset])
                den = jnp.sum(e, axis=-1, keepdims=True)
                acc = _dot(e.astype(BF16), v_ref[krow:krow + wk, :])
                if B_GROUPS[g][1] == 1:
                    rows = slice(i0, i0 + qb)
                    acc = acc + acc_ref[1, rows, :] + acc_ref[2, rows, :]
                    den = den + den_ref[1, rows, :] + den_ref[2, rows, :]
                    o_ref[rows, :] = (acc / den).astype(BF16)
                else:
                    store_rows(acc_ref, g, r, i0, acc)
                    store_rows(den_ref, g, r, i0, jnp.broadcast_to(den, (qb, LANES)))

    @pl.when(jnp.logical_not(fast))
    def _():
        for g in range(len(B_GROUPS)):
            q_ref, k_ref, v_ref = refs[g]
            tables = {}
            for r, i0, qrow, krow, wk, offset, n_side in blocks(g):
                if offset not in tables:
                    tables[offset] = bias_table(g, wk, offset, n_side, 0.0)
                s = _dot_nt(q_ref[qrow:qrow + qb, :], k_ref[krow:krow + wk, :]) + tables[offset]
                m = jnp.max(s, axis=-1, keepdims=True)
                e = jnp.exp2(s - m)
                l = jnp.sum(e, axis=-1, keepdims=True)
                o = _dot(e.astype(BF16), v_ref[krow:krow + wk, :]) * (1.0 / l)
                store_rows(acc_ref, g, r, i0, o)
                store_rows(den_ref, g, r, i0, jnp.broadcast_to(m + jnp.log2(l), (qb, LANES)))
        l0, l1, l2 = den_ref[0], den_ref[1], den_ref[2]
        m = jnp.maximum(jnp.maximum(l0, l1), l2)
        e0, e1, e2 = jnp.exp2(l0 - m), jnp.exp2(l1 - m), jnp.exp2(l2 - m)
        o_ref[...] = ((e0 * acc_ref[0] + e1 * acc_ref[1] + e2 * acc_ref[2]) / (e0 + e1 + e2)).astype(BF16)


def _attn_b(qkv_nat, qkv_g1, qkv_g2, ctl, b, t, qb=128):
    n = b * t
    blk = lambda off: pl.BlockSpec((None, t, LANES), lambda i, h: (off + h, i, 0))
    return pl.pallas_call(
        functools.partial(_attn_b_kernel, t=t, qb=qb),
        grid=(b, B_HEADS),
        in_specs=[
            pl.BlockSpec(memory_space=pltpu.SMEM),
            blk(3 * HEADS_PER_CHUNK), blk(4 * HEADS_PER_CHUNK), blk(5 * HEADS_PER_CHUNK),
            blk(0), blk(HEADS_PER_CHUNK), blk(2 * HEADS_PER_CHUNK),
            blk(0), blk(HEADS_PER_CHUNK), blk(2 * HEADS_PER_CHUNK),
        ],
        out_specs=pl.BlockSpec((None, t, LANES), lambda i, h: (h, i, 0)),
        out_shape=jax.ShapeDtypeStruct((B_HEADS, n, LANES), BF16),
        scratch_shapes=[
            pltpu.VMEM((len(B_GROUPS), t, LANES), F32),
            pltpu.VMEM((len(B_GROUPS), t, LANES), F32),
            pltpu.VMEM((2, LANES, t), BF16),
        ],
        compiler_params=pltpu.CompilerParams(
            dimension_semantics=("arbitrary", "arbitrary"), vmem_limit_bytes=VMEM_LIMIT_BYTES),
        name="attn_b",
    )(ctl, qkv_nat, qkv_nat, qkv_nat, qkv_g1, qkv_g1, qkv_g1, qkv_g2, qkv_g2, qkv_g2)


def _merge_kernel(x_ref, a_ref, b_ref, cq_ref, mem_ref, gm_ref, wkv_ref, gk_ref, g_ref, wg_ref, bg_ref, wb_ref,
                  wo_ref, o_ref, ckv_ref, *, tiles_per_batch):
    half = C_HEADS * C_HEAD_DIM

    @pl.when(pl.program_id(0) % tiles_per_batch == 0)
    def _():
        mn = _rms(mem_ref[0], gm_ref[...]).astype(BF16)
        kv = _dot(mn, wkv_ref[...])
        gk = gk_ref[MEM_K_GAIN_ROW:MEM_K_GAIN_ROW + 1, :]
        ckv_ref[:, :half] = _head_norm(kv[:, :half], gk, "full").astype(BF16)
        ckv_ref[:, half:] = kv[:, half:].astype(BF16)

    x = x_ref[...]
    hb = _rms(x, g_ref[...]).astype(BF16)
    d = x.shape[-1]

    def gated(g, branch):
        gate = jax.nn.sigmoid(_dot(hb, wg_ref[:, g * d:(g + 1) * d]) + bg_ref[:, g * d:(g + 1) * d])
        return gate * _dot(branch, wb_ref[g])

    wide = lambda ref: jnp.concatenate([ref[hh] for hh in range(HEADS_PER_CHUNK)], axis=1)
    acc = gated(0, wide(a_ref)) + gated(1, wide(b_ref))
    heads = []
    for hh in range(C_HEADS):
        cols = slice(hh * C_HEAD_DIM, (hh + 1) * C_HEAD_DIM)
        s = _dot_nt(cq_ref[hh], ckv_ref[:, cols])
        e = jnp.exp2(s - jnp.max(s, axis=-1, keepdims=True))
        inv = 1.0 / jnp.sum(e, axis=-1, keepdims=True)
        heads.append(_dot(e.astype(BF16), ckv_ref[:, half + hh * C_HEAD_DIM:half + (hh + 1) * C_HEAD_DIM]) * inv)
    acc = acc + gated(2, jnp.concatenate(heads, axis=1).astype(BF16))
    o_ref[...] = x + _dot(acc.astype(BF16), wo_ref[...])


def _merge(x2, out_a, out_b, qkv_nat, mem, gm, w_kv, gains, norm_g, w_gate, b_gate, w_branch, w_out, t, tm=512):
    n, d = x2.shape
    n_mem, kv_cols = mem.shape[1], w_kv.shape[1]
    row = lambda w: pl.BlockSpec((tm, w), lambda i: (i, 0))
    heads = lambda chunk: pl.BlockSpec((HEADS_PER_CHUNK, tm, LANES), lambda i: (chunk, i, 0))
    return pl.pallas_call(
        functools.partial(_merge_kernel, tiles_per_batch=t // tm),
        grid=(n // tm,),
        in_specs=[
            row(d), heads(0), heads(0), heads(NAT_HEADS // HEADS_PER_CHUNK - 1),
            pl.BlockSpec((1, n_mem, d), lambda i: (i // (t // tm), 0, 0)),
            _const_spec((1, d)), _const_spec((d, kv_cols)), _const_spec(gains.shape),
            _const_spec((1, d)),
            _const_spec((d, N_BRANCHES * d)),
            _const_spec((1, N_BRANCHES * d)),
            _const_spec((N_BRANCHES, BRANCH_WIDTH, d)),
            _const_spec((d, d)),
        ],
        out_specs=row(d),
        out_shape=jax.ShapeDtypeStruct((n, d), F32),
        scratch_shapes=[pltpu.VMEM((n_mem, kv_cols), BF16)],
        compiler_params=pltpu.CompilerParams(
            dimension_semantics=("arbitrary",), vmem_limit_bytes=VMEM_LIMIT_BYTES),
        name="merge",
    )(x2, out_a, out_b, qkv_nat, mem, gm, w_kv, gains, norm_g, w_gate, b_gate, w_branch, w_out)


def _ffn_kernel(x_ref, g_ref, wg_ref, wu_ref, wd_ref, o_ref, *, chunks):
    x = x_ref[...]
    hb = _rms(x, g_ref[...]).astype(BF16)
    acc = x
    for c0, c1 in chunks:
        gt = _dot(hb, wg_ref[:, c0:c1])
        up = _dot(hb, wu_ref[:, c0:c1])
        acc = acc + _dot((jax.nn.silu(gt) * up).astype(BF16), wd_ref[c0:c1, :])
    o_ref[...] = acc


def _ffn(x2, norm_g, w_gate, w_up, w_down, tm=512, fc=768):
    n, d = x2.shape
    d_ff = w_gate.shape[1]
    chunks = tuple((c, min(c + fc, d_ff)) for c in range(0, d_ff, fc))
    row = pl.BlockSpec((tm, d), lambda i: (i, 0))
    return pl.pallas_call(
        functools.partial(_ffn_kernel, chunks=chunks),
        grid=(n // tm,),
        in_specs=[row, _const_spec((1, d)), _const_spec((d, d_ff)), _const_spec((d, d_ff)),
                  _const_spec((d_ff, d))],
        out_specs=row,
        out_shape=jax.ShapeDtypeStruct((n, d), F32),
        compiler_params=pltpu.CompilerParams(
            dimension_semantics=("arbitrary",), vmem_limit_bytes=VMEM_LIMIT_BYTES),
        name="ffn",
    )(x2, norm_g, w_gate, w_up, w_down)


def _score_ctl(slopes, head_dim, gq_max, gk_max):
    bound = math.sqrt(head_dim) * gq_max * gk_max
    return jnp.concatenate([slopes.reshape(-1),
                            jnp.stack([bound * LOG2E, (bound <= MAX_SAFE_SCORE_BOUND).astype(F32)])])


def kernel(x, mem, norm_mix, w_in, w_gate, b_gate, a_q_norm, a_k_norm, a_lambda_q1, a_lambda_k1, a_lambda_q2,
           a_lambda_k2, a_subln, b_q_norm, b_k_norm, mem_norm, w_mem_kv, c_q_norm, c_k_norm, w_branch, w_out,
           norm_ffn, w_ffn_gate, w_ffn_up, w_ffn_down):
    b, t, d = x.shape
    n = b * t
    n_groups = len(B_GROUPS)
    slopes_a = jnp.exp2(-ALIBI_MAX_BIAS * jnp.arange(1, A_HEADS + 1, dtype=F32) / A_HEADS)
    nb = n_groups * B_HEADS
    slopes_b = jnp.exp2(-ALIBI_MAX_BIAS * jnp.arange(1, nb + 1, dtype=F32) / nb)

    l = 0
    bw = BRANCH_WIDTH
    row = lambda v: v.reshape(1, -1)
    tiled = lambda v: jnp.tile(v, bw // v.shape[0])
    ones = jnp.ones((bw,), F32)
    a_qs = A_HEAD_DIM ** -0.5 * LOG2E
    b_qs = B_HEAD_DIM ** -0.5 * LOG2E
    c_qs = C_HEAD_DIM ** -0.5 * LOG2E
    gains = jnp.stack([tiled(a_q_norm[l]) * a_qs, tiled(a_k_norm[l]), ones, tiled(b_q_norm[l]) * b_qs,
                       tiled(b_k_norm[l]), tiled(c_q_norm[l]) * c_qs, tiled(c_k_norm[l])])
    pad = lambda v: jnp.pad(v, (0, B_HEAD_DIM - v.shape[0]))
    gmax = jnp.max(jnp.abs(jnp.stack([pad(a_q_norm[l]), pad(a_k_norm[l]), b_q_norm[l], b_k_norm[l]])), axis=1)

    qkv_nat, qkv_g1, qkv_g2 = _project(x, row(norm_mix[l]), w_in[l], gains)
    qkv_nat = qkv_nat.reshape(NAT_HEADS, n, LANES)
    qkv_g1 = qkv_g1.reshape(GRP_HEADS, n, LANES)
    qkv_g2 = qkv_g2.reshape(GRP_HEADS, n, LANES)

    out_a = _attn_a(qkv_nat, _score_ctl(slopes_a, A_HEAD_DIM, gmax[0], gmax[1]),
                    row(a_lambda_q1[l]), row(a_lambda_k1[l]), row(a_lambda_q2[l]), row(a_lambda_k2[l]),
                    row(a_subln[l]), b, t)
    out_b = _attn_b(qkv_nat, qkv_g1, qkv_g2, _score_ctl(slopes_b, B_HEAD_DIM, gmax[2], gmax[3]), b, t)
    x2 = x.reshape(n, d)
    x2 = _merge(x2, out_a, out_b, qkv_nat, mem, row(mem_norm[l]), w_mem_kv[l], gains,
                row(norm_mix[l]), w_gate[l], row(b_gate[l]), w_branch[l], w_out[l], t)
    x2 = _ffn(x2, row(norm_ffn[l]), w_ffn_gate[l], w_ffn_up[l], w_ffn_down[l])
    return x2.reshape(b, t, d)
```

```python
import functools
import math

import jax
import jax.numpy as jnp
from jax import lax
from jax.experimental import pallas as pl
from jax.experimental.pallas import tpu as pltpu

F32 = jnp.float32
BF16 = jnp.bfloat16

D_MODEL = 1024
A_HEADS = 4
A_HEAD_DIM = 64
A_V_DIM = 2 * A_HEAD_DIM
B_GROUPS = ((128, 1), (512, 4), (2048, 16))
B_HEADS = 4
B_HEAD_DIM = 128
C_HEADS = 4
C_HEAD_DIM = 128
BRANCH_WIDTH = 512
N_BRANCHES = 3
EPS = 1e-6
ALIBI_MAX_BIAS = 8.0
LAMBDA_INIT = 0.8 - 0.6 * math.exp(-0.3 * 0)

LANES = 128
VMEM_LIMIT_BYTES = 56 * 1024 * 1024

NAT_COLS = 7 * BRANCH_WIDTH
GRP_COLS = 3 * BRANCH_WIDTH
HEADS_PER_CHUNK = BRANCH_WIDTH // LANES
NAT_HEADS = NAT_COLS // LANES
GRP_HEADS = GRP_COLS // LANES
NAT_NORMS = ("half", "half", "none", "full", "full", "none", "full")
GRP_COL_OFFS = (3 * BRANCH_WIDTH, 6 * BRANCH_WIDTH, 9 * BRANCH_WIDTH)
NAT_COL_OFFS = (0, BRANCH_WIDTH, 2 * BRANCH_WIDTH) + GRP_COL_OFFS + (12 * BRANCH_WIDTH,)
GRP_NORMS = ("full", "full", "none")
NAT_GAIN_ROWS = (0, 1, 2, 3, 4, 2, 5)
GRP_GAIN_ROWS = (3, 4, 2)
MEM_K_GAIN_ROW = 6
NEG_BIG = -1e30
LOG2E = 1.4426950408889634
MAX_SAFE_SCORE_BOUND = 40.0


def _rms(x, gain):
    return x * lax.rsqrt(jnp.mean(x * x, axis=-1, keepdims=True) + EPS) * gain


def _dot(a, b):
    return jnp.dot(a, b, preferred_element_type=F32)


def _dot_nt(a, b):
    return lax.dot_general(a, b, (((1,), (1,)), ((), ())), preferred_element_type=F32)


def _const_spec(shape):
    nd = len(shape)
    return pl.BlockSpec(shape, lambda *_: (0,) * nd, pipeline_mode=pl.Buffered(1))


def _head_norm(y, gain, kind):
    if kind == "none":
        return y
    lo_mask = lax.broadcasted_iota(jnp.int32, (1, LANES), 1) < A_HEAD_DIM
    cols = []
    for c in range(0, y.shape[1], LANES):
        z = y[:, c:c + LANES]
        sq = z * z
        s_all = jnp.sum(sq, axis=-1, keepdims=True)
        if kind == "full":
            ms = s_all * (1.0 / LANES)
        else:
            s_lo = jnp.sum(jnp.where(lo_mask, sq, 0.0), axis=-1, keepdims=True)
            ms = jnp.where(lo_mask, s_lo, s_all - s_lo) * (1.0 / A_HEAD_DIM)
        cols.append(z * lax.rsqrt(ms + EPS))
    return jnp.concatenate(cols, axis=1) * gain


def _proj_kernel(x_ref, g_ref, w_ref, gains_ref, on_ref, o1_ref, o2_ref, slab_ref, p4_ref, hp_ref, *, tm):
    h = _rms(x_ref[0], g_ref[...])
    hb = h.astype(BF16)
    bw = BRANCH_WIDTH
    for ci, kind in enumerate(NAT_NORMS):
        c0 = NAT_COL_OFFS[ci]
        gain = gains_ref[NAT_GAIN_ROWS[ci]:NAT_GAIN_ROWS[ci] + 1, :]
        res = _head_norm(_dot(hb, w_ref[:, c0:c0 + bw]), gain, kind).astype(BF16)
        for hh in range(HEADS_PER_CHUNK):
            on_ref[ci * HEADS_PER_CHUNK + hh, 0] = res[:, hh * LANES:(hh + 1) * LANES]
    n_slabs = D_MODEL // LANES
    for s in range(n_slabs):
        slab_ref[s] = h[:, s * LANES:(s + 1) * LANES]
    for g, dil, o_ref in ((1, 4, o1_ref), (2, 16, o2_ref)):
        n = tm // dil
        if dil == 4:
            for r in range(dil):
                for s in range(n_slabs):
                    rows = slab_ref[s, pl.ds(r, n, stride=dil), :]
                    p4_ref[s, r * n:(r + 1) * n, :] = rows
                    hp_ref[0, r * n:(r + 1) * n, s * LANES:(s + 1) * LANES] = rows.astype(BF16)
        else:
            n4 = tm // 4
            for r4 in range(4):
                for q in range(4):
                    r = r4 + 4 * q
                    for s in range(n_slabs):
                        hp_ref[1, r * n:(r + 1) * n, s * LANES:(s + 1) * LANES] = (
                            p4_ref[s, pl.ds(r4 * n4 + q, n, stride=4), :].astype(BF16))
        hp = hp_ref[g - 1]
        for ci, kind in enumerate(GRP_NORMS):
            c0 = GRP_COL_OFFS[ci] + g * bw
            gain = gains_ref[GRP_GAIN_ROWS[ci]:GRP_GAIN_ROWS[ci] + 1, :]
            res = _head_norm(_dot(hp, w_ref[:, c0:c0 + bw]), gain, kind).astype(BF16)
            for hh in range(HEADS_PER_CHUNK):
                for r in range(dil):
                    o_ref[ci * HEADS_PER_CHUNK + hh, 0, r] = res[r * n:(r + 1) * n, hh * LANES:(hh + 1) * LANES]


def _project(x, norm_g, w_in, gains, tm=512):
    b, t, d = x.shape
    grid = (b, t // tm)
    return pl.pallas_call(
        functools.partial(_proj_kernel, tm=tm),
        grid=grid,
        in_specs=[
            pl.BlockSpec((1, tm, d), lambda i, j: (i, j, 0)),
            _const_spec((1, d)),
            _const_spec(w_in.shape),
            _const_spec(gains.shape),
        ],
        out_specs=[
            pl.BlockSpec((NAT_HEADS, 1, tm, LANES), lambda i, j: (0, i, j, 0)),
            pl.BlockSpec((GRP_HEADS, 1, 4, tm // 4, LANES), lambda i, j: (0, i, 0, j, 0)),
            pl.BlockSpec((GRP_HEADS, 1, 16, tm // 16, LANES), lambda i, j: (0, i, 0, j, 0)),
        ],
        out_shape=[
            jax.ShapeDtypeStruct((NAT_HEADS, b, t, LANES), BF16),
            jax.ShapeDtypeStruct((GRP_HEADS, b, 4, t // 4, LANES), BF16),
            jax.ShapeDtypeStruct((GRP_HEADS, b, 16, t // 16, LANES), BF16),
        ],
        scratch_shapes=[
            pltpu.VMEM((d // LANES, tm, LANES), F32),
            pltpu.VMEM((d // LANES, tm, LANES), F32),
            pltpu.VMEM((2, tm, d), BF16),
        ],
        compiler_params=pltpu.CompilerParams(
            dimension_semantics=("arbitrary", "arbitrary"), vmem_limit_bytes=VMEM_LIMIT_BYTES),
        name="proj",
    )(x, norm_g, w_in, gains)


def _attn_a_kernel(ctl_ref, q_ref, k_ref, v_ref, lq1_ref, lk1_ref, lq2_ref, lk2_ref,
                   sub_ref, o_ref, vt_ref, tab_ref, et_ref, ot_ref, l_ref, *, t, qb, kc, inflight):
    h = pl.program_id(0)
    nblk = t // qb
    slope2 = ctl_ref[h] * LOG2E
    shift = ctl_ref[A_HEADS]
    fast = ctl_ref[A_HEADS + 1] > 0.5
    lo_mask = lax.broadcasted_iota(jnp.int32, (1, LANES), 1) < A_HEAD_DIM

    lam = (jnp.exp(jnp.sum(lq1_ref[...] * lk1_ref[...], keepdims=True))
           - jnp.exp(jnp.sum(lq2_ref[...] * lk2_ref[...], keepdims=True)) + LAMBDA_INIT)

    vt_ref[...] = v_ref[...].T

    @pl.when(pl.program_id(1) == 0)
    def _():
        cc = lax.broadcasted_iota(jnp.int32, (2 * t - qb, qb), 0)
        il = lax.broadcasted_iota(jnp.int32, (2 * t - qb, qb), 1)
        tab_ref[...] = (-slope2 * jnp.abs(cc - (t - qb) - il).astype(F32)
                        - jnp.where(fast, shift, 0.0))

    def fold8(x, op):
        return op(x.reshape(x.shape[0] // 8, 8, x.shape[1]), axis=0)

    def block(j, slot, exact_max):
        q = q_ref[pl.ds(pl.multiple_of(j * qb, qb), qb), :]
        zero = jnp.zeros_like(q)
        q2 = jnp.concatenate([jnp.where(lo_mask, q, zero), jnp.where(lo_mask, zero, q)], axis=0)
        off = pl.multiple_of(t - qb - j * qb, qb)

        def scores(c):
            s = _dot_nt(k_ref[c * kc:(c + 1) * kc, :], q2)
            bias = tab_ref[pl.ds(off + c * kc, kc), :]
            return s + jnp.concatenate([bias, bias], axis=1)

        m = None
        if exact_max:
            for c in range(t // kc):
                cm = fold8(scores(c), jnp.max)
                m = cm if m is None else jnp.maximum(m, cm)
            m = jnp.max(m, axis=0, keepdims=True)
        acc = None
        for c in range(t // kc):
            s = scores(c)
            e = jnp.exp2(s - m if exact_max else s)
            et_ref[slot, c * kc:(c + 1) * kc, :] = e.astype(BF16)
            part = fold8(e, jnp.sum)
            acc = part if acc is None else acc + part
        l_ref[pl.ds(j, 1), :] = jnp.sum(acc, axis=0, keepdims=True)
        ot_ref[j] = _dot(vt_ref[...], et_ref[slot])

    def finish(j):
        ot = ot_ref[j]
        inv = 1.0 / l_ref[pl.ds(j, 1), :]
        o = ot[:, :qb] * inv[:, :qb] - ot[:, qb:] * (lam * inv[:, qb:])
        o = _rms(o.T, sub_ref[...]) * (1.0 - LAMBDA_INIT)
        o_ref[pl.ds(pl.multiple_of(j * qb, qb), qb), :] = o.astype(BF16)

    def run(exact_max):
        for s in range(inflight):
            block(s, s, exact_max)

        def step(i, carry):
            for s in range(inflight):
                finish(inflight * (i - 1) + s)
            for s in range(inflight):
                block(inflight * i + s, s, exact_max)
            return carry
        lax.fori_loop(1, nblk // inflight, step, 0)
        for s in range(inflight):
            finish(nblk - inflight + s)

    @pl.when(fast)
    def _():
        run(False)

    @pl.when(jnp.logical_not(fast))
    def _():
        run(True)


def _attn_a(qkv_nat, ctl, lq1, lk1, lq2, lk2, subln, b, t, qb=256, kc=512, inflight=4):
    n = b * t
    nblk = t // qb
    vec = lambda w: _const_spec((1, w))
    return pl.pallas_call(
        functools.partial(_attn_a_kernel, t=t, qb=qb, kc=kc, inflight=inflight),
        grid=(A_HEADS, b),
        in_specs=[
            pl.BlockSpec(memory_space=pltpu.SMEM),
            pl.BlockSpec((None, t, LANES), lambda h, i: (h, i, 0)),
            pl.BlockSpec((None, t, LANES), lambda h, i: (A_HEADS + h, i, 0)),
            pl.BlockSpec((None, t, LANES), lambda h, i: (2 * A_HEADS + h, i, 0)),
            vec(A_HEAD_DIM), vec(A_HEAD_DIM), vec(A_HEAD_DIM), vec(A_HEAD_DIM),
            vec(A_V_DIM),
        ],
        out_specs=pl.BlockSpec((None, t, LANES), lambda h, i: (h, i, 0)),
        out_shape=jax.ShapeDtypeStruct((A_HEADS, n, LANES), BF16),
        scratch_shapes=[
            pltpu.VMEM((LANES, t), BF16),
            pltpu.VMEM((2 * t - qb, qb), F32),
            pltpu.VMEM((inflight, t, 2 * qb), BF16),
            pltpu.VMEM((nblk, LANES, 2 * qb), F32),
            pltpu.VMEM((nblk, 2 * qb), F32),
        ],
        compiler_params=pltpu.CompilerParams(
            dimension_semantics=("arbitrary", "arbitrary"), vmem_limit_bytes=VMEM_LIMIT_BYTES),
        name="attn_a",
    )(ctl, qkv_nat, qkv_nat, qkv_nat, lq1, lk1, lq2, lk2, subln)


def _attn_b_kernel(ctl_ref, q0_ref, k0_ref, v0_ref, q1_ref, k1_ref, v1_ref, q2_ref, k2_ref, v2_ref,
                   o_ref, acc_ref, den_ref, kt_ref, *, t, qb):
    h = pl.program_id(1)
    n_ctl = len(B_GROUPS) * B_HEADS
    shift = ctl_ref[n_ctl]
    fast = ctl_ref[n_ctl + 1] > 0.5
    refs = ((q0_ref, k0_ref, v0_ref), (q1_ref, k1_ref, v1_ref), (q2_ref, k2_ref, v2_ref))

    def blocks(g):
        window, dil = B_GROUPS[g]
        n_side = window // (2 * dil)
        sub = t // dil
        wk = min(2 * qb, sub)
        for r in range(dil):
            for j in range(sub // qb):
                i0 = j * qb
                ws = min(max(i0 - n_side, 0), sub - wk)
                yield r, i0, r * sub + i0, r * sub + ws, wk, i0 - ws, n_side

    def bias_table(g, wk, offset, n_side, sub_shift):
        slope2 = ctl_ref[g * B_HEADS + h] * (float(B_GROUPS[g][1]) * LOG2E)
        ql = lax.broadcasted_iota(jnp.int32, (qb, wk), 0)
        kl = lax.broadcasted_iota(jnp.int32, (qb, wk), 1)
        dist = jnp.abs(kl - ql - offset)
        return jnp.where(dist <= n_side, -slope2 * dist.astype(F32) - sub_shift, NEG_BIG)

    def store_rows(ref, g, r, i0, val):
        dil = B_GROUPS[g][1]
        if dil == 1:
            ref[g, i0:i0 + qb, :] = val
        else:
            ref[g, pl.ds(r + dil * i0, qb, stride=dil), :] = val

    @pl.when(fast)
    def _():
        half = qb // 2
        for g in (1, 2, 0):
            q_ref, k_ref, v_ref = refs[g]
            kt_ref[0] = k_ref[...].T
            if any(krow % qb for _, _, _, krow, _, _, _ in blocks(g)):
                kt_ref[1] = jnp.concatenate([k_ref[half:, :], k_ref[:half, :]], axis=0).T
            tables = {}
            for r, i0, qrow, krow, wk, offset, n_side in blocks(g):
                if offset not in tables:
                    tables[offset] = bias_table(g, wk, offset, n_side, shift)
                if krow % qb == 0:
                    kwt = kt_ref[0, :, krow:krow + wk]
                else:
                    kwt = kt_ref[1, :, krow - half:krow - half + wk]
                s = _dot(q_ref[qrow:qrow + qb, :], kwt)
                e = jnp.exp2(s + tables[offset])
                den = jnp.sum(e, axis=-1, keepdims=True)
                acc = _dot(e.astype(BF16), v_ref[krow:krow + wk, :])
                if B_GROUPS[g][1] == 1:
                    rows = slice(i0, i0 + qb)
                    acc = acc + acc_ref[1, rows, :] + acc_ref[2, rows, :]
                    den = den + den_ref[1, rows, :] + den_ref[2, rows, :]
                    o_ref[rows, :] = (acc / den).astype(BF16)
                else:
                    store_rows(acc_ref, g, r, i0, acc)
                    store_rows(den_ref, g, r, i0, jnp.broadcast_to(den, (qb, LANES)))

    @pl.when(jnp.logical_not(fast))
    def _():
        for g in range(len(B_GROUPS)):
            q_ref, k_ref, v_ref = refs[g]
            tables = {}
            for r, i0, qrow, krow, wk, offset, n_side in blocks(g):
                if offset not in tables:
                    tables[offset] = bias_table(g, wk, offset, n_side, 0.0)
                s = _dot_nt(q_ref[qrow:qrow + qb, :], k_ref[krow:krow + wk, :]) + tables[offset]
                m = jnp.max(s, axis=-1, keepdims=True)
                e = jnp.exp2(s - m)
                l = jnp.sum(e, axis=-1, keepdims=True)
                o = _dot(e.astype(BF16), v_ref[krow:krow + wk, :]) * (1.0 / l)
                store_rows(acc_ref, g, r, i0, o)
                store_rows(den_ref, g, r, i0, jnp.broadcast_to(m + jnp.log2(l), (qb, LANES)))
        l0, l1, l2 = den_ref[0], den_ref[1], den_ref[2]
        m = jnp.maximum(jnp.maximum(l0, l1), l2)
        e0, e1, e2 = jnp.exp2(l0 - m), jnp.exp2(l1 - m), jnp.exp2(l2 - m)
        o_ref[...] = ((e0 * acc_ref[0] + e1 * acc_ref[1] + e2 * acc_ref[2]) / (e0 + e1 + e2)).astype(BF16)


def _attn_b(qkv_nat, qkv_g1, qkv_g2, ctl, b, t, qb=128):
    n = b * t
    blk = lambda off: pl.BlockSpec((None, t, LANES), lambda i, h: (off + h, i, 0))
    return pl.pallas_call(
        functools.partial(_attn_b_kernel, t=t, qb=qb),
        grid=(b, B_HEADS),
        in_specs=[
            pl.BlockSpec(memory_space=pltpu.SMEM),
            blk(3 * HEADS_PER_CHUNK), blk(4 * HEADS_PER_CHUNK), blk(5 * HEADS_PER_CHUNK),
            blk(0), blk(HEADS_PER_CHUNK), blk(2 * HEADS_PER_CHUNK),
            blk(0), blk(HEADS_PER_CHUNK), blk(2 * HEADS_PER_CHUNK),
        ],
        out_specs=pl.BlockSpec((None, t, LANES), lambda i, h: (h, i, 0)),
        out_shape=jax.ShapeDtypeStruct((B_HEADS, n, LANES), BF16),
        scratch_shapes=[
            pltpu.VMEM((len(B_GROUPS), t, LANES), F32),
            pltpu.VMEM((len(B_GROUPS), t, LANES), F32),
            pltpu.VMEM((2, LANES, t), BF16),
        ],
        compiler_params=pltpu.CompilerParams(
            dimension_semantics=("arbitrary", "arbitrary"), vmem_limit_bytes=VMEM_LIMIT_BYTES),
        name="attn_b",
    )(ctl, qkv_nat, qkv_nat, qkv_nat, qkv_g1, qkv_g1, qkv_g1, qkv_g2, qkv_g2, qkv_g2)


def _merge_kernel(x_ref, a_ref, b_ref, cq_ref, mem_ref, gm_ref, wkv_ref, gk_ref, g_ref, wg_ref, bg_ref, wb_ref,
                  wo_ref, o_ref, ckv_ref, *, tiles_per_batch):
    half = C_HEADS * C_HEAD_DIM

    @pl.when(pl.program_id(0) % tiles_per_batch == 0)
    def _():
        mn = _rms(mem_ref[0], gm_ref[...]).astype(BF16)
        kv = _dot(mn, wkv_ref[...])
        gk = gk_ref[MEM_K_GAIN_ROW:MEM_K_GAIN_ROW + 1, :]
        ckv_ref[:, :half] = _head_norm(kv[:, :half], gk, "full").astype(BF16)
        ckv_ref[:, half:] = kv[:, half:].astype(BF16)

    x = x_ref[...]
    hb = _rms(x, g_ref[...]).astype(BF16)
    d = x.shape[-1]

    def gated(g, branch):
        gate = jax.nn.sigmoid(_dot(hb, wg_ref[:, g * d:(g + 1) * d]) + bg_ref[:, g * d:(g + 1) * d])
        return gate * _dot(branch, wb_ref[g])

    wide = lambda ref: jnp.concatenate([ref[hh] for hh in range(HEADS_PER_CHUNK)], axis=1)
    acc = gated(0, wide(a_ref)) + gated(1, wide(b_ref))
    heads = []
    for hh in range(C_HEADS):
        cols = slice(hh * C_HEAD_DIM, (hh + 1) * C_HEAD_DIM)
        s = _dot_nt(cq_ref[hh], ckv_ref[:, cols])
        e = jnp.exp2(s - jnp.max(s, axis=-1, keepdims=True))
        inv = 1.0 / jnp.sum(e, axis=-1, keepdims=True)
        heads.append(_dot(e.astype(BF16), ckv_ref[:, half + hh * C_HEAD_DIM:half + (hh + 1) * C_HEAD_DIM]) * inv)
    acc = acc + gated(2, jnp.concatenate(heads, axis=1).astype(BF16))
    o_ref[...] = x + _dot(acc.astype(BF16), wo_ref[...])


def _merge(x2, out_a, out_b, qkv_nat, mem, gm, w_kv, gains, norm_g, w_gate, b_gate, w_branch, w_out, t, tm=512):
    n, d = x2.shape
    n_mem, kv_cols = mem.shape[1], w_kv.shape[1]
    row = lambda w: pl.BlockSpec((tm, w), lambda i: (i, 0))
    heads = lambda chunk: pl.BlockSpec((HEADS_PER_CHUNK, tm, LANES), lambda i: (chunk, i, 0))
    return pl.pallas_call(
        functools.partial(_merge_kernel, tiles_per_batch=t // tm),
        grid=(n // tm,),
        in_specs=[
            row(d), heads(0), heads(0), heads(NAT_HEADS // HEADS_PER_CHUNK - 1),
            pl.BlockSpec((1, n_mem, d), lambda i: (i // (t // tm), 0, 0)),
            _const_spec((1, d)), _const_spec((d, kv_cols)), _const_spec(gains.shape),
            _const_spec((1, d)),
            _const_spec((d, N_BRANCHES * d)),
            _const_spec((1, N_BRANCHES * d)),
            _const_spec((N_BRANCHES, BRANCH_WIDTH, d)),
            _const_spec((d, d)),
        ],
        out_specs=row(d),
        out_shape=jax.ShapeDtypeStruct((n, d), F32),
        scratch_shapes=[pltpu.VMEM((n_mem, kv_cols), BF16)],
        compiler_params=pltpu.CompilerParams(
            dimension_semantics=("arbitrary",), vmem_limit_bytes=VMEM_LIMIT_BYTES),
        name="merge",
    )(x2, out_a, out_b, qkv_nat, mem, gm, w_kv, gains, norm_g, w_gate, b_gate, w_branch, w_out)


def _ffn_kernel(x_ref, g_ref, wg_ref, wu_ref, wd_ref, o_ref, *, chunks):
    x = x_ref[...]
    hb = _rms(x, g_ref[...]).astype(BF16)
    acc = x
    for c0, c1 in chunks:
        gt = _dot(hb, wg_ref[:, c0:c1])
        up = _dot(hb, wu_ref[:, c0:c1])
        acc = acc + _dot((jax.nn.silu(gt) * up).astype(BF16), wd_ref[c0:c1, :])
    o_ref[...] = acc


def _ffn(x2, norm_g, w_gate, w_up, w_down, tm=512, fc=768):
    n, d = x2.shape
    d_ff = w_gate.shape[1]
    chunks = tuple((c, min(c + fc, d_ff)) for c in range(0, d_ff, fc))
    row = pl.BlockSpec((tm, d), lambda i: (i, 0))
    return pl.pallas_call(
        functools.partial(_ffn_kernel, chunks=chunks),
        grid=(n // tm,),
        in_specs=[row, _const_spec((1, d)), _const_spec((d, d_ff)), _const_spec((d, d_ff)),
                  _const_spec((d_ff, d))],
        out_specs=row,
        out_shape=jax.ShapeDtypeStruct((n, d), F32),
        compiler_params=pltpu.CompilerParams(
            dimension_semantics=("arbitrary",), vmem_limit_bytes=VMEM_LIMIT_BYTES),
        name="ffn",
    )(x2, norm_g, w_gate, w_up, w_down)


def _score_ctl(slopes, head_dim, gq_max, gk_max):
    bound = math.sqrt(head_dim) * gq_max * gk_max
    return jnp.concatenate([slopes.reshape(-1),
                            jnp.stack([bound * LOG2E, (bound <= MAX_SAFE_SCORE_BOUND).astype(F32)])])


def kernel(x, mem, norm_mix, w_in, w_gate, b_gate, a_q_norm, a_k_norm, a_lambda_q1, a_lambda_k1, a_lambda_q2,
           a_lambda_k2, a_subln, b_q_norm, b_k_norm, mem_norm, w_mem_kv, c_q_norm, c_k_norm, w_branch, w_out,
           norm_ffn, w_ffn_gate, w_ffn_up, w_ffn_down):
    b, t, d = x.shape
    n = b * t
    n_groups = len(B_GROUPS)
    slopes_a = jnp.exp2(-ALIBI_MAX_BIAS * jnp.arange(1, A_HEADS + 1, dtype=F32) / A_HEADS)
    nb = n_groups * B_HEADS
    slopes_b = jnp.exp2(-ALIBI_MAX_BIAS * jnp.arange(1, nb + 1, dtype=F32) / nb)

    l = 0
    bw = BRANCH_WIDTH
    row = lambda v: v.reshape(1, -1)
    tiled = lambda v: jnp.tile(v, bw // v.shape[0])
    ones = jnp.ones((bw,), F32)
    a_qs = A_HEAD_DIM ** -0.5 * LOG2E
    b_qs = B_HEAD_DIM ** -0.5 * LOG2E
    c_qs = C_HEAD_DIM ** -0.5 * LOG2E
    gains = jnp.stack([tiled(a_q_norm[l]) * a_qs, tiled(a_k_norm[l]), ones, tiled(b_q_norm[l]) * b_qs,
                       tiled(b_k_norm[l]), tiled(c_q_norm[l]) * c_qs, tiled(c_k_norm[l])])
    pad = lambda v: jnp.pad(v, (0, B_HEAD_DIM - v.shape[0]))
    gmax = jnp.max(jnp.abs(jnp.stack([pad(a_q_norm[l]), pad(a_k_norm[l]), b_q_norm[l], b_k_norm[l]])), axis=1)

    qkv_nat, qkv_g1, qkv_g2 = _project(x, row(norm_mix[l]), w_in[l], gains)
    qkv_nat = qkv_nat.reshape(NAT_HEADS, n, LANES)
    qkv_g1 = qkv_g1.reshape(GRP_HEADS, n, LANES)
    qkv_g2 = qkv_g2.reshape(GRP_HEADS, n, LANES)

    out_a = _attn_a(qkv_nat, _score_ctl(slopes_a, A_HEAD_DIM, gmax[0], gmax[1]),
                    row(a_lambda_q1[l]), row(a_lambda_k1[l]), row(a_lambda_q2[l]), row(a_lambda_k2[l]),
                    row(a_subln[l]), b, t)
    out_b = _attn_b(qkv_nat, qkv_g1, qkv_g2, _score_ctl(slopes_b, B_HEAD_DIM, gmax[2], gmax[3]), b, t)
    x2 = x.reshape(n, d)
    x2 = _merge(x2, out_a, out_b, qkv_nat, mem, row(mem_norm[l]), w_mem_kv[l], gains,
                row(norm_mix[l]), w_gate[l], row(b_gate[l]), w_branch[l], w_out[l], t)
    x2 = _ffn(x2, row(norm_ffn[l]), w_ffn_gate[l], w_ffn_up[l], w_ffn_down[l])
    return x2.reshape(b, t, d)
```

```python
import functools
import math

import jax
import jax.numpy as jnp
from jax import lax
from jax.experimental import pallas as pl
from jax.experimental.pallas import tpu as pltpu

F32 = jnp.float32
BF16 = jnp.bfloat16

D_MODEL = 1024
A_HEADS = 4
A_HEAD_DIM = 64
A_V_DIM = 2 * A_HEAD_DIM
B_GROUPS = ((128, 1), (512, 4), (2048, 16))
B_HEADS = 4
B_HEAD_DIM = 128
C_HEADS = 4
C_HEAD_DIM = 128
BRANCH_WIDTH = 512
N_BRANCHES = 3
EPS = 1e-6
ALIBI_MAX_BIAS = 8.0
LAMBDA_INIT = 0.8 - 0.6 * math.exp(-0.3 * 0)

LANES = 128
VMEM_LIMIT_BYTES = 56 * 1024 * 1024

NAT_COLS = 7 * BRANCH_WIDTH
GRP_COLS = 3 * BRANCH_WIDTH
HEADS_PER_CHUNK = BRANCH_WIDTH // LANES
NAT_HEADS = NAT_COLS // LANES
GRP_HEADS = GRP_COLS // LANES
NAT_NORMS = ("half", "half", "none", "full", "full", "none", "full")
GRP_COL_OFFS = (3 * BRANCH_WIDTH, 6 * BRANCH_WIDTH, 9 * BRANCH_WIDTH)
NAT_COL_OFFS = (0, BRANCH_WIDTH, 2 * BRANCH_WIDTH) + GRP_COL_OFFS + (12 * BRANCH_WIDTH,)
GRP_NORMS = ("full", "full", "none")
NAT_GAIN_ROWS = (0, 1, 2, 3, 4, 2, 5)
GRP_GAIN_ROWS = (3, 4, 2)
MEM_K_GAIN_ROW = 6
NEG_BIG = -1e30
LOG2E = 1.4426950408889634
MAX_SAFE_SCORE_BOUND = 40.0


def _rms(x, gain):
    return x * lax.rsqrt(jnp.mean(x * x, axis=-1, keepdims=True) + EPS) * gain


def _dot(a, b):
    return jnp.dot(a, b, preferred_element_type=F32)


def _dot_nt(a, b):
    return lax.dot_general(a, b, (((1,), (1,)), ((), ())), preferred_element_type=F32)


def _const_spec(shape):
    nd = len(shape)
    return pl.BlockSpec(shape, lambda *_: (0,) * nd, pipeline_mode=pl.Buffered(1))


def _head_norm(y, gain, kind):
    if kind == "none":
        return y
    lo_mask = lax.broadcasted_iota(jnp.int32, (1, LANES), 1) < A_HEAD_DIM
    cols = []
    for c in range(0, y.shape[1], LANES):
        z = y[:, c:c + LANES]
        sq = z * z
        s_all = jnp.sum(sq, axis=-1, keepdims=True)
        if kind == "full":
            ms = s_all * (1.0 / LANES)
        else:
            s_lo = jnp.sum(jnp.where(lo_mask, sq, 0.0), axis=-1, keepdims=True)
            ms = jnp.where(lo_mask, s_lo, s_all - s_lo) * (1.0 / A_HEAD_DIM)
        cols.append(z * lax.rsqrt(ms + EPS))
    return jnp.concatenate(cols, axis=1) * gain


def _proj_kernel(x_ref, g_ref, w_ref, gains_ref, on_ref, o1_ref, o2_ref, slab_ref, p4_ref, hp_ref, *, tm):
    h = _rms(x_ref[0], g_ref[...])
    hb = h.astype(BF16)
    bw = BRANCH_WIDTH
    for ci, kind in enumerate(NAT_NORMS):
        c0 = NAT_COL_OFFS[ci]
        gain = gains_ref[NAT_GAIN_ROWS[ci]:NAT_GAIN_ROWS[ci] + 1, :]
        res = _head_norm(_dot(hb, w_ref[:, c0:c0 + bw]), gain, kind).astype(BF16)
        for hh in range(HEADS_PER_CHUNK):
            on_ref[ci * HEADS_PER_CHUNK + hh, 0] = res[:, hh * LANES:(hh + 1) * LANES]
    n_slabs = D_MODEL // LANES
    for s in range(n_slabs):
        slab_ref[s] = h[:, s * LANES:(s + 1) * LANES]
    for g, dil, o_ref in ((1, 4, o1_ref), (2, 16, o2_ref)):
        n = tm // dil
        if dil == 4:
            for r in range(dil):
                for s in range(n_slabs):
                    rows = slab_ref[s, pl.ds(r, n, stride=dil), :]
                    p4_ref[s, r * n:(r + 1) * n, :] = rows
                    hp_ref[0, r * n:(r + 1) * n, s * LANES:(s + 1) * LANES] = rows.astype(BF16)
        else:
            n4 = tm // 4
            for r4 in range(4):
                for q in range(4):
                    r = r4 + 4 * q
                    for s in range(n_slabs):
                        hp_ref[1, r * n:(r + 1) * n, s * LANES:(s + 1) * LANES] = (
                            p4_ref[s, pl.ds(r4 * n4 + q, n, stride=4), :].astype(BF16))
        hp = hp_ref[g - 1]
        for ci, kind in enumerate(GRP_NORMS):
            c0 = GRP_COL_OFFS[ci] + g * bw
            gain = gains_ref[GRP_GAIN_ROWS[ci]:GRP_GAIN_ROWS[ci] + 1, :]
            res = _head_norm(_dot(hp, w_ref[:, c0:c0 + bw]), gain, kind).astype(BF16)
            for hh in range(HEADS_PER_CHUNK):
                for r in range(dil):
                    o_ref[ci * HEADS_PER_CHUNK + hh, 0, r] = res[r * n:(r + 1) * n, hh * LANES:(hh + 1) * LANES]


def _project(x, norm_g, w_in, gains, tm=512):
    b, t, d = x.shape
    grid = (b, t // tm)
    return pl.pallas_call(
        functools.partial(_proj_kernel, tm=tm),
        grid=grid,
        in_specs=[
            pl.BlockSpec((1, tm, d), lambda i, j: (i, j, 0)),
            _const_spec((1, d)),
            _const_spec(w_in.shape),
            _const_spec(gains.shape),
        ],
        out_specs=[
            pl.BlockSpec((NAT_HEADS, 1, tm, LANES), lambda i, j: (0, i, j, 0)),
            pl.BlockSpec((GRP_HEADS, 1, 4, tm // 4, LANES), lambda i, j: (0, i, 0, j, 0)),
            pl.BlockSpec((GRP_HEADS, 1, 16, tm // 16, LANES), lambda i, j: (0, i, 0, j, 0)),
        ],
        out_shape=[
            jax.ShapeDtypeStruct((NAT_HEADS, b, t, LANES), BF16),
            jax.ShapeDtypeStruct((GRP_HEADS, b, 4, t // 4, LANES), BF16),
            jax.ShapeDtypeStruct((GRP_HEADS, b, 16, t // 16, LANES), BF16),
        ],
        scratch_shapes=[
            pltpu.VMEM((d // LANES, tm, LANES), F32),
            pltpu.VMEM((d // LANES, tm, LANES), F32),
            pltpu.VMEM((2, tm, d), BF16),
        ],
        compiler_params=pltpu.CompilerParams(
            dimension_semantics=("arbitrary", "arbitrary"), vmem_limit_bytes=VMEM_LIMIT_BYTES),
        name="proj",
    )(x, norm_g, w_in, gains)


def _attn_a_kernel(ctl_ref, q_ref, k_ref, v_ref, lq1_ref, lk1_ref, lq2_ref, lk2_ref,
                   sub_ref, o_ref, vt_ref, tab_ref, et_ref, ot_ref, l_ref, *, t, qb, kc, inflight):
    h = pl.program_id(0)
    nblk = t // qb
    slope2 = ctl_ref[h] * LOG2E
    shift = ctl_ref[A_HEADS]
    fast = ctl_ref[A_HEADS + 1] > 0.5
    lo_mask = lax.broadcasted_iota(jnp.int32, (1, LANES), 1) < A_HEAD_DIM

    lam = (jnp.exp(jnp.sum(lq1_ref[...] * lk1_ref[...], keepdims=True))
           - jnp.exp(jnp.sum(lq2_ref[...] * lk2_ref[...], keepdims=True)) + LAMBDA_INIT)

    vt_ref[...] = v_ref[...].T

    @pl.when(pl.program_id(1) == 0)
    def _():
        cc = lax.broadcasted_iota(jnp.int32, (2 * t - qb, qb), 0)
        il = lax.broadcasted_iota(jnp.int32, (2 * t - qb, qb), 1)
        tab_ref[...] = (-slope2 * jnp.abs(cc - (t - qb) - il).astype(F32)
                        - jnp.where(fast, shift, 0.0))

    def fold8(x, op):
        return op(x.reshape(x.shape[0] // 8, 8, x.shape[1]), axis=0)

    def block(j, slot, exact_max):
        q = q_ref[pl.ds(pl.multiple_of(j * qb, qb), qb), :]
        zero = jnp.zeros_like(q)
        q2 = jnp.concatenate([jnp.where(lo_mask, q, zero), jnp.where(lo_mask, zero, q)], axis=0)
        off = pl.multiple_of(t - qb - j * qb, qb)

        def scores(c):
            s = _dot_nt(k_ref[c * kc:(c + 1) * kc, :], q2)
            bias = tab_ref[pl.ds(off + c * kc, kc), :]
            return s + jnp.concatenate([bias, bias], axis=1)

        m = None
        if exact_max:
            for c in range(t // kc):
                cm = fold8(scores(c), jnp.max)
                m = cm if m is None else jnp.maximum(m, cm)
            m = jnp.max(m, axis=0, keepdims=True)
        acc = None
        for c in range(t // kc):
            s = scores(c)
            e = jnp.exp2(s - m if exact_max else s)
            et_ref[slot, c * kc:(c + 1) * kc, :] = e.astype(BF16)
            part = fold8(e, jnp.sum)
            acc = part if acc is None else acc + part
        l_ref[pl.ds(j, 1), :] = jnp.sum(acc, axis=0, keepdims=True)
        ot_ref[j] = _dot(vt_ref[...], et_ref[slot])

    def finish(j):
        ot = ot_ref[j]
        inv = 1.0 / l_ref[pl.ds(j, 1), :]
        o = ot[:, :qb] * inv[:, :qb] - ot[:, qb:] * (lam * inv[:, qb:])
        o = _rms(o.T, sub_ref[...]) * (1.0 - LAMBDA_INIT)
        o_ref[pl.ds(pl.multiple_of(j * qb, qb), qb), :] = o.astype(BF16)

    def run(exact_max):
        for s in range(inflight):
            block(s, s, exact_max)

        def step(i, carry):
            for s in range(inflight):
                finish(inflight * (i - 1) + s)
            for s in range(inflight):
                block(inflight * i + s, s, exact_max)
            return carry
        lax.fori_loop(1, nblk // inflight, step, 0)
        for s in range(inflight):
            finish(nblk - inflight + s)

    @pl.when(fast)
    def _():
        run(False)

    @pl.when(jnp.logical_not(fast))
    def _():
        run(True)


def _attn_a(qkv_nat, ctl, lq1, lk1, lq2, lk2, subln, b, t, qb=256, kc=512, inflight=4):
    n = b * t
    nblk = t // qb
    vec = lambda w: _const_spec((1, w))
    return pl.pallas_call(
        functools.partial(_attn_a_kernel, t=t, qb=qb, kc=kc, inflight=inflight),
        grid=(A_HEADS, b),
        in_specs=[
            pl.BlockSpec(memory_space=pltpu.SMEM),
            pl.BlockSpec((None, t, LANES), lambda h, i: (h, i, 0)),
            pl.BlockSpec((None, t, LANES), lambda h, i: (A_HEADS + h, i, 0)),
            pl.BlockSpec((None, t, LANES), lambda h, i: (2 * A_HEADS + h, i, 0)),
            vec(A_HEAD_DIM), vec(A_HEAD_DIM), vec(A_HEAD_DIM), vec(A_HEAD_DIM),
            vec(A_V_DIM),
        ],
        out_specs=pl.BlockSpec((None, t, LANES), lambda h, i: (h, i, 0)),
        out_shape=jax.ShapeDtypeStruct((A_HEADS, n, LANES), BF16),
        scratch_shapes=[
            pltpu.VMEM((LANES, t), BF16),
            pltpu.VMEM((2 * t - qb, qb), F32),
            pltpu.VMEM((inflight, t, 2 * qb), BF16),
            pltpu.VMEM((nblk, LANES, 2 * qb), F32),
            pltpu.VMEM((nblk, 2 * qb), F32),
        ],
        compiler_params=pltpu.CompilerParams(
            dimension_semantics=("arbitrary", "arbitrary"), vmem_limit_bytes=VMEM_LIMIT_BYTES),
        name="attn_a",
    )(ctl, qkv_nat, qkv_nat, qkv_nat, lq1, lk1, lq2, lk2, subln)


def _attn_b_kernel(ctl_ref, q0_ref, k0_ref, v0_ref, q1_ref, k1_ref, v1_ref, q2_ref, k2_ref, v2_ref,
                   o_ref, acc_ref, den_ref, kt_ref, *, t, qb):
    h = pl.program_id(1)
    n_ctl = len(B_GROUPS) * B_HEADS
    shift = ctl_ref[n_ctl]
    fast = ctl_ref[n_ctl + 1] > 0.5
    refs = ((q0_ref, k0_ref, v0_ref), (q1_ref, k1_ref, v1_ref), (q2_ref, k2_ref, v2_ref))

    def blocks(g):
        window, dil = B_GROUPS[g]
        n_side = window // (2 * dil)
        sub = t // dil
        wk = min(2 * qb, sub)
        for r in range(dil):
            for j in range(sub // qb):
                i0 = j * qb
                ws = min(max(i0 - n_side, 0), sub - wk)
                yield r, i0, r * sub + i0, r * sub + ws, wk, i0 - ws, n_side

    def bias_table(g, wk, offset, n_side, sub_shift):
        slope2 = ctl_ref[g * B_HEADS + h] * (float(B_GROUPS[g][1]) * LOG2E)
        ql = lax.broadcasted_iota(jnp.int32, (qb, wk), 0)
        kl = lax.broadcasted_iota(jnp.int32, (qb, wk), 1)
        dist = jnp.abs(kl - ql - offset)
        return jnp.where(dist <= n_side, -slope2 * dist.astype(F32) - sub_shift, NEG_BIG)

    def store_rows(ref, g, r, i0, val):
        dil = B_GROUPS[g][1]
        if dil == 1:
            ref[g, i0:i0 + qb, :] = val
        else:
            ref[g, pl.ds(r + dil * i0, qb, stride=dil), :] = val

    @pl.when(fast)
    def _():
        half = qb // 2
        for g in (1, 2, 0):
            q_ref, k_ref, v_ref = refs[g]
            kt_ref[0] = k_ref[...].T
            if any(krow % qb for _, _, _, krow, _, _, _ in blocks(g)):
                kt_ref[1] = jnp.concatenate([k_ref[half:, :], k_ref[:half, :]], axis=0).T
            tables = {}
            for r, i0, qrow, krow, wk, offset, n_side in blocks(g):
                if offset not in tables:
                    tables[offset] = bias_table(g, wk, offset, n_side, shift)
                if krow % qb == 0:
                    kwt = kt_ref[0, :, krow:krow + wk]
                else:
                    kwt = kt_ref[1, :, krow - half:krow - half + wk]
                s = _dot(q_ref[qrow:qrow + qb, :], kwt)
                e = jnp.exp2(s + tables[offset])
                den = jnp.sum(e, axis=-1, keepdims=True)
                acc = _dot(e.astype(BF16), v_ref[krow:krow + wk, :])
                if B_GROUPS[g][1] == 1:
                    rows = slice(i0, i0 + qb)
                    acc = acc + acc_ref[1, rows, :] + acc_ref[2, rows, :]
                    den = den + den_ref[1, rows, :] + den_ref[2, rows, :]
                    o_ref[rows, :] = (acc / den).astype(BF16)
                else:
                    store_rows(acc_ref, g, r, i0, acc)
                    store_rows(den_ref, g, r, i0, jnp.broadcast_to(den, (qb, LANES)))

    @pl.when(jnp.logical_not(fast))
    def _():
        for g in range(len(B_GROUPS)):
            q_ref, k_ref, v_ref = refs[g]
            tables = {}
            for r, i0, qrow, krow, wk, offset, n_side in blocks(g):
                if offset not in tables:
                    tables[offset] = bias_table(g, wk, offset, n_side, 0.0)
                s = _dot_nt(q_ref[qrow:qrow + qb, :], k_ref[krow:krow + wk, :]) + tables[offset]
                m = jnp.max(s, axis=-1, keepdims=True)
                e = jnp.exp2(s - m)
                l = jnp.sum(e, axis=-1, keepdims=True)
                o = _dot(e.astype(BF16), v_ref[krow:krow + wk, :]) * (1.0 / l)
                store_rows(acc_ref, g, r, i0, o)
                store_rows(den_ref, g, r, i0, jnp.broadcast_to(m + jnp.log2(l), (qb, LANES)))
        l0, l1, l2 = den_ref[0], den_ref[1], den_ref[2]
        m = jnp.maximum(jnp.maximum(l0, l1), l2)
        e0, e1, e2 = jnp.exp2(l0 - m), jnp.exp2(l1 - m), jnp.exp2(l2 - m)
        o_ref[...] = ((e0 * acc_ref[0] + e1 * acc_ref[1] + e2 * acc_ref[2]) / (e0 + e1 + e2)).astype(BF16)


def _attn_b(qkv_nat, qkv_g1, qkv_g2, ctl, b, t, qb=128):
    n = b * t
    blk = lambda off: pl.BlockSpec((None, t, LANES), lambda i, h: (off + h, i, 0))
    return pl.pallas_call(
        functools.partial(_attn_b_kernel, t=t, qb=qb),
        grid=(b, B_HEADS),
        in_specs=[
            pl.BlockSpec(memory_space=pltpu.SMEM),
            blk(3 * HEADS_PER_CHUNK), blk(4 * HEADS_PER_CHUNK), blk(5 * HEADS_PER_CHUNK),
            blk(0), blk(HEADS_PER_CHUNK), blk(2 * HEADS_PER_CHUNK),
            blk(0), blk(HEADS_PER_CHUNK), blk(2 * HEADS_PER_CHUNK),
        ],
        out_specs=pl.BlockSpec((None, t, LANES), lambda i, h: (h, i, 0)),
        out_shape=jax.ShapeDtypeStruct((B_HEADS, n, LANES), BF16),
        scratch_shapes=[
            pltpu.VMEM((len(B_GROUPS), t, LANES), F32),
            pltpu.VMEM((len(B_GROUPS), t, LANES), F32),
            pltpu.VMEM((2, LANES, t), BF16),
        ],
        compiler_params=pltpu.CompilerParams(
            dimension_semantics=("arbitrary", "arbitrary"), vmem_limit_bytes=VMEM_LIMIT_BYTES),
        name="attn_b",
    )(ctl, qkv_nat, qkv_nat, qkv_nat, qkv_g1, qkv_g1, qkv_g1, qkv_g2, qkv_g2, qkv_g2)


def _merge_kernel(x_ref, a_ref, b_ref, cq_ref, mem_ref, gm_ref, wkv_ref, gk_ref, g_ref, wg_ref, bg_ref, wb_ref,
                  wo_ref, o_ref, ckv_ref, *, tiles_per_batch):
    half = C_HEADS * C_HEAD_DIM

    @pl.when(pl.program_id(0) % tiles_per_batch == 0)
    def _():
        mn = _rms(mem_ref[0], gm_ref[...]).astype(BF16)
        kv = _dot(mn, wkv_ref[...])
        gk = gk_ref[MEM_K_GAIN_ROW:MEM_K_GAIN_ROW + 1, :]
        ckv_ref[:, :half] = _head_norm(kv[:, :half], gk, "full").astype(BF16)
        ckv_ref[:, half:] = kv[:, half:].astype(BF16)

    x = x_ref[...]
    hb = _rms(x, g_ref[...]).astype(BF16)
    d = x.shape[-1]

    def gated(g, branch):
        z = _dot(hb, wg_ref[:, g * d:(g + 1) * d]) + bg_ref[:, g * d:(g + 1) * d]
        gate = 0.5 * jnp.tanh(0.5 * z) + 0.5
        return gate * _dot(branch, wb_ref[g])

    wide = lambda ref: jnp.concatenate([ref[hh] for hh in range(HEADS_PER_CHUNK)], axis=1)
    acc = gated(0, wide(a_ref)) + gated(1, wide(b_ref))
    heads = []
    for hh in range(C_HEADS):
        cols = slice(hh * C_HEAD_DIM, (hh + 1) * C_HEAD_DIM)
        s = _dot_nt(cq_ref[hh], ckv_ref[:, cols])
        e = jnp.exp2(s - jnp.max(s, axis=-1, keepdims=True))
        inv = 1.0 / jnp.sum(e, axis=-1, keepdims=True)
        heads.append(_dot(e.astype(BF16), ckv_ref[:, half + hh * C_HEAD_DIM:half + (hh + 1) * C_HEAD_DIM]) * inv)
    acc = acc + gated(2, jnp.concatenate(heads, axis=1).astype(BF16))
    o_ref[...] = x + _dot(acc.astype(BF16), wo_ref[...])


def _merge(x2, out_a, out_b, qkv_nat, mem, gm, w_kv, gains, norm_g, w_gate, b_gate, w_branch, w_out, t, tm=512):
    n, d = x2.shape
    n_mem, kv_cols = mem.shape[1], w_kv.shape[1]
    row = lambda w: pl.BlockSpec((tm, w), lambda i: (i, 0))
    heads = lambda chunk: pl.BlockSpec((HEADS_PER_CHUNK, tm, LANES), lambda i: (chunk, i, 0))
    return pl.pallas_call(
        functools.partial(_merge_kernel, tiles_per_batch=t // tm),
        grid=(n // tm,),
        in_specs=[
            row(d), heads(0), heads(0), heads(NAT_HEADS // HEADS_PER_CHUNK - 1),
            pl.BlockSpec((1, n_mem, d), lambda i: (i // (t // tm), 0, 0)),
            _const_spec((1, d)), _const_spec((d, kv_cols)), _const_spec(gains.shape),
            _const_spec((1, d)),
            _const_spec((d, N_BRANCHES * d)),
            _const_spec((1, N_BRANCHES * d)),
            _const_spec((N_BRANCHES, BRANCH_WIDTH, d)),
            _const_spec((d, d)),
        ],
        out_specs=row(d),
        out_shape=jax.ShapeDtypeStruct((n, d), F32),
        scratch_shapes=[pltpu.VMEM((n_mem, kv_cols), BF16)],
        compiler_params=pltpu.CompilerParams(
            dimension_semantics=("arbitrary",), vmem_limit_bytes=VMEM_LIMIT_BYTES),
        name="merge",
    )(x2, out_a, out_b, qkv_nat, mem, gm, w_kv, gains, norm_g, w_gate, b_gate, w_branch, w_out)


def _ffn_kernel(x_ref, g_ref, wg_ref, wu_ref, wd_ref, o_ref, *, chunks):
    x = x_ref[...]
    hb = _rms(x, g_ref[...]).astype(BF16)
    acc = x
    for c0, c1 in chunks:
        gt = _dot(hb, wg_ref[:, c0:c1])
        up = _dot(hb, wu_ref[:, c0:c1])
        acc = acc + _dot((jax.nn.silu(gt) * up).astype(BF16), wd_ref[c0:c1, :])
    o_ref[...] = acc


def _ffn(x2, norm_g, w_gate, w_up, w_down, tm=512, fc=768):
    n, d = x2.shape
    d_ff = w_gate.shape[1]
    chunks = tuple((c, min(c + fc, d_ff)) for c in range(0, d_ff, fc))
    row = pl.BlockSpec((tm, d), lambda i: (i, 0))
    return pl.pallas_call(
        functools.partial(_ffn_kernel, chunks=chunks),
        grid=(n // tm,),
        in_specs=[row, _const_spec((1, d)), _const_spec((d, d_ff)), _const_spec((d, d_ff)),
                  _const_spec((d_ff, d))],
        out_specs=row,
        out_shape=jax.ShapeDtypeStruct((n, d), F32),
        compiler_params=pltpu.CompilerParams(
            dimension_semantics=("arbitrary",), vmem_limit_bytes=VMEM_LIMIT_BYTES),
        name="ffn",
    )(x2, norm_g, w_gate, w_up, w_down)


def _score_ctl(slopes, head_dim, gq_max, gk_max):
    bound = math.sqrt(head_dim) * gq_max * gk_max
    return jnp.concatenate([slopes.reshape(-1),
                            jnp.stack([bound * LOG2E, (bound <= MAX_SAFE_SCORE_BOUND).astype(F32)])])


def kernel(x, mem, norm_mix, w_in, w_gate, b_gate, a_q_norm, a_k_norm, a_lambda_q1, a_lambda_k1, a_lambda_q2,
           a_lambda_k2, a_subln, b_q_norm, b_k_norm, mem_norm, w_mem_kv, c_q_norm, c_k_norm, w_branch, w_out,
           norm_ffn, w_ffn_gate, w_ffn_up, w_ffn_down):
    b, t, d = x.shape
    n = b * t
    n_groups = len(B_GROUPS)
    slopes_a = jnp.exp2(-ALIBI_MAX_BIAS * jnp.arange(1, A_HEADS + 1, dtype=F32) / A_HEADS)
    nb = n_groups * B_HEADS
    slopes_b = jnp.exp2(-ALIBI_MAX_BIAS * jnp.arange(1, nb + 1, dtype=F32) / nb)

    l = 0
    bw = BRANCH_WIDTH
    row = lambda v: v.reshape(1, -1)
    tiled = lambda v: jnp.tile(v, bw // v.shape[0])
    ones = jnp.ones((bw,), F32)
    a_qs = A_HEAD_DIM ** -0.5 * LOG2E
    b_qs = B_HEAD_DIM ** -0.5 * LOG2E
    c_qs = C_HEAD_DIM ** -0.5 * LOG2E
    gains = jnp.stack([tiled(a_q_norm[l]) * a_qs, tiled(a_k_norm[l]), ones, tiled(b_q_norm[l]) * b_qs,
                       tiled(b_k_norm[l]), tiled(c_q_norm[l]) * c_qs, tiled(c_k_norm[l])])
    pad = lambda v: jnp.pad(v, (0, B_HEAD_DIM - v.shape[0]))
    gmax = jnp.max(jnp.abs(jnp.stack([pad(a_q_norm[l]), pad(a_k_norm[l]), b_q_norm[l], b_k_norm[l]])), axis=1)

    qkv_nat, qkv_g1, qkv_g2 = _project(x, row(norm_mix[l]), w_in[l], gains)
    qkv_nat = qkv_nat.reshape(NAT_HEADS, n, LANES)
    qkv_g1 = qkv_g1.reshape(GRP_HEADS, n, LANES)
    qkv_g2 = qkv_g2.reshape(GRP_HEADS, n, LANES)

    out_a = _attn_a(qkv_nat, _score_ctl(slopes_a, A_HEAD_DIM, gmax[0], gmax[1]),
                    row(a_lambda_q1[l]), row(a_lambda_k1[l]), row(a_lambda_q2[l]), row(a_lambda_k2[l]),
                    row(a_subln[l]), b, t)
    out_b = _attn_b(qkv_nat, qkv_g1, qkv_g2, _score_ctl(slopes_b, B_HEAD_DIM, gmax[2], gmax[3]), b, t)
    x2 = x.reshape(n, d)
    x2 = _merge(x2, out_a, out_b, qkv_nat, mem, row(mem_norm[l]), w_mem_kv[l], gains,
                row(norm_mix[l]), w_gate[l], row(b_gate[l]), w_branch[l], w_out[l], t)
    x2 = _ffn(x2, row(norm_ffn[l]), w_ffn_gate[l], w_ffn_up[l], w_ffn_down[l])
    return x2.reshape(b, t, d)
```

```python
import functools
import math

import jax
import jax.numpy as jnp
from jax import lax
from jax.experimental import pallas as pl
from jax.experimental.pallas import tpu as pltpu

F32 = jnp.float32
BF16 = jnp.bfloat16

D_MODEL = 1024
A_HEADS = 4
A_HEAD_DIM = 64
A_V_DIM = 2 * A_HEAD_DIM
B_GROUPS = ((128, 1), (512, 4), (2048, 16))
B_HEADS = 4
B_HEAD_DIM = 128
C_HEADS = 4
C_HEAD_DIM = 128
BRANCH_WIDTH = 512
N_BRANCHES = 3
EPS = 1e-6
ALIBI_MAX_BIAS = 8.0
LAMBDA_INIT = 0.8 - 0.6 * math.exp(-0.3 * 0)

LANES = 128
VMEM_LIMIT_BYTES = 56 * 1024 * 1024

NAT_COLS = 7 * BRANCH_WIDTH
GRP_COLS = 3 * BRANCH_WIDTH
HEADS_PER_CHUNK = BRANCH_WIDTH // LANES
NAT_HEADS = NAT_COLS // LANES
GRP_HEADS = GRP_COLS // LANES
NAT_NORMS = ("half", "half", "none", "full", "full", "none", "full")
GRP_COL_OFFS = (3 * BRANCH_WIDTH, 6 * BRANCH_WIDTH, 9 * BRANCH_WIDTH)
NAT_COL_OFFS = (0, BRANCH_WIDTH, 2 * BRANCH_WIDTH) + GRP_COL_OFFS + (12 * BRANCH_WIDTH,)
GRP_NORMS = ("full", "full", "none")
NAT_GAIN_ROWS = (0, 1, 2, 3, 4, 2, 5)
GRP_GAIN_ROWS = (3, 4, 2)
MEM_K_GAIN_ROW = 6
NEG_BIG = -1e30
LOG2E = 1.4426950408889634
MAX_SAFE_SCORE_BOUND = 40.0


def _rms(x, gain):
    return x * lax.rsqrt(jnp.mean(x * x, axis=-1, keepdims=True) + EPS) * gain


def _dot(a, b):
    return jnp.dot(a, b, preferred_element_type=F32)


def _dot_nt(a, b):
    return lax.dot_general(a, b, (((1,), (1,)), ((), ())), preferred_element_type=F32)


def _const_spec(shape):
    nd = len(shape)
    return pl.BlockSpec(shape, lambda *_: (0,) * nd, pipeline_mode=pl.Buffered(1))


def _head_norm(y, gain, kind):
    if kind == "none":
        return y
    lo_mask = lax.broadcasted_iota(jnp.int32, (1, LANES), 1) < A_HEAD_DIM
    cols = []
    for c in range(0, y.shape[1], LANES):
        z = y[:, c:c + LANES]
        sq = z * z
        s_all = jnp.sum(sq, axis=-1, keepdims=True)
        if kind == "full":
            ms = s_all * (1.0 / LANES)
        else:
            s_lo = jnp.sum(jnp.where(lo_mask, sq, 0.0), axis=-1, keepdims=True)
            ms = jnp.where(lo_mask, s_lo, s_all - s_lo) * (1.0 / A_HEAD_DIM)
        cols.append(z * lax.rsqrt(ms + EPS))
    return jnp.concatenate(cols, axis=1) * gain


def _proj_kernel(x_ref, g_ref, w_ref, gains_ref, on_ref, o1_ref, o2_ref, slab_ref, p4_ref, hp_ref, *, tm):
    h = _rms(x_ref[0], g_ref[...])
    hb = h.astype(BF16)
    bw = BRANCH_WIDTH
    for ci, kind in enumerate(NAT_NORMS):
        c0 = NAT_COL_OFFS[ci]
        gain = gains_ref[NAT_GAIN_ROWS[ci]:NAT_GAIN_ROWS[ci] + 1, :]
        res = _head_norm(_dot(hb, w_ref[:, c0:c0 + bw]), gain, kind).astype(BF16)
        for hh in range(HEADS_PER_CHUNK):
            on_ref[ci * HEADS_PER_CHUNK + hh, 0] = res[:, hh * LANES:(hh + 1) * LANES]
    n_slabs = D_MODEL // LANES
    for s in range(n_slabs):
        slab_ref[s] = h[:, s * LANES:(s + 1) * LANES]
    for g, dil, o_ref in ((1, 4, o1_ref), (2, 16, o2_ref)):
        n = tm // dil
        if dil == 4:
            for r in range(dil):
                for s in range(n_slabs):
                    rows = slab_ref[s, pl.ds(r, n, stride=dil), :]
                    p4_ref[s, r * n:(r + 1) * n, :] = rows
                    hp_ref[0, r * n:(r + 1) * n, s * LANES:(s + 1) * LANES] = rows.astype(BF16)
        else:
            n4 = tm // 4
            for r4 in range(4):
                for q in range(4):
                    r = r4 + 4 * q
                    for s in range(n_slabs):
                        hp_ref[1, r * n:(r + 1) * n, s * LANES:(s + 1) * LANES] = (
                            p4_ref[s, pl.ds(r4 * n4 + q, n, stride=4), :].astype(BF16))
        hp = hp_ref[g - 1]
        for ci, kind in enumerate(GRP_NORMS):
            c0 = GRP_COL_OFFS[ci] + g * bw
            gain = gains_ref[GRP_GAIN_ROWS[ci]:GRP_GAIN_ROWS[ci] + 1, :]
            res = _head_norm(_dot(hp, w_ref[:, c0:c0 + bw]), gain, kind).astype(BF16)
            for hh in range(HEADS_PER_CHUNK):
                for r in range(dil):
                    o_ref[ci * HEADS_PER_CHUNK + hh, 0, r] = res[r * n:(r + 1) * n, hh * LANES:(hh + 1) * LANES]


def _project(x, norm_g, w_in, gains, tm=512):
    b, t, d = x.shape
    grid = (b, t // tm)
    return pl.pallas_call(
        functools.partial(_proj_kernel, tm=tm),
        grid=grid,
        in_specs=[
            pl.BlockSpec((1, tm, d), lambda i, j: (i, j, 0)),
            _const_spec((1, d)),
            _const_spec(w_in.shape),
            _const_spec(gains.shape),
        ],
        out_specs=[
            pl.BlockSpec((NAT_HEADS, 1, tm, LANES), lambda i, j: (0, i, j, 0)),
            pl.BlockSpec((GRP_HEADS, 1, 4, tm // 4, LANES), lambda i, j: (0, i, 0, j, 0)),
            pl.BlockSpec((GRP_HEADS, 1, 16, tm // 16, LANES), lambda i, j: (0, i, 0, j, 0)),
        ],
        out_shape=[
            jax.ShapeDtypeStruct((NAT_HEADS, b, t, LANES), BF16),
            jax.ShapeDtypeStruct((GRP_HEADS, b, 4, t // 4, LANES), BF16),
            jax.ShapeDtypeStruct((GRP_HEADS, b, 16, t // 16, LANES), BF16),
        ],
        scratch_shapes=[
            pltpu.VMEM((d // LANES, tm, LANES), F32),
            pltpu.VMEM((d // LANES, tm, LANES), F32),
            pltpu.VMEM((2, tm, d), BF16),
        ],
        compiler_params=pltpu.CompilerParams(
            dimension_semantics=("arbitrary", "arbitrary"), vmem_limit_bytes=VMEM_LIMIT_BYTES),
        name="proj",
    )(x, norm_g, w_in, gains)


def _attn_a_kernel(ctl_ref, q_ref, k_ref, v_ref, lq1_ref, lk1_ref, lq2_ref, lk2_ref,
                   sub_ref, o_ref, vt_ref, tab_ref, et_ref, ot_ref, l_ref, *, t, qb, kc, inflight):
    h = pl.program_id(0)
    nblk = t // qb
    slope2 = ctl_ref[h] * LOG2E
    shift = ctl_ref[A_HEADS]
    fast = ctl_ref[A_HEADS + 1] > 0.5
    lo_mask = lax.broadcasted_iota(jnp.int32, (1, LANES), 1) < A_HEAD_DIM

    lam = (jnp.exp(jnp.sum(lq1_ref[...] * lk1_ref[...], keepdims=True))
           - jnp.exp(jnp.sum(lq2_ref[...] * lk2_ref[...], keepdims=True)) + LAMBDA_INIT)

    vt_ref[...] = v_ref[...].T

    @pl.when(pl.program_id(1) == 0)
    def _():
        cc = lax.broadcasted_iota(jnp.int32, (2 * t - qb, qb), 0)
        il = lax.broadcasted_iota(jnp.int32, (2 * t - qb, qb), 1)
        tab_ref[...] = (-slope2 * jnp.abs(cc - (t - qb) - il).astype(F32)
                        - jnp.where(fast, shift, 0.0))

    def fold8(x, op):
        return op(x.reshape(x.shape[0] // 8, 8, x.shape[1]), axis=0)

    def block(j, slot, exact_max):
        q = q_ref[pl.ds(pl.multiple_of(j * qb, qb), qb), :]
        zero = jnp.zeros_like(q)
        q2 = jnp.concatenate([jnp.where(lo_mask, q, zero), jnp.where(lo_mask, zero, q)], axis=0)
        off = pl.multiple_of(t - qb - j * qb, qb)

        def scores(c):
            s = _dot_nt(k_ref[c * kc:(c + 1) * kc, :], q2)
            bias = tab_ref[pl.ds(off + c * kc, kc), :]
            return s + jnp.concatenate([bias, bias], axis=1)

        m = None
        if exact_max:
            for c in range(t // kc):
                cm = fold8(scores(c), jnp.max)
                m = cm if m is None else jnp.maximum(m, cm)
            m = jnp.max(m, axis=0, keepdims=True)
        acc = None
        for c in range(t // kc):
            s = scores(c)
            e = jnp.exp2(s - m if exact_max else s)
            et_ref[slot, c * kc:(c + 1) * kc, :] = e.astype(BF16)
            part = fold8(e, jnp.sum)
            acc = part if acc is None else acc + part
        l_ref[pl.ds(j, 1), :] = jnp.sum(acc, axis=0, keepdims=True)
        ot_ref[j] = _dot(vt_ref[...], et_ref[slot])

    def finish(j):
        ot = ot_ref[j]
        inv = 1.0 / l_ref[pl.ds(j, 1), :]
        o = ot[:, :qb] * inv[:, :qb] - ot[:, qb:] * (lam * inv[:, qb:])
        o = _rms(o.T, sub_ref[...]) * (1.0 - LAMBDA_INIT)
        o_ref[pl.ds(pl.multiple_of(j * qb, qb), qb), :] = o.astype(BF16)

    def run(exact_max):
        for s in range(inflight):
            block(s, s, exact_max)

        def step(i, carry):
            for s in range(inflight):
                finish(inflight * (i - 1) + s)
            for s in range(inflight):
                block(inflight * i + s, s, exact_max)
            return carry
        lax.fori_loop(1, nblk // inflight, step, 0)
        for s in range(inflight):
            finish(nblk - inflight + s)

    @pl.when(fast)
    def _():
        run(False)

    @pl.when(jnp.logical_not(fast))
    def _():
        run(True)


def _attn_a(qkv_nat, ctl, lq1, lk1, lq2, lk2, subln, b, t, qb=256, kc=512, inflight=4):
    n = b * t
    nblk = t // qb
    vec = lambda w: _const_spec((1, w))
    return pl.pallas_call(
        functools.partial(_attn_a_kernel, t=t, qb=qb, kc=kc, inflight=inflight),
        grid=(A_HEADS, b),
        in_specs=[
            pl.BlockSpec(memory_space=pltpu.SMEM),
            pl.BlockSpec((None, t, LANES), lambda h, i: (h, i, 0)),
            pl.BlockSpec((None, t, LANES), lambda h, i: (A_HEADS + h, i, 0)),
            pl.BlockSpec((None, t, LANES), lambda h, i: (2 * A_HEADS + h, i, 0)),
            vec(A_HEAD_DIM), vec(A_HEAD_DIM), vec(A_HEAD_DIM), vec(A_HEAD_DIM),
            vec(A_V_DIM),
        ],
        out_specs=pl.BlockSpec((None, t, LANES), lambda h, i: (h, i, 0)),
        out_shape=jax.ShapeDtypeStruct((A_HEADS, n, LANES), BF16),
        scratch_shapes=[
            pltpu.VMEM((LANES, t), BF16),
            pltpu.VMEM((2 * t - qb, qb), F32),
            pltpu.VMEM((inflight, t, 2 * qb), BF16),
            pltpu.VMEM((nblk, LANES, 2 * qb), F32),
            pltpu.VMEM((nblk, 2 * qb), F32),
        ],
        compiler_params=pltpu.CompilerParams(
            dimension_semantics=("arbitrary", "arbitrary"), vmem_limit_bytes=VMEM_LIMIT_BYTES),
        name="attn_a",
    )(ctl, qkv_nat, qkv_nat, qkv_nat, lq1, lk1, lq2, lk2, subln)


def _attn_b_kernel(ctl_ref, q0_ref, k0_ref, v0_ref, q1_ref, k1_ref, v1_ref, q2_ref, k2_ref, v2_ref,
                   o_ref, acc_ref, den_ref, kt_ref, *, t, qb):
    h = pl.program_id(1)
    n_ctl = len(B_GROUPS) * B_HEADS
    shift = ctl_ref[n_ctl]
    fast = ctl_ref[n_ctl + 1] > 0.5
    refs = ((q0_ref, k0_ref, v0_ref), (q1_ref, k1_ref, v1_ref), (q2_ref, k2_ref, v2_ref))

    def blocks(g):
        window, dil = B_GROUPS[g]
        n_side = window // (2 * dil)
        sub = t // dil
        wk = min(2 * qb, sub)
        for r in range(dil):
            for j in range(sub // qb):
                i0 = j * qb
                ws = min(max(i0 - n_side, 0), sub - wk)
                yield r, i0, r * sub + i0, r * sub + ws, wk, i0 - ws, n_side

    def bias_table(g, wk, offset, n_side, sub_shift):
        slope2 = ctl_ref[g * B_HEADS + h] * (float(B_GROUPS[g][1]) * LOG2E)
        ql = lax.broadcasted_iota(jnp.int32, (qb, wk), 0)
        kl = lax.broadcasted_iota(jnp.int32, (qb, wk), 1)
        dist = jnp.abs(kl - ql - offset)
        return jnp.where(dist <= n_side, -slope2 * dist.astype(F32) - sub_shift, NEG_BIG)

    def store_rows(ref, g, r, i0, val):
        dil = B_GROUPS[g][1]
        if dil == 1:
            ref[g, i0:i0 + qb, :] = val
        else:
            ref[g, pl.ds(r + dil * i0, qb, stride=dil), :] = val

    @pl.when(fast)
    def _():
        half = qb // 2
        for g in (1, 2, 0):
            q_ref, k_ref, v_ref = refs[g]
            kt_ref[0] = k_ref[...].T
            if any(krow % qb for _, _, _, krow, _, _, _ in blocks(g)):
                kt_ref[1] = jnp.concatenate([k_ref[half:, :], k_ref[:half, :]], axis=0).T
            tables = {}
            for r, i0, qrow, krow, wk, offset, n_side in blocks(g):
                if offset not in tables:
                    tables[offset] = bias_table(g, wk, offset, n_side, shift)
                if krow % qb == 0:
                    kwt = kt_ref[0, :, krow:krow + wk]
                else:
                    kwt = kt_ref[1, :, krow - half:krow - half + wk]
                s = _dot(q_ref[qrow:qrow + qb, :], kwt)
                e = jnp.exp2(s + tables[offset])
                den = jnp.sum(e, axis=-1, keepdims=True)
                acc = _dot(e.astype(BF16), v_ref[krow:krow + wk, :])
                if B_GROUPS[g][1] == 1:
                    rows = slice(i0, i0 + qb)
                    acc = acc + acc_ref[1, rows, :] + acc_ref[2, rows, :]
                    den = den + den_ref[1, rows, :] + den_ref[2, rows, :]
                    o_ref[rows, :] = (acc / den).astype(BF16)
                else:
                    store_rows(acc_ref, g, r, i0, acc)
                    store_rows(den_ref, g, r, i0, jnp.broadcast_to(den, (qb, LANES)))

    @pl.when(jnp.logical_not(fast))
    def _():
        for g in range(len(B_GROUPS)):
            q_ref, k_ref, v_ref = refs[g]
            tables = {}
            for r, i0, qrow, krow, wk, offset, n_side in blocks(g):
                if offset not in tables:
                    tables[offset] = bias_table(g, wk, offset, n_side, 0.0)
                s = _dot_nt(q_ref[qrow:qrow + qb, :], k_ref[krow:krow + wk, :]) + tables[offset]
                m = jnp.max(s, axis=-1, keepdims=True)
                e = jnp.exp2(s - m)
                l = jnp.sum(e, axis=-1, keepdims=True)
                o = _dot(e.astype(BF16), v_ref[krow:krow + wk, :]) * (1.0 / l)
                store_rows(acc_ref, g, r, i0, o)
                store_rows(den_ref, g, r, i0, jnp.broadcast_to(m + jnp.log2(l), (qb, LANES)))
        l0, l1, l2 = den_ref[0], den_ref[1], den_ref[2]
        m = jnp.maximum(jnp.maximum(l0, l1), l2)
        e0, e1, e2 = jnp.exp2(l0 - m), jnp.exp2(l1 - m), jnp.exp2(l2 - m)
        o_ref[...] = ((e0 * acc_ref[0] + e1 * acc_ref[1] + e2 * acc_ref[2]) / (e0 + e1 + e2)).astype(BF16)


def _attn_b(qkv_nat, qkv_g1, qkv_g2, ctl, b, t, qb=128):
    n = b * t
    blk = lambda off: pl.BlockSpec((None, t, LANES), lambda i, h: (off + h, i, 0))
    return pl.pallas_call(
        functools.partial(_attn_b_kernel, t=t, qb=qb),
        grid=(b, B_HEADS),
        in_specs=[
            pl.BlockSpec(memory_space=pltpu.SMEM),
            blk(3 * HEADS_PER_CHUNK), blk(4 * HEADS_PER_CHUNK), blk(5 * HEADS_PER_CHUNK),
            blk(0), blk(HEADS_PER_CHUNK), blk(2 * HEADS_PER_CHUNK),
            blk(0), blk(HEADS_PER_CHUNK), blk(2 * HEADS_PER_CHUNK),
        ],
        out_specs=pl.BlockSpec((None, t, LANES), lambda i, h: (h, i, 0)),
        out_shape=jax.ShapeDtypeStruct((B_HEADS, n, LANES), BF16),
        scratch_shapes=[
            pltpu.VMEM((len(B_GROUPS), t, LANES), F32),
            pltpu.VMEM((len(B_GROUPS), t, LANES), F32),
            pltpu.VMEM((2, LANES, t), BF16),
        ],
        compiler_params=pltpu.CompilerParams(
            dimension_semantics=("arbitrary", "arbitrary"), vmem_limit_bytes=VMEM_LIMIT_BYTES),
        name="attn_b",
    )(ctl, qkv_nat, qkv_nat, qkv_nat, qkv_g1, qkv_g1, qkv_g1, qkv_g2, qkv_g2, qkv_g2)


def _merge_kernel(x_ref, a_ref, b_ref, cq_ref, mem_ref, gm_ref, wkv_ref, gk_ref, g_ref, wg_ref, bg_ref, wb_ref,
                  wo_ref, o_ref, ckv_ref, *, tiles_per_batch):
    half = C_HEADS * C_HEAD_DIM

    @pl.when(pl.program_id(0) % tiles_per_batch == 0)
    def _():
        mn = _rms(mem_ref[0], gm_ref[...]).astype(BF16)
        kv = _dot(mn, wkv_ref[...])
        gk = gk_ref[MEM_K_GAIN_ROW:MEM_K_GAIN_ROW + 1, :]
        ckv_ref[:, :half] = _head_norm(kv[:, :half], gk, "full").astype(BF16)
        ckv_ref[:, half:] = kv[:, half:].astype(BF16)

    x = x_ref[...]
    hb = _rms(x, g_ref[...]).astype(BF16)
    d = x.shape[-1]

    def gated(g, branch):
        z = _dot(hb, wg_ref[:, g * d:(g + 1) * d]) + bg_ref[:, g * d:(g + 1) * d]
        gate = 0.5 * jnp.tanh(0.5 * z) + 0.5
        return gate * _dot(branch, wb_ref[g])

    wide = lambda ref: jnp.concatenate([ref[hh] for hh in range(HEADS_PER_CHUNK)], axis=1)
    acc = gated(0, wide(a_ref)) + gated(1, wide(b_ref))
    heads = []
    for hh in range(C_HEADS):
        cols = slice(hh * C_HEAD_DIM, (hh + 1) * C_HEAD_DIM)
        s = _dot_nt(cq_ref[hh], ckv_ref[:, cols])
        e = jnp.exp2(s - jnp.max(s, axis=-1, keepdims=True))
        inv = 1.0 / jnp.sum(e, axis=-1, keepdims=True)
        heads.append(_dot(e.astype(BF16), ckv_ref[:, half + hh * C_HEAD_DIM:half + (hh + 1) * C_HEAD_DIM]) * inv)
    acc = acc + gated(2, jnp.concatenate(heads, axis=1).astype(BF16))
    o_ref[...] = x + _dot(acc.astype(BF16), wo_ref[...])


def _merge(x2, out_a, out_b, qkv_nat, mem, gm, w_kv, gains, norm_g, w_gate, b_gate, w_branch, w_out, t, tm=512):
    n, d = x2.shape
    n_mem, kv_cols = mem.shape[1], w_kv.shape[1]
    row = lambda w: pl.BlockSpec((tm, w), lambda i: (i, 0))
    heads = lambda chunk: pl.BlockSpec((HEADS_PER_CHUNK, tm, LANES), lambda i: (chunk, i, 0))
    return pl.pallas_call(
        functools.partial(_merge_kernel, tiles_per_batch=t // tm),
        grid=(n // tm,),
        in_specs=[
            row(d), heads(0), heads(0), heads(NAT_HEADS // HEADS_PER_CHUNK - 1),
            pl.BlockSpec((1, n_mem, d), lambda i: (i // (t // tm), 0, 0)),
            _const_spec((1, d)), _const_spec((d, kv_cols)), _const_spec(gains.shape),
            _const_spec((1, d)),
            _const_spec((d, N_BRANCHES * d)),
            _const_spec((1, N_BRANCHES * d)),
            _const_spec((N_BRANCHES, BRANCH_WIDTH, d)),
            _const_spec((d, d)),
        ],
        out_specs=row(d),
        out_shape=jax.ShapeDtypeStruct((n, d), F32),
        scratch_shapes=[pltpu.VMEM((n_mem, kv_cols), BF16)],
        compiler_params=pltpu.CompilerParams(
            dimension_semantics=("arbitrary",), vmem_limit_bytes=VMEM_LIMIT_BYTES),
        name="merge",
    )(x2, out_a, out_b, qkv_nat, mem, gm, w_kv, gains, norm_g, w_gate, b_gate, w_branch, w_out)


def _ffn_kernel(x_ref, g_ref, wg_ref, wu_ref, wd_ref, o_ref, *, chunks):
    x = x_ref[...]
    hb = _rms(x, g_ref[...]).astype(BF16)
    acc = x
    for c0, c1 in chunks:
        gt = _dot(hb, wg_ref[:, c0:c1])
        up = _dot(hb, wu_ref[:, c0:c1])
        half_gt = 0.5 * gt
        acc = acc + _dot(((half_gt * jnp.tanh(half_gt) + half_gt) * up).astype(BF16), wd_ref[c0:c1, :])
    o_ref[...] = acc


def _ffn(x2, norm_g, w_gate, w_up, w_down, tm=512, fc=768):
    n, d = x2.shape
    d_ff = w_gate.shape[1]
    chunks = tuple((c, min(c + fc, d_ff)) for c in range(0, d_ff, fc))
    row = pl.BlockSpec((tm, d), lambda i: (i, 0))
    return pl.pallas_call(
        functools.partial(_ffn_kernel, chunks=chunks),
        grid=(n // tm,),
        in_specs=[row, _const_spec((1, d)), _const_spec((d, d_ff)), _const_spec((d, d_ff)),
                  _const_spec((d_ff, d))],
        out_specs=row,
        out_shape=jax.ShapeDtypeStruct((n, d), F32),
        compiler_params=pltpu.CompilerParams(
            dimension_semantics=("arbitrary",), vmem_limit_bytes=VMEM_LIMIT_BYTES),
        name="ffn",
    )(x2, norm_g, w_gate, w_up, w_down)


def _score_ctl(slopes, head_dim, gq_max, gk_max):
    bound = math.sqrt(head_dim) * gq_max * gk_max
    return jnp.concatenate([slopes.reshape(-1),
                            jnp.stack([bound * LOG2E, (bound <= MAX_SAFE_SCORE_BOUND).astype(F32)])])


def kernel(x, mem, norm_mix, w_in, w_gate, b_gate, a_q_norm, a_k_norm, a_lambda_q1, a_lambda_k1, a_lambda_q2,
           a_lambda_k2, a_subln, b_q_norm, b_k_norm, mem_norm, w_mem_kv, c_q_norm, c_k_norm, w_branch, w_out,
           norm_ffn, w_ffn_gate, w_ffn_up, w_ffn_down):
    b, t, d = x.shape
    n = b * t
    n_groups = len(B_GROUPS)
    slopes_a = jnp.exp2(-ALIBI_MAX_BIAS * jnp.arange(1, A_HEADS + 1, dtype=F32) / A_HEADS)
    nb = n_groups * B_HEADS
    slopes_b = jnp.exp2(-ALIBI_MAX_BIAS * jnp.arange(1, nb + 1, dtype=F32) / nb)

    l = 0
    bw = BRANCH_WIDTH
    row = lambda v: v.reshape(1, -1)
    tiled = lambda v: jnp.tile(v, bw // v.shape[0])
    ones = jnp.ones((bw,), F32)
    a_qs = A_HEAD_DIM ** -0.5 * LOG2E
    b_qs = B_HEAD_DIM ** -0.5 * LOG2E
    c_qs = C_HEAD_DIM ** -0.5 * LOG2E
    gains = jnp.stack([tiled(a_q_norm[l]) * a_qs, tiled(a_k_norm[l]), ones, tiled(b_q_norm[l]) * b_qs,
                       tiled(b_k_norm[l]), tiled(c_q_norm[l]) * c_qs, tiled(c_k_norm[l])])
    pad = lambda v: jnp.pad(v, (0, B_HEAD_DIM - v.shape[0]))
    gmax = jnp.max(jnp.abs(jnp.stack([pad(a_q_norm[l]), pad(a_k_norm[l]), b_q_norm[l], b_k_norm[l]])), axis=1)

    qkv_nat, qkv_g1, qkv_g2 = _project(x, row(norm_mix[l]), w_in[l], gains)
    qkv_nat = qkv_nat.reshape(NAT_HEADS, n, LANES)
    qkv_g1 = qkv_g1.reshape(GRP_HEADS, n, LANES)
    qkv_g2 = qkv_g2.reshape(GRP_HEADS, n, LANES)

    out_a = _attn_a(qkv_nat, _score_ctl(slopes_a, A_HEAD_DIM, gmax[0], gmax[1]),
                    row(a_lambda_q1[l]), row(a_lambda_k1[l]), row(a_lambda_q2[l]), row(a_lambda_k2[l]),
                    row(a_subln[l]), b, t)
    out_b = _attn_b(qkv_nat, qkv_g1, qkv_g2, _score_ctl(slopes_b, B_HEAD_DIM, gmax[2], gmax[3]), b, t)
    x2 = x.reshape(n, d)
    x2 = _merge(x2, out_a, out_b, qkv_nat, mem, row(mem_norm[l]), w_mem_kv[l], gains,
                row(norm_mix[l]), w_gate[l], row(b_gate[l]), w_branch[l], w_out[l], t)
    x2 = _ffn(x2, row(norm_ffn[l]), w_ffn_gate[l], w_ffn_up[l], w_ffn_down[l])
    return x2.reshape(b, t, d)
```
